```python
import jax
import jax.numpy as jnp
from jax import lax
import numpy as np


D_MODEL = 2048
BATCH = 4
SEQ = 2048
DEPTH = 2
DEC_BATCH = 8
DEC_SEQ = 2048
PAST_LEN = 128

GRID_W = 64
N_EVEN = (DEPTH + 1) // 2
N_ODD = DEPTH // 2
MIX_W = D_MODEL
EPS = 1e-6
ROPE_THETA = 10000.0
Q_BLOCK = 128

MLA_H = 8
MLA_NOPE = 128
MLA_ROPE = 64
MLA_V = 128
MLA_Q_LORA = D_MODEL // 4
MLA_KV_LORA = D_MODEL // 4

HG_H = 8
HG_DK = 128
HG_DV = (MIX_W - MLA_H * MLA_V) // HG_H
HG_CHUNK = 32

WIN_H = 16
WIN_KV = 2
WIN_HD = 64
WINDOW = 128
WIN_BLOCK = 128

AX_H = 8
AX_KV = 2
AX_HD = (MIX_W - WIN_H * WIN_HD) // AX_H

FF_DENSE = 5632
N_EXPERTS = 8
TOP_K = 2
FF_EXPERT = 1408

IN_EVEN = MLA_Q_LORA + MLA_KV_LORA + MLA_ROPE + 3 * HG_H * HG_DK + 2 * HG_H * HG_DV
IN_ODD = (WIN_H + 2 * WIN_KV) * WIN_HD + (AX_H + 2 * AX_KV) * AX_HD

kernel_name = 'hybrid_mla_hgrn2_swa_axial_encoder'


def _split(t, sizes):
    idx = np.cumsum(np.asarray(sizes))[:-1].tolist()
    return jnp.split(t, idx, axis=-1)


def _rms_norm(x, g):
    xf = x.astype(jnp.float32)
    y = xf * lax.rsqrt(jnp.mean(xf * xf, axis=-1, keepdims=True) + EPS)
    return (y * g.astype(jnp.float32)).astype(x.dtype)


def _rope(x, pos):
    d = x.shape[-1]
    inv = ROPE_THETA ** (-jnp.arange(0, d, 2, dtype=jnp.float32) / d)
    ang = pos[:, None] * inv[None, :]
    cos = jnp.cos(ang)[None, :, None, :]
    sin = jnp.sin(ang)[None, :, None, :]
    xf = x.astype(jnp.float32)
    x1, x2 = xf[..., : d // 2], xf[..., d // 2:]
    return jnp.concatenate([x1 * cos - x2 * sin, x1 * sin + x2 * cos], axis=-1).astype(x.dtype)


def _axial_rope(x, row, col):
    half = x.shape[-1] // 2
    return jnp.concatenate([_rope(x[..., :half], row), _rope(x[..., half:], col)], axis=-1)


def _alibi_slopes(n):
    return jnp.asarray(2.0 ** (-8.0 * np.arange(1, n + 1) / n), dtype=jnp.float32)


def _block_attention(q, k, v, scale):
    B, T, H, dq = q.shape
    KV = k.shape[2]
    G = H // KV
    dv = v.shape[-1]
    nb = T // Q_BLOCK
    qb = q.reshape(B, nb, Q_BLOCK, KV, G, dq).transpose(1, 0, 2, 3, 4, 5)

    def one(qblk):
        s = jnp.einsum('bqkgd,bskd->bkgqs', qblk, k).astype(jnp.float32) * scale
        p = jax.nn.softmax(s, axis=-1).astype(v.dtype)
        return jnp.einsum('bkgqs,bskd->bqkgd', p, v)

    o = lax.map(one, qb)
    return o.transpose(1, 0, 2, 3, 4, 5).reshape(B, T, H, dv)


def _window_attention(q, k, v, sink, slopes):
    B, T, H, d = q.shape
    KV = k.shape[2]
    G = H // KV
    WB = WIN_BLOCK
    nb = T // WB
    qb = q.reshape(B, nb, WB, KV, G, d).transpose(1, 0, 2, 3, 4, 5)

    def neighbours(t):
        tp = jnp.pad(t, ((0, 0), (WB, WB), (0, 0), (0, 0))).reshape(B, nb + 2, WB, KV, d)
        return jnp.concatenate([tp[:, :-2], tp[:, 1:-1], tp[:, 2:]], axis=2).transpose(1, 0, 2, 3, 4)

    kn = neighbours(k)
    vn = neighbours(v)
    qpos = jnp.arange(T).reshape(nb, WB)
    kpos = (jnp.arange(nb)[:, None] - 1) * WB + jnp.arange(3 * WB)[None, :]
    dist = jnp.abs(qpos[:, :, None] - kpos[:, None, :])
    valid = (dist <= WINDOW) & (kpos >= 0)[:, None, :] & (kpos < T)[:, None, :]
    slope_kg = slopes.reshape(KV, G)[None, :, :, None, None]
    sink_l = sink.astype(jnp.float32).reshape(KV, G)[None, :, :, None, None]
    scale = d ** -0.5

    def one(args):
        qblk, kblk, vblk, dblk, vld = args
        s = jnp.einsum('bqkgd,bskd->bkgqs', qblk, kblk).astype(jnp.float32) * scale
        s = s - slope_kg * dblk.astype(jnp.float32)[None, None, None]
        s = jnp.where(vld[None, None, None], s, -jnp.inf)
        m = jnp.maximum(jnp.max(s, axis=-1, keepdims=True), sink_l)
        e = jnp.exp(s - m)
        p = e / (jnp.sum(e, axis=-1, keepdims=True) + jnp.exp(sink_l - m))
        return jnp.einsum('bkgqs,bskd->bqkgd', p.astype(vblk.dtype), vblk)

    o = lax.map(one, (qb, kn, vn, dist, valid))
    return o.transpose(1, 0, 2, 3, 4, 5).reshape(B, T, H, d)


def _gla_scan(q, k, v, log_f):
    B, H, T, DK = q.shape
    DV = v.shape[-1]
    nc = T // HG_CHUNK

    def chunks(t):
        return t.reshape(B, H, nc, HG_CHUNK, t.shape[-1]).transpose(2, 0, 1, 3, 4)

    tri = jnp.tril(jnp.ones((HG_CHUNK, HG_CHUNK), dtype=bool))[:, :, None]

    def step(S, inp):
        qc, kc, vc, gc = inp
        b = jnp.cumsum(gc, axis=-2)
        o_inter = jnp.einsum('bhtd,bhde->bhte', qc * jnp.exp(b), S)
        diff = b[:, :, :, None, :] - b[:, :, None, :, :]
        decay = jnp.where(tri, jnp.exp(jnp.where(tri, diff, 0.0)), 0.0)
        scores = jnp.einsum('bhtd,bhsd,bhtsd->bhts', qc, kc, decay)
        o_intra = jnp.einsum('bhts,bhse->bhte', scores, vc)
        b_last = b[:, :, -1:, :]
        S = jnp.exp(b_last[:, :, 0, :])[..., None] * S + jnp.einsum('bhsd,bhse->bhde', kc * jnp.exp(b_last - b), vc)
        return S, o_inter + o_intra

    S0 = jnp.zeros((B, H, DK, DV), jnp.float32)
    _, o = lax.scan(step, S0, (chunks(q), chunks(k), chunks(v), chunks(log_f)))
    return o.transpose(1, 2, 0, 3, 4).reshape(B, H, T, DV)


def _hgrn2(hq, f_fwd, f_bwd, hi, hg, lb_fwd, lb_bwd, out_norm, j):
    B, T, _ = hq.shape

    def heads(t):
        return t.astype(jnp.float32).reshape(B, T, HG_H, -1).transpose(0, 2, 1, 3)

    q = heads(jax.nn.silu(hq))
    v = heads(hi)

    def one_direction(f_raw, lb_tab, reverse):
        lb = jnp.cumsum(jax.nn.softmax(lb_tab.astype(jnp.float32), axis=0), axis=0)[j]
        gate = lb + (1.0 - lb) * jax.nn.sigmoid(f_raw.astype(jnp.float32))
        k = heads(1.0 - gate)
        lf = heads(jnp.log(gate))
        if reverse:
            o = _gla_scan(jnp.flip(q, 2), jnp.flip(k, 2), jnp.flip(v, 2), jnp.flip(lf, 2))
            return jnp.flip(o, 2)
        return _gla_scan(q, k, v, lf)

    o = one_direction(f_fwd, lb_fwd, False) + one_direction(f_bwd, lb_bwd, True)
    o = o.transpose(0, 2, 1, 3)
    o = _rms_norm(o, out_norm) * jax.nn.silu(hg.astype(jnp.float32).reshape(B, T, HG_H, HG_DV))
    return o.reshape(B, T, HG_H * HG_DV).astype(hq.dtype)


def _mixer_even(h, j, w_in, q_norm, w_uq, kv_norm, w_ukv, lb_fwd, lb_bwd, out_norm, w_out):
    B, T, _ = h.shape
    proj = h @ w_in
    c_q, c_kv, k_r, hq, f_fwd, f_bwd, hi, hg = _split(
        proj, [MLA_Q_LORA, MLA_KV_LORA, MLA_ROPE, HG_H * HG_DK, HG_H * HG_DK, HG_H * HG_DK,
               HG_H * HG_DV, HG_H * HG_DV])
    pos = jnp.arange(T, dtype=jnp.float32)
    q = (_rms_norm(c_q, q_norm) @ w_uq).reshape(B, T, MLA_H, MLA_NOPE + MLA_ROPE)
    kv = (_rms_norm(c_kv, kv_norm) @ w_ukv).reshape(B, T, MLA_H, MLA_NOPE + MLA_V)
    q_nope, q_rope = q[..., :MLA_NOPE], _rope(q[..., MLA_NOPE:], pos)
    k_nope, v = kv[..., :MLA_NOPE], kv[..., MLA_NOPE:]
    k_rope = jnp.broadcast_to(_rope(k_r[:, :, None, :], pos), (B, T, MLA_H, MLA_ROPE))
    qf = jnp.concatenate([q_nope, q_rope], axis=-1)
    kf = jnp.concatenate([k_nope, k_rope], axis=-1)
    o_a = _block_attention(qf, kf, v, (MLA_NOPE + MLA_ROPE) ** -0.5).reshape(B, T, MLA_H * MLA_V)
    o_b = _hgrn2(hq, f_fwd, f_bwd, hi, hg, lb_fwd, lb_bwd, out_norm, j)
    return jnp.concatenate([o_a, o_b], axis=-1) @ w_out


def _mixer_odd(h, w_in, sink, q_norm, k_norm, w_out):
    B, T, _ = h.shape
    proj = h @ w_in
    qc, kc, vc, qd, kd, vd = _split(
        proj, [WIN_H * WIN_HD, WIN_KV * WIN_HD, WIN_KV * WIN_HD, AX_H * AX_HD, AX_KV * AX_HD, AX_KV * AX_HD])
    o_c = _window_attention(qc.reshape(B, T, WIN_H, WIN_HD), kc.reshape(B, T, WIN_KV, WIN_HD),
                            vc.reshape(B, T, WIN_KV, WIN_HD), sink, _alibi_slopes(WIN_H))
    o_c = o_c.reshape(B, T, WIN_H * WIN_HD)
    rows = T // GRID_W
    row = jnp.repeat(jnp.arange(rows, dtype=jnp.float32), GRID_W)
    col = jnp.tile(jnp.arange(GRID_W, dtype=jnp.float32), rows)
    qh = _axial_rope(_rms_norm(qd.reshape(B, T, AX_H, AX_HD), q_norm), row, col)
    kh = _axial_rope(_rms_norm(kd.reshape(B, T, AX_KV, AX_HD), k_norm), row, col)
    o_d = _block_attention(qh, kh, vd.reshape(B, T, AX_KV, AX_HD), AX_HD ** -0.5).reshape(B, T, AX_H * AX_HD)
    return jnp.concatenate([o_c, o_d], axis=-1) @ w_out


def _swiglu(h, w_gate, w_up, w_down):
    return (jax.nn.silu(h @ w_gate) * (h @ w_up)) @ w_down


def _moe(h, router, w_gate, w_up, w_down):
    B, T, D = h.shape
    xt = h.reshape(B * T, D)
    logits = (xt @ router).astype(jnp.float32)
    top_v, top_i = lax.top_k(logits, TOP_K)
    wts = jax.nn.softmax(top_v, axis=-1)
    gates = jnp.sum(jax.nn.one_hot(top_i, N_EXPERTS, dtype=jnp.float32) * wts[..., None], axis=1)
    out = jnp.zeros_like(xt)
    for e in range(N_EXPERTS):
        y = _swiglu(xt, w_gate[e], w_up[e], w_down[e])
        out = out + y * gates[:, e:e + 1].astype(xt.dtype)
    return out.reshape(B, T, D)


def _forward(x, norm_mix_e, w_in_e, mla_q_norm, mla_w_uq, mla_kv_norm, mla_w_ukv, hg_lb_fwd, hg_lb_bwd,
             hg_out_norm, w_out_e, norm_ffn_e, ffn_w_gate, ffn_w_up, ffn_w_down, norm_mix_o, w_in_o,
             win_sink, ax_q_norm, ax_k_norm, w_out_o, norm_ffn_o, moe_router, moe_w_gate, moe_w_up,
             moe_w_down, final_norm):
    for l in range(DEPTH):
        j = l // 2
        if l % 2 == 0:
            x = x + _mixer_even(_rms_norm(x, norm_mix_e[j]), j, w_in_e[j], mla_q_norm[j], mla_w_uq[j],
                                mla_kv_norm[j], mla_w_ukv[j], hg_lb_fwd, hg_lb_bwd, hg_out_norm[j], w_out_e[j])
            x = x + _swiglu(_rms_norm(x, norm_ffn_e[j]), ffn_w_gate[j], ffn_w_up[j], ffn_w_down[j])
        else:
            x = x + _mixer_odd(_rms_norm(x, norm_mix_o[j]), w_in_o[j], win_sink[j], ax_q_norm[j],
                               ax_k_norm[j], w_out_o[j])
            x = x + _moe(_rms_norm(x, norm_ffn_o[j]), moe_router[j], moe_w_gate[j], moe_w_up[j], moe_w_down[j])
    return _rms_norm(x, final_norm)


def setup_inputs(seed: int = 0) -> dict:
    key = jax.random.key(seed)
    ks = jax.random.split(key, 28)
    D = D_MODEL

    def dense(k, shape, fan_in):
        return jax.random.normal(k, shape, jnp.float32) * (fan_in ** -0.5)

    def gain(k, shape):
        return 1.0 + 0.02 * jax.random.normal(k, shape, jnp.float32)

    return {
        'x_prompt': jax.random.normal(ks[0], (BATCH, SEQ, D), jnp.float32),
        'x_sample': jax.random.normal(ks[1], (DEC_BATCH, DEC_SEQ, D), jnp.float32),
        'norm_mix_e': gain(ks[2], (N_EVEN, D)),
        'w_in_e': dense(ks[3], (N_EVEN, D, IN_EVEN), D),
        'mla_q_norm': gain(ks[4], (N_EVEN, MLA_Q_LORA)),
        'mla_w_uq': dense(ks[5], (N_EVEN, MLA_Q_LORA, MLA_H * (MLA_NOPE + MLA_ROPE)), MLA_Q_LORA),
        'mla_kv_norm': gain(ks[6], (N_EVEN, MLA_KV_LORA)),
        'mla_w_ukv': dense(ks[7], (N_EVEN, MLA_KV_LORA, MLA_H * (MLA_NOPE + MLA_V)), MLA_KV_LORA),
        'hg_lb_fwd': 0.5 * jax.random.normal(ks[8], (N_EVEN + 1, HG_H * HG_DK), jnp.float32),
        'hg_lb_bwd': 0.5 * jax.random.normal(ks[9], (N_EVEN + 1, HG_H * HG_DK), jnp.float32),
        'hg_out_norm': gain(ks[10], (N_EVEN, HG_DV)),
        'w_out_e': dense(ks[11], (N_EVEN, MIX_W, D), MIX_W),
        'norm_ffn_e': gain(ks[12], (N_EVEN, D)),
        'ffn_w_gate': dense(ks[13], (N_EVEN, D, FF_DENSE), D),
        'ffn_w_up': dense(ks[14], (N_EVEN, D, FF_DENSE), D),
        'ffn_w_down': dense(ks[15], (N_EVEN, FF_DENSE, D), FF_DENSE),
        'norm_mix_o': gain(ks[16], (N_ODD, D)),
        'w_in_o': dense(ks[17], (N_ODD, D, IN_ODD), D),
        'win_sink': 0.5 * jax.random.normal(ks[18], (N_ODD, WIN_H), jnp.float32),
        'ax_q_norm': gain(ks[19], (N_ODD, AX_HD)),
        'ax_k_norm': gain(ks[20], (N_ODD, AX_HD)),
        'w_out_o': dense(ks[21], (N_ODD, MIX_W, D), MIX_W),
        'norm_ffn_o': gain(ks[22], (N_ODD, D)),
        'moe_router': dense(ks[23], (N_ODD, D, N_EXPERTS), D),
        'moe_w_gate': dense(ks[24], (N_ODD, N_EXPERTS, D, FF_EXPERT), D),
        'moe_w_up': dense(ks[25], (N_ODD, N_EXPERTS, D, FF_EXPERT), D),
        'moe_w_down': dense(ks[26], (N_ODD, N_EXPERTS, FF_EXPERT, D), FF_EXPERT),
        'final_norm': gain(ks[27], (D,)),
    }


def reference(x_prompt, x_sample, norm_mix_e, w_in_e, mla_q_norm, mla_w_uq, mla_kv_norm, mla_w_ukv,
              hg_lb_fwd, hg_lb_bwd, hg_out_norm, w_out_e, norm_ffn_e, ffn_w_gate, ffn_w_up, ffn_w_down,
              norm_mix_o, w_in_o, win_sink, ax_q_norm, ax_k_norm, w_out_o, norm_ffn_o, moe_router,
              moe_w_gate, moe_w_up, moe_w_down, final_norm):
    w = (norm_mix_e, w_in_e, mla_q_norm, mla_w_uq, mla_kv_norm, mla_w_ukv, hg_lb_fwd, hg_lb_bwd,
         hg_out_norm, w_out_e, norm_ffn_e, ffn_w_gate, ffn_w_up, ffn_w_down, norm_mix_o, w_in_o,
         win_sink, ax_q_norm, ax_k_norm, w_out_o, norm_ffn_o, moe_router, moe_w_gate, moe_w_up,
         moe_w_down, final_norm)
    y_prompt = _forward(x_prompt, *w)
    y_sample = _forward(x_sample, *w)
    return (y_prompt, y_sample)
```

```python
import functools

import jax
import jax.numpy as jnp
import numpy as np
from jax import lax
from jax.experimental import pallas as pl
from jax.experimental.pallas import tpu as pltpu

D_MODEL = 2048
DEPTH = 2
GRID_W = 64
EPS = 1e-6
ROPE_THETA = 10000.0

MLA_H = 8
MLA_NOPE = 128
MLA_ROPE = 64
MLA_V = 128
MLA_LORA = D_MODEL // 4

HG_H = 8
HG_DK = 128
HG_DV = 128

WIN_H = 16
WIN_KV = 2
WIN_HD = 64
WINDOW = 128

AX_H = 8
AX_KV = 2
AX_HD = 128

FF_DENSE = 5632
N_EXPERTS = 8
FF_EXPERT = 1408

LANES = 128
VMEM_CAP = 60000 * 1024
BF16 = jnp.bfloat16
F32 = jnp.float32

E_CQ, E_CKV, E_KRA, E_KRB, E_HQ, E_FF, E_FB, E_HI, E_HG, E_END = 0, 4, 8, 9, 10, 18, 26, 34, 42, 50
O_QD, O_QDS, O_QC, O_KD, O_KDS, O_VD, O_KC, O_VC, O_END = 0, 8, 16, 24, 26, 28, 30, 32, 36


def _cparams(sem, *block_bytes):
    need = int(sum(block_bytes)) + (6 << 20)
    return pltpu.CompilerParams(dimension_semantics=sem, vmem_limit_bytes=min(max(need, 16 << 20), VMEM_CAP))


def _nbytes(shape, dtype):
    return int(np.prod(shape)) * jnp.dtype(dtype).itemsize


def _rms_rows(x, g):
    return x * lax.rsqrt(jnp.mean(x * x, axis=-1, keepdims=True) + EPS) * g


def _norm_into(dst_ref, x_ref, g_ref, chunk=256):
    rows = x_ref.shape[0]
    chunk = min(chunk, rows)

    def body(c, carry):
        r = pl.ds(pl.multiple_of(c * chunk, chunk), chunk)
        dst_ref[r, :] = _rms_rows(x_ref[r, :].astype(F32), g_ref[...]).astype(dst_ref.dtype)
        return carry

    lax.fori_loop(0, rows // chunk, body, 0)


def _norm_matmul_kernel(x_ref, g_ref, w_ref, o_ref, xn_ref):
    @pl.when(pl.program_id(1) == 0)
    def _():
        _norm_into(xn_ref, x_ref, g_ref)

    o_ref[...] = jnp.dot(xn_ref[...], w_ref[...], preferred_element_type=F32).astype(o_ref.dtype)


def _norm_matmul(x, g, w, tn, out_dtype, tm=1024):
    n, k = x.shape
    nout = w.shape[1]
    tm = min(tm, n)
    return pl.pallas_call(
        _norm_matmul_kernel,
        grid=(n // tm, nout // tn),
        in_specs=[pl.BlockSpec((tm, k), lambda i, j: (i, 0)),
                  pl.BlockSpec((1, k), lambda i, j: (0, 0)),
                  pl.BlockSpec((k, tn), lambda i, j: (0, j))],
        out_specs=pl.BlockSpec((tm, tn), lambda i, j: (i, j)),
        out_shape=jax.ShapeDtypeStruct((n, nout), out_dtype),
        scratch_shapes=[pltpu.VMEM((tm, k), BF16)],
        compiler_params=_cparams(("parallel", "arbitrary"), 2 * _nbytes((tm, k), x.dtype), _nbytes((tm, k), BF16),
                                 2 * _nbytes((k, tn), BF16), 3 * _nbytes((tm, tn), F32)),
        name="norm_matmul",
    )(x, g.reshape(1, k), w)


def _mla_up_kernel(cq_ref, ckv_ref, kra_ref, krb_ref, qn_ref, kvn_ref, wq_ref, wkv_ref, cos_ref, sin_ref,
                   q_ref, k_ref, v_ref, cqn_ref, ckvn_ref, kr_ref, *, scale):
    @pl.when(pl.program_id(1) == 0)
    def _():
        _norm_into(cqn_ref, cq_ref, qn_ref)
        _norm_into(ckvn_ref, ckv_ref, kvn_ref)
        kr_ref[...] = (kra_ref[...] * cos_ref[...] + krb_ref[...] * sin_ref[...]).astype(BF16)

    q = jnp.dot(cqn_ref[...], wq_ref[...], preferred_element_type=F32)
    q_rope = q[:, LANES:2 * LANES] * cos_ref[...] + q[:, 2 * LANES:] * sin_ref[...]
    q_ref[:, :LANES] = (q[:, :LANES] * scale).astype(BF16)
    q_ref[:, LANES:] = (q_rope * scale).astype(BF16)
    kv = jnp.dot(ckvn_ref[...], wkv_ref[...], preferred_element_type=F32)
    k_ref[:, :LANES] = kv[:, :LANES].astype(BF16)
    k_ref[:, LANES:] = kr_ref[...]
    v_ref[...] = kv[:, LANES:].astype(BF16)


def _mla_up(proj, qn, kvn, wq, wkv, cos_t, sin_t, seq, tm=512):
    n = proj.shape[0]
    tm = min(tm, seq)
    nt = seq // tm
    lora = MLA_LORA
    scale = float((MLA_NOPE + MLA_ROPE) ** -0.5)
    return pl.pallas_call(
        functools.partial(_mla_up_kernel, scale=scale),
        grid=(n // tm, MLA_H),
        in_specs=[pl.BlockSpec((tm, lora), lambda i, h: (i, 0)),
                  pl.BlockSpec((tm, lora), lambda i, h: (i, 1)),
                  pl.BlockSpec((tm, LANES), lambda i, h: (i, E_KRA)),
                  pl.BlockSpec((tm, LANES), lambda i, h: (i, E_KRB)),
                  pl.BlockSpec((1, lora), lambda i, h: (0, 0)),
                  pl.BlockSpec((1, lora), lambda i, h: (0, 0)),
                  pl.BlockSpec((lora, 3 * LANES), lambda i, h: (0, h)),
                  pl.BlockSpec((lora, 2 * LANES), lambda i, h: (0, h)),
                  pl.BlockSpec((tm, LANES), lambda i, h: (i % nt, 0)),
                  pl.BlockSpec((tm, LANES), lambda i, h: (i % nt, 0))],
        out_specs=[pl.BlockSpec((tm, 2 * LANES), lambda i, h: (i, h)),
                   pl.BlockSpec((tm, 2 * LANES), lambda i, h: (i, h)),
                   pl.BlockSpec((tm, LANES), lambda i, h: (i, h))],
        out_shape=[jax.ShapeDtypeStruct((n, MLA_H * 2 * LANES), BF16),
                   jax.ShapeDtypeStruct((n, MLA_H * 2 * LANES), BF16),
                   jax.ShapeDtypeStruct((n, MLA_H * LANES), BF16)],
        scratch_shapes=[pltpu.VMEM((tm, lora), BF16), pltpu.VMEM((tm, lora), BF16), pltpu.VMEM((tm, LANES), BF16)],
        compiler_params=_cparams(("parallel", "arbitrary"), 4 * _nbytes((tm, lora), F32), 12 * _nbytes((tm, LANES), F32),
                                 4 * _nbytes((lora, 5 * LANES), BF16), 8 * _nbytes((tm, 3 * LANES), F32)),
        name="mla_up",
    )(proj, proj, proj, proj, qn.reshape(1, lora), kvn.reshape(1, lora), wq, wkv, cos_t, sin_t)


def _attention_kernel(q_ref, k_ref, v_ref, o_ref):
    s = lax.dot_general(q_ref[0], k_ref[0], (((1,), (1,)), ((), ())), preferred_element_type=F32)
    p = jnp.exp(s - jnp.max(s, axis=-1, keepdims=True))
    l = jnp.sum(p, axis=-1, keepdims=True)
    o = jnp.dot(p.astype(BF16), v_ref[0], preferred_element_type=F32)
    o_ref[0] = (o / l).astype(o_ref.dtype)


def _attention(q, k, v, heads, kv_heads, tq=512):
    b, t, _ = q.shape
    dq = q.shape[2] // heads
    dv = v.shape[2] // kv_heads
    g = heads // kv_heads
    tq = min(tq, t)
    return pl.pallas_call(
        _attention_kernel,
        grid=(b, heads, t // tq),
        in_specs=[pl.BlockSpec((1, tq, dq), lambda bi, h, qi: (bi, qi, h)),
                  pl.BlockSpec((1, t, dq), lambda bi, h, qi: (bi, 0, h // g)),
                  pl.BlockSpec((1, t, dv), lambda bi, h, qi: (bi, 0, h // g))],
        out_specs=pl.BlockSpec((1, tq, dv), lambda bi, h, qi: (bi, qi, h)),
        out_shape=jax.ShapeDtypeStruct((b, t, heads * dv), BF16),
        compiler_params=_cparams(("parallel", "parallel", "arbitrary"), 2 * _nbytes((tq, dq), BF16),
                                 2 * _nbytes((t, dq + dv), BF16), 2 * _nbytes((tq, dv), BF16),
                                 3 * _nbytes((tq, t), F32)),
        name="attention",
    )(q, k, v)


HG_CHUNK = 32


def _split3(x):
    a = x.astype(BF16)
    r = x - a.astype(F32)
    b = r.astype(BF16)
    c = (r - b.astype(F32)).astype(BF16)
    return a, b, c


def _hgrn2_kernel(hq_ref, ff_ref, fb_ref, hi_ref, hg_ref, lbf_ref, lbb_ref, on_ref, o_ref, acc_ref, s_ref):
    t = hq_ref.shape[1]
    c_sz = HG_CHUNK
    nc = t // c_sz
    row = lax.broadcasted_iota(jnp.int32, (c_sz, c_sz), 0)
    col = lax.broadcasted_iota(jnp.int32, (c_sz, c_sz), 1)
    row1 = lax.broadcasted_iota(jnp.int32, (c_sz, 1), 0)

    def scan(f_ref, lb_ref, reverse):
        lb = lb_ref[...]
        tri = jnp.where((row <= col) if reverse else (row >= col), 1.0, 0.0).astype(BF16)
        s_ref[...] = jnp.zeros_like(s_ref)

        def body(ci, carry):
            c = (nc - 1 - ci) if reverse else ci
            rows = pl.ds(pl.multiple_of(c * c_sz, c_sz), c_sz)
            gate = lb + (1.0 - lb) * jax.nn.sigmoid(f_ref[0, rows, :])
            lf = jnp.log(gate)
            k = 1.0 - gate
            hq = hq_ref[0, rows, :]
            q = hq * jax.nn.sigmoid(hq)
            v = hi_ref[0, rows, :]
            p0, p1, p2 = _split3(lf)
            b = (jnp.dot(tri, p0, preferred_element_type=F32) + jnp.dot(tri, p1, preferred_element_type=F32)
                 + jnp.dot(tri, p2, preferred_element_type=F32))
            b_last = b[0:1, :] if reverse else b[c_sz - 1:c_sz, :]
            s_t = s_ref[...]
            o = lax.dot_general((q * jnp.exp(b)).astype(BF16), s_t.astype(BF16), (((1,), (1,)), ((), ())),
                                preferred_element_type=F32)
            for s in range(c_sz):
                e = jnp.exp(jnp.minimum(b - b[s:s + 1, :], 0.0))
                a = jnp.sum(q * k[s:s + 1, :] * e, axis=-1, keepdims=True)
                a = jnp.where((row1 <= s) if reverse else (row1 >= s), a, 0.0)
                o = o + a * v[s:s + 1, :]
            if reverse:
                acc_ref[rows, :] = acc_ref[rows, :] + o
            else:
                acc_ref[rows, :] = o
            kd = (k * jnp.exp(b_last - b)).astype(BF16)
            upd = lax.dot_general(v.astype(BF16), kd, (((0,), (0,)), ((), ())), preferred_element_type=F32)
            s_ref[...] = jnp.exp(b_last) * s_t + upd
            return carry

        lax.fori_loop(0, nc, body, 0)

    scan(ff_ref, lbf_ref, False)
    scan(fb_ref, lbb_ref, True)

    def finish(c, carry):
        rows = pl.ds(pl.multiple_of(c * 256, 256), 256)
        hg = hg_ref[0, rows, :]
        y = _rms_rows(acc_ref[rows, :], on_ref[...]) * (hg * jax.nn.sigmoid(hg))
        o_ref[0, rows, :] = y.astype(o_ref.dtype)
        return carry

    lax.fori_loop(0, t // 256, finish, 0)


def _hgrn2(proj, lb_f, lb_b, out_norm):
    b, t, _ = proj.shape

    def col(base):
        return pl.BlockSpec((1, t, LANES), lambda bi, h: (bi, 0, base + h))

    return pl.pallas_call(
        _hgrn2_kernel,
        grid=(b, HG_H),
        in_specs=[col(E_HQ), col(E_FF), col(E_FB), col(E_HI), col(E_HG),
                  pl.BlockSpec((1, LANES), lambda bi, h: (0, h)),
                  pl.BlockSpec((1, LANES), lambda bi, h: (0, h)),
                  pl.BlockSpec((1, LANES), lambda bi, h: (0, 0))],
        out_specs=pl.BlockSpec((1, t, LANES), lambda bi, h: (bi, 0, h)),
        out_shape=jax.ShapeDtypeStruct((b, t, HG_H * HG_DV), BF16),
        scratch_shapes=[pltpu.VMEM((t, HG_DV), F32), pltpu.VMEM((HG_DV, HG_DK), F32)],
        compiler_params=_cparams(("parallel", "parallel"), 10 * _nbytes((t, LANES), F32), 3 * _nbytes((t, LANES), F32)),
        name="hgrn2",
    )(proj, proj, proj, proj, proj, lb_f, lb_b, out_norm.reshape(1, HG_DV))


def _out_proj_kernel(a_ref, b_ref, wa_ref, wb_ref, x_ref, o_ref):
    acc = jnp.dot(a_ref[...], wa_ref[...], preferred_element_type=F32)
    acc = acc + jnp.dot(b_ref[...], wb_ref[...], preferred_element_type=F32)
    o_ref[...] = x_ref[...] + acc


def _out_proj(a, b, w, x, tm=1024, tn=512):
    n, ka = a.shape
    d = w.shape[1]
    tm = min(tm, n)
    return pl.pallas_call(
        _out_proj_kernel,
        grid=(n // tm, d // tn),
        in_specs=[pl.BlockSpec((tm, ka), lambda i, j: (i, 0)),
                  pl.BlockSpec((tm, ka), lambda i, j: (i, 0)),
                  pl.BlockSpec((ka, tn), lambda i, j: (0, j)),
                  pl.BlockSpec((ka, tn), lambda i, j: (1, j)),
                  pl.BlockSpec((tm, tn), lambda i, j: (i, j))],
        out_specs=pl.BlockSpec((tm, tn), lambda i, j: (i, j)),
        out_shape=jax.ShapeDtypeStruct((n, d), F32),
        compiler_params=_cparams(("parallel", "arbitrary"), 4 * _nbytes((tm, ka), BF16), 4 * _nbytes((ka, tn), BF16),
                                 5 * _nbytes((tm, tn), F32)),
        name="out_proj",
    )(a, b, w, w, x)


def _ffn_kernel(x_ref, g_ref, wg_ref, wu_ref, wd_ref, o_ref, xn_ref):
    f = pl.program_id(1)

    @pl.when(f == 0)
    def _():
        _norm_into(xn_ref, x_ref, g_ref)
        o_ref[...] = x_ref[...]

    xn = xn_ref[...]
    gt = jnp.dot(xn, wg_ref[...], preferred_element_type=F32)
    up = jnp.dot(xn, wu_ref[...], preferred_element_type=F32)
    act = (gt * jax.nn.sigmoid(gt) * up).astype(BF16)
    o_ref[...] += jnp.dot(act, wd_ref[...], preferred_element_type=F32)


def _ffn(x, g, wg, wu, wd, tm=512, tf=512):
    n, d = x.shape
    ff = wg.shape[1]
    tm = min(tm, n)
    return pl.pallas_call(
        _ffn_kernel,
        grid=(n // tm, ff // tf),
        in_specs=[pl.BlockSpec((tm, d), lambda i, f: (i, 0)),
                  pl.BlockSpec((1, d), lambda i, f: (0, 0)),
                  pl.BlockSpec((d, tf), lambda i, f: (0, f)),
                  pl.BlockSpec((d, tf), lambda i, f: (0, f)),
                  pl.BlockSpec((tf, d), lambda i, f: (f, 0))],
        out_specs=pl.BlockSpec((tm, d), lambda i, f: (i, 0)),
        out_shape=jax.ShapeDtypeStruct((n, d), F32),
        scratch_shapes=[pltpu.VMEM((tm, d), BF16)],
        compiler_params=_cparams(("parallel", "arbitrary"), 4 * _nbytes((tm, d), F32), _nbytes((tm, d), BF16),
                                 6 * _nbytes((d, tf), BF16), 4 * _nbytes((tm, tf), F32), _nbytes((tm, d), F32)),
        name="ffn",
    )(x, g.reshape(1, d), wg, wu, wd)


def _axial_prep_kernel(qd_ref, qs_ref, kd_ref, ks_ref, vd_ref, cq_ref, sq_ref, ck_ref, sk_ref,
                       q_ref, k_ref, v_ref, *, scale):
    def rope(x_ref, xs_ref, c_ref, s_ref, h, mul):
        sl = slice(h * LANES, (h + 1) * LANES)
        x = x_ref[:, sl]
        r = lax.rsqrt(jnp.mean(x * x, axis=-1, keepdims=True) + EPS)
        return ((x * c_ref[...] + xs_ref[:, sl] * s_ref[...]) * (r * mul)).astype(BF16)

    for h in range(AX_H):
        q_ref[:, h * LANES:(h + 1) * LANES] = rope(qd_ref, qs_ref, cq_ref, sq_ref, h, scale)
    for h in range(AX_KV):
        k_ref[:, h * LANES:(h + 1) * LANES] = rope(kd_ref, ks_ref, ck_ref, sk_ref, h, 1.0)
    v_ref[...] = vd_ref[...].astype(BF16)


def _axial_prep(proj, cq, sq, ck, sk, seq, tm=512):
    n = proj.shape[0]
    tm = min(tm, seq)
    nt = seq // tm
    qw, kw = AX_H * AX_HD, AX_KV * AX_HD
    tab = pl.BlockSpec((tm, LANES), lambda i: (i % nt, 0))
    return pl.pallas_call(
        functools.partial(_axial_prep_kernel, scale=float(AX_HD ** -0.5)),
        grid=(n // tm,),
        in_specs=[pl.BlockSpec((tm, qw), lambda i: (i, O_QD * LANES // qw)),
                  pl.BlockSpec((tm, qw), lambda i: (i, O_QDS * LANES // qw)),
                  pl.BlockSpec((tm, kw), lambda i: (i, O_KD * LANES // kw)),
                  pl.BlockSpec((tm, kw), lambda i: (i, O_KDS * LANES // kw)),
                  pl.BlockSpec((tm, kw), lambda i: (i, O_VD * LANES // kw)),
                  tab, tab, tab, tab],
        out_specs=[pl.BlockSpec((tm, qw), lambda i: (i, 0)),
                   pl.BlockSpec((tm, kw), lambda i: (i, 0)),
                   pl.BlockSpec((tm, kw), lambda i: (i, 0))],
        out_shape=[jax.ShapeDtypeStruct((n, qw), BF16), jax.ShapeDtypeStruct((n, kw), BF16),
                   jax.ShapeDtypeStruct((n, kw), BF16)],
        compiler_params=_cparams(("parallel",), 4 * _nbytes((tm, qw), F32), 6 * _nbytes((tm, kw), F32),
                                 8 * _nbytes((tm, LANES), F32), 2 * _nbytes((tm, qw + 2 * kw), BF16)),
        name="axial_prep",
    )(proj, proj, proj, proj, proj, cq, sq, ck, sk)


WIN_BLOCK = 128


def _window_kernel(sink_ref, slope_ref, q_ref, k_ref, v_ref, o_ref, *, scale):
    t = q_ref.shape[1]
    wb = WIN_BLOCK
    span = 3 * wb
    pair = pl.program_id(1)
    lane = lax.broadcasted_iota(jnp.int32, (wb, LANES), 1)
    low = lane < WIN_HD
    delta = lax.broadcasted_iota(jnp.int32, (wb, span), 0) - lax.broadcasted_iota(jnp.int32, (wb, span), 1)

    def body(qb, carry):
        start = jnp.clip((qb - 1) * wb, 0, t - span)
        start = pl.multiple_of(start, wb)
        kwin = k_ref[0, pl.ds(start, span), :].astype(BF16)
        vwin = v_ref[0, pl.ds(start, span), :].astype(BF16)
        qrows = pl.ds(pl.multiple_of(qb * wb, wb), wb)
        q2 = q_ref[0, qrows, :] * scale
        dist = jnp.abs(delta + (qb * wb - start))
        valid = dist <= WINDOW
        distf = dist.astype(F32)
        outs = []
        for j in range(2):
            head = 2 * pair + j
            sink = sink_ref[head]
            qh = jnp.where(low if j == 0 else jnp.logical_not(low), q2, 0.0).astype(BF16)
            s = lax.dot_general(qh, kwin, (((1,), (1,)), ((), ())), preferred_element_type=F32)
            s = jnp.where(valid, s - slope_ref[head] * distf, -jnp.inf)
            m = jnp.maximum(jnp.max(s, axis=-1, keepdims=True), sink)
            e = jnp.exp(s - m)
            den = jnp.sum(e, axis=-1, keepdims=True) + jnp.exp(sink - m)
            outs.append(jnp.dot(e.astype(BF16), vwin, preferred_element_type=F32) / den)
        o_ref[0, qrows, :] = jnp.where(low, outs[0], outs[1]).astype(o_ref.dtype)
        return carry

    lax.fori_loop(0, t // wb, body, 0)


def _window_attention(proj, sink, slopes):
    b, t, _ = proj.shape
    pairs = WIN_H // 2
    per_kv = pairs // WIN_KV
    smem = pl.BlockSpec(memory_space=pltpu.SMEM)
    return pl.pallas_call(
        functools.partial(_window_kernel, scale=float(WIN_HD ** -0.5)),
        grid=(b, pairs),
        in_specs=[smem, smem,
                  pl.BlockSpec((1, t, LANES), lambda bi, p: (bi, 0, O_QC + p)),
                  pl.BlockSpec((1, t, LANES), lambda bi, p: (bi, 0, O_KC + p // per_kv)),
                  pl.BlockSpec((1, t, LANES), lambda bi, p: (bi, 0, O_VC + p // per_kv))],
        out_specs=pl.BlockSpec((1, t, LANES), lambda bi, p: (bi, 0, p)),
        out_shape=jax.ShapeDtypeStruct((b, t, WIN_H * WIN_HD), BF16),
        compiler_params=_cparams(("parallel", "parallel"), 6 * _nbytes((t, LANES), F32), 2 * _nbytes((t, LANES), BF16)),
        name="window_attention",
    )(sink, slopes, proj, proj, proj)


def _router_kernel(x_ref, g_ref, r_ref, xn_ref, gate_ref):
    xn = _rms_rows(x_ref[...], g_ref[...])
    xn_ref[...] = xn.astype(BF16)
    logits = [jnp.sum(xn * r_ref[e:e + 1, :], axis=-1, keepdims=True) for e in range(N_EXPERTS)]

    def top(ls):
        m = functools.reduce(jnp.maximum, ls)
        idx = jnp.full_like(m, N_EXPERTS).astype(jnp.int32)
        for e in reversed(range(N_EXPERTS)):
            idx = jnp.where(ls[e] == m, e, idx)
        return m, idx

    m1, i1 = top(logits)
    m2, i2 = top([jnp.where(i1 == e, -jnp.inf, logits[e]) for e in range(N_EXPERTS)])
    e2 = jnp.exp(m2 - m1)
    w1 = 1.0 / (1.0 + e2)
    w2 = e2 / (1.0 + e2)
    lane = lax.broadcasted_iota(jnp.int32, gate_ref.shape, 1)
    gate_ref[...] = jnp.where(lane == i1, w1, 0.0) + jnp.where(lane == i2, w2, 0.0)


def _router(x, g, router_t, tm=256):
    n, d = x.shape
    tm = min(tm, n)
    return pl.pallas_call(
        _router_kernel,
        grid=(n // tm,),
        in_specs=[pl.BlockSpec((tm, d), lambda i: (i, 0)),
                  pl.BlockSpec((1, d), lambda i: (0, 0)),
                  pl.BlockSpec((N_EXPERTS, d), lambda i: (0, 0))],
        out_specs=[pl.BlockSpec((tm, d), lambda i: (i, 0)),
                   pl.BlockSpec((tm, LANES), lambda i: (i, 0))],
        out_shape=[jax.ShapeDtypeStruct((n, d), BF16), jax.ShapeDtypeStruct((n, LANES), F32)],
        compiler_params=_cparams(("parallel",), 6 * _nbytes((tm, d), F32), 2 * _nbytes((tm, d), BF16)),
        name="router",
    )(x, g.reshape(1, d), router_t)


def _moe_expert_kernel(xn_ref, gate_ref, wg_ref, wu_ref, wd_ref, acc_ref, o_ref, *, expert):
    xn = xn_ref[...]
    gt = jnp.dot(xn, wg_ref[0], preferred_element_type=F32)
    up = jnp.dot(xn, wu_ref[0], preferred_element_type=F32)
    act = (gt * jax.nn.sigmoid(gt) * up).astype(BF16)
    y = jnp.dot(act, wd_ref[0], preferred_element_type=F32)
    lane = lax.broadcasted_iota(jnp.int32, gate_ref.shape, 1)
    w = jnp.sum(jnp.where(lane == expert, gate_ref[...], 0.0), axis=-1, keepdims=True)
    o_ref[...] = acc_ref[...] + w * y


def _moe_expert(acc, xn, gates, wg, wu, wd, expert, tm=512):
    n, d = acc.shape
    ff = wg.shape[2]
    tm = min(tm, n)
    once = dict(pipeline_mode=pl.Buffered(1))
    return pl.pallas_call(
        functools.partial(_moe_expert_kernel, expert=expert),
        grid=(n // tm,),
        in_specs=[pl.BlockSpec((tm, d), lambda i: (i, 0)),
                  pl.BlockSpec((tm, LANES), lambda i: (i, 0)),
                  pl.BlockSpec((1, d, ff), lambda i: (expert, 0, 0), **once),
                  pl.BlockSpec((1, d, ff), lambda i: (expert, 0, 0), **once),
                  pl.BlockSpec((1, ff, d), lambda i: (expert, 0, 0), **once),
                  pl.BlockSpec((tm, d), lambda i: (i, 0))],
        out_specs=pl.BlockSpec((tm, d), lambda i: (i, 0)),
        out_shape=jax.ShapeDtypeStruct((n, d), F32),
        input_output_aliases={5: 0},
        compiler_params=_cparams(("parallel",), 2 * _nbytes((tm, d), BF16), 3 * _nbytes((d, ff), BF16),
                                 5 * _nbytes((tm, d), F32), 4 * _nbytes((tm, ff), F32)),
        name="moe_expert",
    )(xn, gates, wg, wu, wd, acc)


def _final_norm_kernel(x_ref, g_ref, o_ref):
    o_ref[...] = _rms_rows(x_ref[...], g_ref[...])


def _final_norm(x, g, row0, rows, tm=512):
    d = x.shape[1]
    tm = min(tm, rows)
    off = row0 // tm
    return pl.pallas_call(
        _final_norm_kernel,
        grid=(rows // tm,),
        in_specs=[pl.BlockSpec((tm, d), lambda i: (i + off, 0)),
                  pl.BlockSpec((1, d), lambda i: (0, 0))],
        out_specs=pl.BlockSpec((tm, d), lambda i: (i, 0)),
        out_shape=jax.ShapeDtypeStruct((rows, d), F32),
        compiler_params=_cparams(("parallel",), 6 * _nbytes((tm, d), F32)),
        name="final_norm",
    )(x, g.reshape(1, d))


def _swap_halves(w, width):
    lead = w.shape[:-1]
    return jnp.flip(w.reshape(lead + (-1, 2, width // 2)), axis=-2).reshape(w.shape)


def _pad_cols(w, width):
    return jnp.pad(w, [(0, 0)] * (w.ndim - 1) + [(0, width - w.shape[-1])])


def _even_in_weight(w):
    kr = w[:, 2 * MLA_LORA:2 * MLA_LORA + MLA_ROPE]
    return jnp.concatenate([w[:, :2 * MLA_LORA], _pad_cols(kr, LANES), _pad_cols(_swap_halves(kr, MLA_ROPE), LANES),
                            w[:, 2 * MLA_LORA + MLA_ROPE:]], axis=1).astype(BF16)


def _mla_q_weight(w):
    w = w.reshape(MLA_LORA, MLA_H, MLA_NOPE + MLA_ROPE)
    rope = w[..., MLA_NOPE:]
    w = jnp.concatenate([w[..., :MLA_NOPE], _pad_cols(rope, LANES), _pad_cols(_swap_halves(rope, MLA_ROPE), LANES)], -1)
    return w.reshape(MLA_LORA, MLA_H * 3 * LANES).astype(BF16)


def _odd_in_weight(w):
    d = w.shape[0]
    o = 0
    parts = {}
    for name, width in (("qc", WIN_H * WIN_HD), ("kc", WIN_KV * WIN_HD), ("vc", WIN_KV * WIN_HD),
                        ("qd", AX_H * AX_HD), ("kd", AX_KV * AX_HD), ("vd", AX_KV * AX_HD)):
        parts[name] = w[:, o:o + width]
        o += width

    def dup(x):
        x = x.reshape(d, WIN_KV, WIN_HD)
        return jnp.concatenate([x, x], axis=-1).reshape(d, WIN_KV * LANES)

    half = AX_HD // 2
    out = jnp.concatenate([parts["qd"], _swap_halves(parts["qd"], half), parts["qc"], parts["kd"],
                           _swap_halves(parts["kd"], half), parts["vd"], dup(parts["kc"]), dup(parts["vc"])], axis=1)
    return _pad_cols(out, O_END * LANES).astype(BF16)


def _rope_tables(pos, dim):
    inv = ROPE_THETA ** (-jnp.arange(0, dim, 2, dtype=F32) / dim)
    ang = pos[:, None] * inv[None, :]
    cos, sin = jnp.cos(ang), jnp.sin(ang)
    return jnp.concatenate([cos, cos], -1), jnp.concatenate([-sin, sin], -1)


def _alibi_slopes(n):
    return jnp.asarray(2.0 ** (-8.0 * np.arange(1, n + 1) / n), dtype=F32)


def _mixer_even(x, bsz, seq, j, norm_g, w_in, q_norm, w_uq, kv_norm, w_ukv, lb_fwd, lb_bwd, out_norm, w_out):
    n = x.shape[0]
    proj = _norm_matmul(x, norm_g, _even_in_weight(w_in), tn=5 * LANES, out_dtype=F32)
    cos, sin = _rope_tables(jnp.arange(seq, dtype=F32), MLA_ROPE)
    q, k, v = _mla_up(proj, q_norm, kv_norm, _mla_q_weight(w_uq), w_ukv.astype(BF16),
                      _pad_cols(cos, LANES), _pad_cols(sin, LANES), seq)
    o_a = _attention(q.reshape(bsz, seq, -1), k.reshape(bsz, seq, -1), v.reshape(bsz, seq, -1), MLA_H, MLA_H)

    def lower_bound(tab):
        return jnp.cumsum(jax.nn.softmax(tab.astype(F32), axis=0), axis=0)[j].reshape(1, HG_H * HG_DK)

    o_b = _hgrn2(proj.reshape(bsz, seq, -1), lower_bound(lb_fwd), lower_bound(lb_bwd), out_norm)
    return _out_proj(o_a.reshape(n, -1), o_b.reshape(n, -1), w_out.astype(BF16), x)


def _mixer_odd(x, bsz, seq, norm_g, w_in, sink, q_norm, k_norm, w_out):
    n = x.shape[0]
    proj = _norm_matmul(x, norm_g, _odd_in_weight(w_in), tn=4 * LANES, out_dtype=F32)
    o_c = _window_attention(proj.reshape(bsz, seq, -1), sink.astype(F32), _alibi_slopes(WIN_H))
    pos = jnp.arange(seq)
    half = AX_HD // 2
    c_row, s_row = _rope_tables((pos // GRID_W).astype(F32), half)
    c_col, s_col = _rope_tables((pos % GRID_W).astype(F32), half)
    cos = jnp.concatenate([c_row, c_col], -1)
    sin = jnp.concatenate([s_row, s_col], -1)

    def tables(g):
        g = g.astype(F32)
        return cos * g[None, :], sin * _swap_halves(g, half)[None, :]

    cq, sq = tables(q_norm)
    ck, sk = tables(k_norm)
    q, k, v = _axial_prep(proj, cq, sq, ck, sk, seq)
    o_d = _attention(q.reshape(bsz, seq, -1), k.reshape(bsz, seq, -1), v.reshape(bsz, seq, -1), AX_H, AX_KV)
    return _out_proj(o_c.reshape(n, -1), o_d.reshape(n, -1), w_out.astype(BF16), x)


def _moe(x, norm_g, router, w_gate, w_up, w_down):
    xn, gates = _router(x, norm_g, router.astype(F32).T)
    wg, wu, wd = w_gate.astype(BF16), w_up.astype(BF16), w_down.astype(BF16)
    for e in range(N_EXPERTS):
        x = _moe_expert(x, xn, gates, wg, wu, wd, e)
    return x


def _trunk(x, bsz, seq, norm_mix_e, w_in_e, mla_q_norm, mla_w_uq, mla_kv_norm, mla_w_ukv, hg_lb_fwd, hg_lb_bwd,
           hg_out_norm, w_out_e, norm_ffn_e, ffn_w_gate, ffn_w_up, ffn_w_down, norm_mix_o, w_in_o, win_sink,
           ax_q_norm, ax_k_norm, w_out_o, norm_ffn_o, moe_router, moe_w_gate, moe_w_up, moe_w_down):
    for l in range(DEPTH):
        j = l // 2
        if l % 2 == 0:
            x = _mixer_even(x, bsz, seq, j, norm_mix_e[j], w_in_e[j], mla_q_norm[j], mla_w_uq[j], mla_kv_norm[j],
                            mla_w_ukv[j], hg_lb_fwd, hg_lb_bwd, hg_out_norm[j], w_out_e[j])
            x = _ffn(x, norm_ffn_e[j], ffn_w_gate[j].astype(BF16), ffn_w_up[j].astype(BF16),
                     ffn_w_down[j].astype(BF16))
        else:
            x = _mixer_odd(x, bsz, seq, norm_mix_o[j], w_in_o[j], win_sink[j], ax_q_norm[j], ax_k_norm[j], w_out_o[j])
            x = _moe(x, norm_ffn_o[j], moe_router[j], moe_w_gate[j], moe_w_up[j], moe_w_down[j])
    return x


def kernel(x_prompt, x_sample, norm_mix_e, w_in_e, mla_q_norm, mla_w_uq, mla_kv_norm, mla_w_ukv, hg_lb_fwd, hg_lb_bwd, hg_out_norm, w_out_e, norm_ffn_e, ffn_w_gate, ffn_w_up, ffn_w_down, norm_mix_o, w_in_o, win_sink, ax_q_norm, ax_k_norm, w_out_o, norm_ffn_o, moe_router, moe_w_gate, moe_w_up, moe_w_down, final_norm):
    bp, seq, d = x_prompt.shape
    bs = x_sample.shape[0]
    assert x_sample.shape[1:] == (seq, d)
    x = jnp.concatenate([x_prompt.reshape(bp * seq, d), x_sample.reshape(bs * seq, d)], axis=0)
    x = _trunk(x, bp + bs, seq, norm_mix_e, w_in_e, mla_q_norm, mla_w_uq, mla_kv_norm, mla_w_ukv, hg_lb_fwd,
               hg_lb_bwd, hg_out_norm, w_out_e, norm_ffn_e, ffn_w_gate, ffn_w_up, ffn_w_down, norm_mix_o, w_in_o,
               win_sink, ax_q_norm, ax_k_norm, w_out_o, norm_ffn_o, moe_router, moe_w_gate, moe_w_up, moe_w_down)
    y_prompt = _final_norm(x, final_norm, 0, bp * seq).reshape(bp, seq, d)
    y_sample = _final_norm(x, final_norm, bp * seq, bs * seq).reshape(bs, seq, d)
    return (y_prompt, y_sample)
```

```python
import functools

import jax
import jax.numpy as jnp
import numpy as np
from jax import lax
from jax.experimental import pallas as pl
from jax.experimental.pallas import tpu as pltpu

D_MODEL = 2048
DEPTH = 2
GRID_W = 64
EPS = 1e-6
ROPE_THETA = 10000.0

MLA_H = 8
MLA_NOPE = 128
MLA_ROPE = 64
MLA_V = 128
MLA_LORA = D_MODEL // 4

HG_H = 8
HG_DK = 128
HG_DV = 128

WIN_H = 16
WIN_KV = 2
WIN_HD = 64
WINDOW = 128

AX_H = 8
AX_KV = 2
AX_HD = 128

FF_DENSE = 5632
N_EXPERTS = 8
FF_EXPERT = 1408

LANES = 128
VMEM_CAP = 60000 * 1024
BF16 = jnp.bfloat16
F32 = jnp.float32

E_CQ, E_CKV, E_KRA, E_KRB, E_HQ, E_FF, E_FB, E_HI, E_HG, E_END = 0, 4, 8, 9, 10, 18, 26, 34, 42, 50
O_QD, O_QDS, O_QC, O_KD, O_KDS, O_VD, O_KC, O_VC, O_END = 0, 8, 16, 24, 26, 28, 30, 32, 36


def _cparams(sem, *block_bytes):
    need = int(sum(block_bytes)) + (6 << 20)
    return pltpu.CompilerParams(dimension_semantics=sem, vmem_limit_bytes=min(max(need, 16 << 20), VMEM_CAP))


def _nbytes(shape, dtype):
    return int(np.prod(shape)) * jnp.dtype(dtype).itemsize


def _rms_rows(x, g):
    return x * lax.rsqrt(jnp.mean(x * x, axis=-1, keepdims=True) + EPS) * g


def _norm_into(dst_ref, x_ref, g_ref, chunk=256):
    rows = x_ref.shape[0]
    chunk = min(chunk, rows)

    def body(c, carry):
        r = pl.ds(pl.multiple_of(c * chunk, chunk), chunk)
        dst_ref[r, :] = _rms_rows(x_ref[r, :].astype(F32), g_ref[...]).astype(dst_ref.dtype)
        return carry

    lax.fori_loop(0, rows // chunk, body, 0)


def _norm_matmul_kernel(x_ref, g_ref, w_ref, o_ref, xn_ref):
    @pl.when(pl.program_id(1) == 0)
    def _():
        _norm_into(xn_ref, x_ref, g_ref)

    o_ref[...] = jnp.dot(xn_ref[...], w_ref[...], preferred_element_type=F32).astype(o_ref.dtype)


def _norm_matmul(x, g, w, tn, out_dtype, tm=1024):
    n, k = x.shape
    nout = w.shape[1]
    tm = min(tm, n)
    return pl.pallas_call(
        _norm_matmul_kernel,
        grid=(n // tm, nout // tn),
        in_specs=[pl.BlockSpec((tm, k), lambda i, j: (i, 0)),
                  pl.BlockSpec((1, k), lambda i, j: (0, 0)),
                  pl.BlockSpec((k, tn), lambda i, j: (0, j))],
        out_specs=pl.BlockSpec((tm, tn), lambda i, j: (i, j)),
        out_shape=jax.ShapeDtypeStruct((n, nout), out_dtype),
        scratch_shapes=[pltpu.VMEM((tm, k), BF16)],
        compiler_params=_cparams(("parallel", "arbitrary"), 2 * _nbytes((tm, k), x.dtype), _nbytes((tm, k), BF16),
                                 2 * _nbytes((k, tn), BF16), 3 * _nbytes((tm, tn), F32)),
        name="norm_matmul",
    )(x, g.reshape(1, k), w)


def _mla_up_kernel(cq_ref, ckv_ref, kra_ref, krb_ref, qn_ref, kvn_ref, wq_ref, wkv_ref, cos_ref, sin_ref,
                   q_ref, k_ref, v_ref, cqn_ref, ckvn_ref, kr_ref, *, scale):
    @pl.when(pl.program_id(1) == 0)
    def _():
        _norm_into(cqn_ref, cq_ref, qn_ref)
        _norm_into(ckvn_ref, ckv_ref, kvn_ref)
        kr_ref[...] = (kra_ref[...] * cos_ref[...] + krb_ref[...] * sin_ref[...]).astype(BF16)

    q = jnp.dot(cqn_ref[...], wq_ref[...], preferred_element_type=F32)
    q_rope = q[:, LANES:2 * LANES] * cos_ref[...] + q[:, 2 * LANES:] * sin_ref[...]
    q_ref[:, :LANES] = (q[:, :LANES] * scale).astype(BF16)
    q_ref[:, LANES:] = (q_rope * scale).astype(BF16)
    kv = jnp.dot(ckvn_ref[...], wkv_ref[...], preferred_element_type=F32)
    k_ref[:, :LANES] = kv[:, :LANES].astype(BF16)
    k_ref[:, LANES:] = kr_ref[...]
    v_ref[...] = kv[:, LANES:].astype(BF16)


def _mla_up(proj, qn, kvn, wq, wkv, cos_t, sin_t, seq, tm=512):
    n = proj.shape[0]
    tm = min(tm, seq)
    nt = seq // tm
    lora = MLA_LORA
    scale = float((MLA_NOPE + MLA_ROPE) ** -0.5)
    return pl.pallas_call(
        functools.partial(_mla_up_kernel, scale=scale),
        grid=(n // tm, MLA_H),
        in_specs=[pl.BlockSpec((tm, lora), lambda i, h: (i, 0)),
                  pl.BlockSpec((tm, lora), lambda i, h: (i, 1)),
                  pl.BlockSpec((tm, LANES), lambda i, h: (i, E_KRA)),
                  pl.BlockSpec((tm, LANES), lambda i, h: (i, E_KRB)),
                  pl.BlockSpec((1, lora), lambda i, h: (0, 0)),
                  pl.BlockSpec((1, lora), lambda i, h: (0, 0)),
                  pl.BlockSpec((lora, 3 * LANES), lambda i, h: (0, h)),
                  pl.BlockSpec((lora, 2 * LANES), lambda i, h: (0, h)),
                  pl.BlockSpec((tm, LANES), lambda i, h: (i % nt, 0)),
                  pl.BlockSpec((tm, LANES), lambda i, h: (i % nt, 0))],
        out_specs=[pl.BlockSpec((tm, 2 * LANES), lambda i, h: (i, h)),
                   pl.BlockSpec((tm, 2 * LANES), lambda i, h: (i, h)),
                   pl.BlockSpec((tm, LANES), lambda i, h: (i, h))],
        out_shape=[jax.ShapeDtypeStruct((n, MLA_H * 2 * LANES), BF16),
                   jax.ShapeDtypeStruct((n, MLA_H * 2 * LANES), BF16),
                   jax.ShapeDtypeStruct((n, MLA_H * LANES), BF16)],
        scratch_shapes=[pltpu.VMEM((tm, lora), BF16), pltpu.VMEM((tm, lora), BF16), pltpu.VMEM((tm, LANES), BF16)],
        compiler_params=_cparams(("parallel", "arbitrary"), 4 * _nbytes((tm, lora), F32), 12 * _nbytes((tm, LANES), F32),
                                 4 * _nbytes((lora, 5 * LANES), BF16), 8 * _nbytes((tm, 3 * LANES), F32)),
        name="mla_up",
    )(proj, proj, proj, proj, qn.reshape(1, lora), kvn.reshape(1, lora), wq, wkv, cos_t, sin_t)


def _attention_kernel(q_ref, k_ref, v_ref, o_ref):
    s = lax.dot_general(q_ref[0], k_ref[0], (((1,), (1,)), ((), ())), preferred_element_type=F32)
    p = jnp.exp(s - jnp.max(s, axis=-1, keepdims=True))
    l = jnp.sum(p, axis=-1, keepdims=True)
    o = jnp.dot(p.astype(BF16), v_ref[0], preferred_element_type=F32)
    o_ref[0] = (o / l).astype(o_ref.dtype)


def _attention(q, k, v, heads, kv_heads, tq=512):
    b, t, _ = q.shape
    dq = q.shape[2] // heads
    dv = v.shape[2] // kv_heads
    g = heads // kv_heads
    tq = min(tq, t)
    return pl.pallas_call(
        _attention_kernel,
        grid=(b, heads, t // tq),
        in_specs=[pl.BlockSpec((1, tq, dq), lambda bi, h, qi: (bi, qi, h)),
                  pl.BlockSpec((1, t, dq), lambda bi, h, qi: (bi, 0, h // g)),
                  pl.BlockSpec((1, t, dv), lambda bi, h, qi: (bi, 0, h // g))],
        out_specs=pl.BlockSpec((1, tq, dv), lambda bi, h, qi: (bi, qi, h)),
        out_shape=jax.ShapeDtypeStruct((b, t, heads * dv), BF16),
        compiler_params=_cparams(("parallel", "parallel", "arbitrary"), 2 * _nbytes((tq, dq), BF16),
                                 2 * _nbytes((t, dq + dv), BF16), 2 * _nbytes((tq, dv), BF16),
                                 3 * _nbytes((tq, t), F32)),
        name="attention",
    )(q, k, v)


HG_CHUNK = 128
HG_DIRECT_CHUNK = 32
HG_MAX_LOG_RANGE = 80.0


def _split3(x):
    a = x.astype(BF16)
    r = x - a.astype(F32)
    b = r.astype(BF16)
    c = (r - b.astype(F32)).astype(BF16)
    return a, b, c


def _tri_masks(c_sz, reverse):
    row = lax.broadcasted_iota(jnp.int32, (c_sz, c_sz), 0)
    col = lax.broadcasted_iota(jnp.int32, (c_sz, c_sz), 1)
    keep = (row <= col) if reverse else (row >= col)
    return keep, jnp.where(keep, 1.0, 0.0).astype(BF16)


def _hgrn2_kernel(hq_ref, ff_ref, fb_ref, hi_ref, hg_ref, lbf_ref, lbb_ref, on_ref, o_ref,
                  accf_ref, accb_ref, sf_ref, sb_ref, dev_ref):
    t = hq_ref.shape[1]

    def chunk_inputs(f_ref, lb, rows):
        gate = lb + (1.0 - lb) * jax.nn.sigmoid(f_ref[0, rows, :])
        hq = hq_ref[0, rows, :]
        return hq * jax.nn.sigmoid(hq), 1.0 - gate, hi_ref[0, rows, :], jnp.log(gate)

    def log_decay(lf, tri):
        p0, p1, p2 = _split3(lf)
        return (jnp.dot(tri, p0, preferred_element_type=F32) + jnp.dot(tri, p1, preferred_element_type=F32)
                + jnp.dot(tri, p2, preferred_element_type=F32))

    def state_step(s_ref, q, k, v, b, b_last):
        s_t = s_ref[...]
        inter = lax.dot_general((q * jnp.exp(b)).astype(BF16), s_t.astype(BF16), (((1,), (1,)), ((), ())),
                                preferred_element_type=F32)
        kd = (k * jnp.exp(b_last - b)).astype(BF16)
        upd = lax.dot_general(v.astype(BF16), kd, (((0,), (0,)), ((), ())), preferred_element_type=F32)
        s_ref[...] = jnp.exp(b_last) * s_t + upd
        return inter

    def fast_step(f_ref, lb, s_ref, acc_ref, c, masks, reverse):
        c_sz = HG_CHUNK
        keep, tri = masks
        rows = pl.ds(pl.multiple_of(c * c_sz, c_sz), c_sz)
        q, k, v, lf = chunk_inputs(f_ref, lb, rows)
        b = log_decay(lf, tri)
        r = b[c_sz // 2:c_sz // 2 + 1, :]
        b_last = b[0:1, :] if reverse else b[c_sz - 1:c_sz, :]
        b_first = b[c_sz - 1:c_sz, :] if reverse else b[0:1, :]
        dev_ref[...] = jnp.maximum(dev_ref[...], jnp.maximum(jnp.abs(b_first - r), jnp.abs(b_last - r)))
        qe = (q * jnp.exp(b - r)).astype(BF16)
        ke = (k * jnp.exp(r - b)).astype(BF16)
        a = lax.dot_general(qe, ke, (((1,), (1,)), ((), ())), preferred_element_type=F32)
        a = jnp.where(keep, a, 0.0).astype(BF16)
        intra = jnp.dot(a, v.astype(BF16), preferred_element_type=F32)
        acc_ref[rows, :] = intra + state_step(s_ref, q, k, v, b, b_last)

    def direct_step(f_ref, lb, s_ref, acc_ref, c, masks, reverse):
        c_sz = HG_DIRECT_CHUNK
        _, tri = masks
        row1 = lax.broadcasted_iota(jnp.int32, (c_sz, 1), 0)
        rows = pl.ds(pl.multiple_of(c * c_sz, c_sz), c_sz)
        q, k, v, lf = chunk_inputs(f_ref, lb, rows)
        b = log_decay(lf, tri)
        b_last = b[0:1, :] if reverse else b[c_sz - 1:c_sz, :]
        o = state_step(s_ref, q, k, v, b, b_last)
        for s in range(c_sz):
            e = jnp.exp(jnp.minimum(b - b[s:s + 1, :], 0.0))
            a = jnp.sum(q * k[s:s + 1, :] * e, axis=-1, keepdims=True)
            a = jnp.where((row1 <= s) if reverse else (row1 >= s), a, 0.0)
            o = o + a * v[s:s + 1, :]
        acc_ref[rows, :] = o

    def scan_both(step, c_sz):
        nc = t // c_sz
        masks_f = _tri_masks(c_sz, False)
        masks_b = _tri_masks(c_sz, True)
        sf_ref[...] = jnp.zeros_like(sf_ref)
        sb_ref[...] = jnp.zeros_like(sb_ref)

        def body(ci, carry):
            step(ff_ref, lbf_ref[...], sf_ref, accf_ref, ci, masks_f, False)
            step(fb_ref, lbb_ref[...], sb_ref, accb_ref, nc - 1 - ci, masks_b, True)
            return carry

        lax.fori_loop(0, nc, body, 0)

    dev_ref[...] = jnp.zeros_like(dev_ref)
    scan_both(fast_step, HG_CHUNK)

    @pl.when(jnp.logical_not(jnp.max(dev_ref[...]) <= HG_MAX_LOG_RANGE))
    def _():
        scan_both(direct_step, HG_DIRECT_CHUNK)

    def finish(c, carry):
        rows = pl.ds(pl.multiple_of(c * 256, 256), 256)
        hg = hg_ref[0, rows, :]
        y = _rms_rows(accf_ref[rows, :] + accb_ref[rows, :], on_ref[...]) * (hg * jax.nn.sigmoid(hg))
        o_ref[0, rows, :] = y.astype(o_ref.dtype)
        return carry

    lax.fori_loop(0, t // 256, finish, 0)


def _hgrn2(proj, lb_f, lb_b, out_norm):
    b, t, _ = proj.shape

    def col(base):
        return pl.BlockSpec((1, t, LANES), lambda bi, h: (bi, 0, base + h))

    return pl.pallas_call(
        _hgrn2_kernel,
        grid=(b, HG_H),
        in_specs=[col(E_HQ), col(E_FF), col(E_FB), col(E_HI), col(E_HG),
                  pl.BlockSpec((1, LANES), lambda bi, h: (0, h)),
                  pl.BlockSpec((1, LANES), lambda bi, h: (0, h)),
                  pl.BlockSpec((1, LANES), lambda bi, h: (0, 0))],
        out_specs=pl.BlockSpec((1, t, LANES), lambda bi, h: (bi, 0, h)),
        out_shape=jax.ShapeDtypeStruct((b, t, HG_H * HG_DV), BF16),
        scratch_shapes=[pltpu.VMEM((t, HG_DV), F32), pltpu.VMEM((t, HG_DV), F32),
                        pltpu.VMEM((HG_DV, HG_DK), F32), pltpu.VMEM((HG_DV, HG_DK), F32), pltpu.VMEM((1, HG_DK), F32)],
        compiler_params=_cparams(("parallel", "parallel"), 10 * _nbytes((t, LANES), F32), 4 * _nbytes((t, LANES), F32)),
        name="hgrn2",
    )(proj, proj, proj, proj, proj, lb_f, lb_b, out_norm.reshape(1, HG_DV))


def _out_proj_kernel(a_ref, b_ref, wa_ref, wb_ref, x_ref, o_ref):
    acc = jnp.dot(a_ref[...], wa_ref[...], preferred_element_type=F32)
    acc = acc + jnp.dot(b_ref[...], wb_ref[...], preferred_element_type=F32)
    o_ref[...] = x_ref[...] + acc


def _out_proj(a, b, w, x, tm=1024, tn=512):
    n, ka = a.shape
    d = w.shape[1]
    tm = min(tm, n)
    return pl.pallas_call(
        _out_proj_kernel,
        grid=(n // tm, d // tn),
        in_specs=[pl.BlockSpec((tm, ka), lambda i, j: (i, 0)),
                  pl.BlockSpec((tm, ka), lambda i, j: (i, 0)),
                  pl.BlockSpec((ka, tn), lambda i, j: (0, j)),
                  pl.BlockSpec((ka, tn), lambda i, j: (1, j)),
                  pl.BlockSpec((tm, tn), lambda i, j: (i, j))],
        out_specs=pl.BlockSpec((tm, tn), lambda i, j: (i, j)),
        out_shape=jax.ShapeDtypeStruct((n, d), F32),
        compiler_params=_cparams(("parallel", "arbitrary"), 4 * _nbytes((tm, ka), BF16), 4 * _nbytes((ka, tn), BF16),
                                 5 * _nbytes((tm, tn), F32)),
        name="out_proj",
    )(a, b, w, w, x)


def _ffn_kernel(x_ref, g_ref, wg_ref, wu_ref, wd_ref, o_ref, xn_ref):
    f = pl.program_id(1)

    @pl.when(f == 0)
    def _():
        _norm_into(xn_ref, x_ref, g_ref)
        o_ref[...] = x_ref[...]

    xn = xn_ref[...]
    gt = jnp.dot(xn, wg_ref[...], preferred_element_type=F32)
    up = jnp.dot(xn, wu_ref[...], preferred_element_type=F32)
    act = (gt * jax.nn.sigmoid(gt) * up).astype(BF16)
    o_ref[...] += jnp.dot(act, wd_ref[...], preferred_element_type=F32)


def _ffn(x, g, wg, wu, wd, tm=512, tf=512):
    n, d = x.shape
    ff = wg.shape[1]
    tm = min(tm, n)
    return pl.pallas_call(
        _ffn_kernel,
        grid=(n // tm, ff // tf),
        in_specs=[pl.BlockSpec((tm, d), lambda i, f: (i, 0)),
                  pl.BlockSpec((1, d), lambda i, f: (0, 0)),
                  pl.BlockSpec((d, tf), lambda i, f: (0, f)),
                  pl.BlockSpec((d, tf), lambda i, f: (0, f)),
                  pl.BlockSpec((tf, d), lambda i, f: (f, 0))],
        out_specs=pl.BlockSpec((tm, d), lambda i, f: (i, 0)),
        out_shape=jax.ShapeDtypeStruct((n, d), F32),
        scratch_shapes=[pltpu.VMEM((tm, d), BF16)],
        compiler_params=_cparams(("parallel", "arbitrary"), 4 * _nbytes((tm, d), F32), _nbytes((tm, d), BF16),
                                 6 * _nbytes((d, tf), BF16), 4 * _nbytes((tm, tf), F32), _nbytes((tm, d), F32)),
        name="ffn",
    )(x, g.reshape(1, d), wg, wu, wd)


def _axial_prep_kernel(qd_ref, qs_ref, kd_ref, ks_ref, vd_ref, cq_ref, sq_ref, ck_ref, sk_ref,
                       q_ref, k_ref, v_ref, *, scale):
    def rope(x_ref, xs_ref, c_ref, s_ref, h, mul):
        sl = slice(h * LANES, (h + 1) * LANES)
        x = x_ref[:, sl]
        r = lax.rsqrt(jnp.mean(x * x, axis=-1, keepdims=True) + EPS)
        return ((x * c_ref[...] + xs_ref[:, sl] * s_ref[...]) * (r * mul)).astype(BF16)

    for h in range(AX_H):
        q_ref[:, h * LANES:(h + 1) * LANES] = rope(qd_ref, qs_ref, cq_ref, sq_ref, h, scale)
    for h in range(AX_KV):
        k_ref[:, h * LANES:(h + 1) * LANES] = rope(kd_ref, ks_ref, ck_ref, sk_ref, h, 1.0)
    v_ref[...] = vd_ref[...].astype(BF16)


def _axial_prep(proj, cq, sq, ck, sk, seq, tm=512):
    n = proj.shape[0]
    tm = min(tm, seq)
    nt = seq // tm
    qw, kw = AX_H * AX_HD, AX_KV * AX_HD
    tab = pl.BlockSpec((tm, LANES), lambda i: (i % nt, 0))
    return pl.pallas_call(
        functools.partial(_axial_prep_kernel, scale=float(AX_HD ** -0.5)),
        grid=(n // tm,),
        in_specs=[pl.BlockSpec((tm, qw), lambda i: (i, O_QD * LANES // qw)),
                  pl.BlockSpec((tm, qw), lambda i: (i, O_QDS * LANES // qw)),
                  pl.BlockSpec((tm, kw), lambda i: (i, O_KD * LANES // kw)),
                  pl.BlockSpec((tm, kw), lambda i: (i, O_KDS * LANES // kw)),
                  pl.BlockSpec((tm, kw), lambda i: (i, O_VD * LANES // kw)),
                  tab, tab, tab, tab],
        out_specs=[pl.BlockSpec((tm, qw), lambda i: (i, 0)),
                   pl.BlockSpec((tm, kw), lambda i: (i, 0)),
                   pl.BlockSpec((tm, kw), lambda i: (i, 0))],
        out_shape=[jax.ShapeDtypeStruct((n, qw), BF16), jax.ShapeDtypeStruct((n, kw), BF16),
                   jax.ShapeDtypeStruct((n, kw), BF16)],
        compiler_params=_cparams(("parallel",), 4 * _nbytes((tm, qw), F32), 6 * _nbytes((tm, kw), F32),
                                 8 * _nbytes((tm, LANES), F32), 2 * _nbytes((tm, qw + 2 * kw), BF16)),
        name="axial_prep",
    )(proj, proj, proj, proj, proj, cq, sq, ck, sk)


WIN_BLOCK = 128


def _window_kernel(sink_ref, slope_ref, q_ref, k_ref, v_ref, o_ref, *, scale):
    t = q_ref.shape[1]
    wb = WIN_BLOCK
    span = 3 * wb
    pair = pl.program_id(1)
    lane = lax.broadcasted_iota(jnp.int32, (wb, LANES), 1)
    low = lane < WIN_HD
    delta = lax.broadcasted_iota(jnp.int32, (wb, span), 0) - lax.broadcasted_iota(jnp.int32, (wb, span), 1)

    def body(qb, carry):
        start = jnp.clip((qb - 1) * wb, 0, t - span)
        start = pl.multiple_of(start, wb)
        kwin = k_ref[0, pl.ds(start, span), :].astype(BF16)
        vwin = v_ref[0, pl.ds(start, span), :].astype(BF16)
        qrows = pl.ds(pl.multiple_of(qb * wb, wb), wb)
        q2 = q_ref[0, qrows, :] * scale
        dist = jnp.abs(delta + (qb * wb - start))
        valid = dist <= WINDOW
        distf = dist.astype(F32)
        outs = []
        for j in range(2):
            head = 2 * pair + j
            sink = sink_ref[head]
            qh = jnp.where(low if j == 0 else jnp.logical_not(low), q2, 0.0).astype(BF16)
            s = lax.dot_general(qh, kwin, (((1,), (1,)), ((), ())), preferred_element_type=F32)
            s = jnp.where(valid, s - slope_ref[head] * distf, -jnp.inf)
            m = jnp.maximum(jnp.max(s, axis=-1, keepdims=True), sink)
            e = jnp.exp(s - m)
            den = jnp.sum(e, axis=-1, keepdims=True) + jnp.exp(sink - m)
            outs.append(jnp.dot(e.astype(BF16), vwin, preferred_element_type=F32) / den)
        o_ref[0, qrows, :] = jnp.where(low, outs[0], outs[1]).astype(o_ref.dtype)
        return carry

    lax.fori_loop(0, t // wb, body, 0)


def _window_attention(proj, sink, slopes):
    b, t, _ = proj.shape
    pairs = WIN_H // 2
    per_kv = pairs // WIN_KV
    smem = pl.BlockSpec(memory_space=pltpu.SMEM)
    return pl.pallas_call(
        functools.partial(_window_kernel, scale=float(WIN_HD ** -0.5)),
        grid=(b, pairs),
        in_specs=[smem, smem,
                  pl.BlockSpec((1, t, LANES), lambda bi, p: (bi, 0, O_QC + p)),
                  pl.BlockSpec((1, t, LANES), lambda bi, p: (bi, 0, O_KC + p // per_kv)),
                  pl.BlockSpec((1, t, LANES), lambda bi, p: (bi, 0, O_VC + p // per_kv))],
        out_specs=pl.BlockSpec((1, t, LANES), lambda bi, p: (bi, 0, p)),
        out_shape=jax.ShapeDtypeStruct((b, t, WIN_H * WIN_HD), BF16),
        compiler_params=_cparams(("parallel", "parallel"), 6 * _nbytes((t, LANES), F32), 2 * _nbytes((t, LANES), BF16)),
        name="window_attention",
    )(sink, slopes, proj, proj, proj)


def _router_kernel(x_ref, g_ref, r_ref, xn_ref, gate_ref):
    xn = _rms_rows(x_ref[...], g_ref[...])
    xn_ref[...] = xn.astype(BF16)
    logits = [jnp.sum(xn * r_ref[e:e + 1, :], axis=-1, keepdims=True) for e in range(N_EXPERTS)]

    def top(ls):
        m = functools.reduce(jnp.maximum, ls)
        idx = jnp.full_like(m, N_EXPERTS).astype(jnp.int32)
        for e in reversed(range(N_EXPERTS)):
            idx = jnp.where(ls[e] == m, e, idx)
        return m, idx

    m1, i1 = top(logits)
    m2, i2 = top([jnp.where(i1 == e, -jnp.inf, logits[e]) for e in range(N_EXPERTS)])
    e2 = jnp.exp(m2 - m1)
    w1 = 1.0 / (1.0 + e2)
    w2 = e2 / (1.0 + e2)
    lane = lax.broadcasted_iota(jnp.int32, gate_ref.shape, 1)
    gate_ref[...] = jnp.where(lane == i1, w1, 0.0) + jnp.where(lane == i2, w2, 0.0)


def _router(x, g, router_t, tm=256):
    n, d = x.shape
    tm = min(tm, n)
    return pl.pallas_call(
        _router_kernel,
        grid=(n // tm,),
        in_specs=[pl.BlockSpec((tm, d), lambda i: (i, 0)),
                  pl.BlockSpec((1, d), lambda i: (0, 0)),
                  pl.BlockSpec((N_EXPERTS, d), lambda i: (0, 0))],
        out_specs=[pl.BlockSpec((tm, d), lambda i: (i, 0)),
                   pl.BlockSpec((tm, LANES), lambda i: (i, 0))],
        out_shape=[jax.ShapeDtypeStruct((n, d), BF16), jax.ShapeDtypeStruct((n, LANES), F32)],
        compiler_params=_cparams(("parallel",), 6 * _nbytes((tm, d), F32), 2 * _nbytes((tm, d), BF16)),
        name="router",
    )(x, g.reshape(1, d), router_t)


def _moe_expert_kernel(xn_ref, gate_ref, wg_ref, wu_ref, wd_ref, acc_ref, o_ref, *, expert):
    xn = xn_ref[...]
    gt = jnp.dot(xn, wg_ref[0], preferred_element_type=F32)
    up = jnp.dot(xn, wu_ref[0], preferred_element_type=F32)
    act = (gt * jax.nn.sigmoid(gt) * up).astype(BF16)
    y = jnp.dot(act, wd_ref[0], preferred_element_type=F32)
    lane = lax.broadcasted_iota(jnp.int32, gate_ref.shape, 1)
    w = jnp.sum(jnp.where(lane == expert, gate_ref[...], 0.0), axis=-1, keepdims=True)
    o_ref[...] = acc_ref[...] + w * y


def _moe_expert(acc, xn, gates, wg, wu, wd, expert, tm=512):
    n, d = acc.shape
    ff = wg.shape[2]
    tm = min(tm, n)
    once = dict(pipeline_mode=pl.Buffered(1))
    return pl.pallas_call(
        functools.partial(_moe_expert_kernel, expert=expert),
        grid=(n // tm,),
        in_specs=[pl.BlockSpec((tm, d), lambda i: (i, 0)),
                  pl.BlockSpec((tm, LANES), lambda i: (i, 0)),
                  pl.BlockSpec((1, d, ff), lambda i: (expert, 0, 0), **once),
                  pl.BlockSpec((1, d, ff), lambda i: (expert, 0, 0), **once),
                  pl.BlockSpec((1, ff, d), lambda i: (expert, 0, 0), **once),
                  pl.BlockSpec((tm, d), lambda i: (i, 0))],
        out_specs=pl.BlockSpec((tm, d), lambda i: (i, 0)),
        out_shape=jax.ShapeDtypeStruct((n, d), F32),
        input_output_aliases={5: 0},
        compiler_params=_cparams(("parallel",), 2 * _nbytes((tm, d), BF16), 3 * _nbytes((d, ff), BF16),
                                 5 * _nbytes((tm, d), F32), 4 * _nbytes((tm, ff), F32)),
        name="moe_expert",
    )(xn, gates, wg, wu, wd, acc)


def _final_norm_kernel(x_ref, g_ref, o_ref):
    o_ref[...] = _rms_rows(x_ref[...], g_ref[...])


def _final_norm(x, g, row0, rows, tm=512):
    d = x.shape[1]
    tm = min(tm, rows)
    off = row0 // tm
    return pl.pallas_call(
        _final_norm_kernel,
        grid=(rows // tm,),
        in_specs=[pl.BlockSpec((tm, d), lambda i: (i + off, 0)),
                  pl.BlockSpec((1, d), lambda i: (0, 0))],
        out_specs=pl.BlockSpec((tm, d), lambda i: (i, 0)),
        out_shape=jax.ShapeDtypeStruct((rows, d), F32),
        compiler_params=_cparams(("parallel",), 6 * _nbytes((tm, d), F32)),
        name="final_norm",
    )(x, g.reshape(1, d))


def _swap_halves(w, width):
    lead = w.shape[:-1]
    return jnp.flip(w.reshape(lead + (-1, 2, width // 2)), axis=-2).reshape(w.shape)


def _pad_cols(w, width):
    return jnp.pad(w, [(0, 0)] * (w.ndim - 1) + [(0, width - w.shape[-1])])


def _even_in_weight(w):
    kr = w[:, 2 * MLA_LORA:2 * MLA_LORA + MLA_ROPE]
    return jnp.concatenate([w[:, :2 * MLA_LORA], _pad_cols(kr, LANES), _pad_cols(_swap_halves(kr, MLA_ROPE), LANES),
                            w[:, 2 * MLA_LORA + MLA_ROPE:]], axis=1).astype(BF16)


def _mla_q_weight(w):
    w = w.reshape(MLA_LORA, MLA_H, MLA_NOPE + MLA_ROPE)
    rope = w[..., MLA_NOPE:]
    w = jnp.concatenate([w[..., :MLA_NOPE], _pad_cols(rope, LANES), _pad_cols(_swap_halves(rope, MLA_ROPE), LANES)], -1)
    return w.reshape(MLA_LORA, MLA_H * 3 * LANES).astype(BF16)


def _odd_in_weight(w):
    d = w.shape[0]
    o = 0
    parts = {}
    for name, width in (("qc", WIN_H * WIN_HD), ("kc", WIN_KV * WIN_HD), ("vc", WIN_KV * WIN_HD),
                        ("qd", AX_H * AX_HD), ("kd", AX_KV * AX_HD), ("vd", AX_KV * AX_HD)):
        parts[name] = w[:, o:o + width]
        o += width

    def dup(x):
        x = x.reshape(d, WIN_KV, WIN_HD)
        return jnp.concatenate([x, x], axis=-1).reshape(d, WIN_KV * LANES)

    half = AX_HD // 2
    out = jnp.concatenate([parts["qd"], _swap_halves(parts["qd"], half), parts["qc"], parts["kd"],
                           _swap_halves(parts["kd"], half), parts["vd"], dup(parts["kc"]), dup(parts["vc"])], axis=1)
    return _pad_cols(out, O_END * LANES).astype(BF16)


def _rope_tables(pos, dim):
    inv = ROPE_THETA ** (-jnp.arange(0, dim, 2, dtype=F32) / dim)
    ang = pos[:, None] * inv[None, :]
    cos, sin = jnp.cos(ang), jnp.sin(ang)
    return jnp.concatenate([cos, cos], -1), jnp.concatenate([-sin, sin], -1)


def _alibi_slopes(n):
    return jnp.asarray(2.0 ** (-8.0 * np.arange(1, n + 1) / n), dtype=F32)


def _mixer_even(x, bsz, seq, j, norm_g, w_in, q_norm, w_uq, kv_norm, w_ukv, lb_fwd, lb_bwd, out_norm, w_out):
    n = x.shape[0]
    proj = _norm_matmul(x, norm_g, _even_in_weight(w_in), tn=5 * LANES, out_dtype=F32)
    cos, sin = _rope_tables(jnp.arange(seq, dtype=F32), MLA_ROPE)
    q, k, v = _mla_up(proj, q_norm, kv_norm, _mla_q_weight(w_uq), w_ukv.astype(BF16),
                      _pad_cols(cos, LANES), _pad_cols(sin, LANES), seq)
    o_a = _attention(q.reshape(bsz, seq, -1), k.reshape(bsz, seq, -1), v.reshape(bsz, seq, -1), MLA_H, MLA_H)

    def lower_bound(tab):
        return jnp.cumsum(jax.nn.softmax(tab.astype(F32), axis=0), axis=0)[j].reshape(1, HG_H * HG_DK)

    o_b = _hgrn2(proj.reshape(bsz, seq, -1), lower_bound(lb_fwd), lower_bound(lb_bwd), out_norm)
    return _out_proj(o_a.reshape(n, -1), o_b.reshape(n, -1), w_out.astype(BF16), x)


def _mixer_odd(x, bsz, seq, norm_g, w_in, sink, q_norm, k_norm, w_out):
    n = x.shape[0]
    proj = _norm_matmul(x, norm_g, _odd_in_weight(w_in), tn=4 * LANES, out_dtype=F32)
    o_c = _window_attention(proj.reshape(bsz, seq, -1), sink.astype(F32), _alibi_slopes(WIN_H))
    pos = jnp.arange(seq)
    half = AX_HD // 2
    c_row, s_row = _rope_tables((pos // GRID_W).astype(F32), half)
    c_col, s_col = _rope_tables((pos % GRID_W).astype(F32), half)
    cos = jnp.concatenate([c_row, c_col], -1)
    sin = jnp.concatenate([s_row, s_col], -1)

    def tables(g):
        g = g.astype(F32)
        return cos * g[None, :], sin * _swap_halves(g, half)[None, :]

    cq, sq = tables(q_norm)
    ck, sk = tables(k_norm)
    q, k, v = _axial_prep(proj, cq, sq, ck, sk, seq)
    o_d = _attention(q.reshape(bsz, seq, -1), k.reshape(bsz, seq, -1), v.reshape(bsz, seq, -1), AX_H, AX_KV)
    return _out_proj(o_c.reshape(n, -1), o_d.reshape(n, -1), w_out.astype(BF16), x)


def _moe(x, norm_g, router, w_gate, w_up, w_down):
    xn, gates = _router(x, norm_g, router.astype(F32).T)
    wg, wu, wd = w_gate.astype(BF16), w_up.astype(BF16), w_down.astype(BF16)
    for e in range(N_EXPERTS):
        x = _moe_expert(x, xn, gates, wg, wu, wd, e)
    return x


def _trunk(x, bsz, seq, norm_mix_e, w_in_e, mla_q_norm, mla_w_uq, mla_kv_norm, mla_w_ukv, hg_lb_fwd, hg_lb_bwd,
           hg_out_norm, w_out_e, norm_ffn_e, ffn_w_gate, ffn_w_up, ffn_w_down, norm_mix_o, w_in_o, win_sink,
           ax_q_norm, ax_k_norm, w_out_o, norm_ffn_o, moe_router, moe_w_gate, moe_w_up, moe_w_down):
    for l in range(DEPTH):
        j = l // 2
        if l % 2 == 0:
            x = _mixer_even(x, bsz, seq, j, norm_mix_e[j], w_in_e[j], mla_q_norm[j], mla_w_uq[j], mla_kv_norm[j],
                            mla_w_ukv[j], hg_lb_fwd, hg_lb_bwd, hg_out_norm[j], w_out_e[j])
            x = _ffn(x, norm_ffn_e[j], ffn_w_gate[j].astype(BF16), ffn_w_up[j].astype(BF16),
                     ffn_w_down[j].astype(BF16))
        else:
            x = _mixer_odd(x, bsz, seq, norm_mix_o[j], w_in_o[j], win_sink[j], ax_q_norm[j], ax_k_norm[j], w_out_o[j])
            x = _moe(x, norm_ffn_o[j], moe_router[j], moe_w_gate[j], moe_w_up[j], moe_w_down[j])
    return x


def kernel(x_prompt, x_sample, norm_mix_e, w_in_e, mla_q_norm, mla_w_uq, mla_kv_norm, mla_w_ukv, hg_lb_fwd, hg_lb_bwd, hg_out_norm, w_out_e, norm_ffn_e, ffn_w_gate, ffn_w_up, ffn_w_down, norm_mix_o, w_in_o, win_sink, ax_q_norm, ax_k_norm, w_out_o, norm_ffn_o, moe_router, moe_w_gate, moe_w_up, moe_w_down, final_norm):
    bp, seq, d = x_prompt.shape
    bs = x_sample.shape[0]
    assert x_sample.shape[1:] == (seq, d)
    x = jnp.concatenate([x_prompt.reshape(bp * seq, d), x_sample.reshape(bs * seq, d)], axis=0)
    x = _trunk(x, bp + bs, seq, norm_mix_e, w_in_e, mla_q_norm, mla_w_uq, mla_kv_norm, mla_w_ukv, hg_lb_fwd,
               hg_lb_bwd, hg_out_norm, w_out_e, norm_ffn_e, ffn_w_gate, ffn_w_up, ffn_w_down, norm_mix_o, w_in_o,
               win_sink, ax_q_norm, ax_k_norm, w_out_o, norm_ffn_o, moe_router, moe_w_gate, moe_w_up, moe_w_down)
    y_prompt = _final_norm(x, final_norm, 0, bp * seq).reshape(bp, seq, d)
    y_sample = _final_norm(x, final_norm, bp * seq, bs * seq).reshape(bs, seq, d)
    return (y_prompt, y_sample)
```

```python
import functools

import jax
import jax.numpy as jnp
import numpy as np
from jax import lax
from jax.experimental import pallas as pl
from jax.experimental.pallas import tpu as pltpu

D_MODEL = 2048
DEPTH = 2
GRID_W = 64
EPS = 1e-6
ROPE_THETA = 10000.0

MLA_H = 8
MLA_NOPE = 128
MLA_ROPE = 64
MLA_V = 128
MLA_LORA = D_MODEL // 4

HG_H = 8
HG_DK = 128
HG_DV = 128

WIN_H = 16
WIN_KV = 2
WIN_HD = 64
WINDOW = 128

AX_H = 8
AX_KV = 2
AX_HD = 128

FF_DENSE = 5632
N_EXPERTS = 8
FF_EXPERT = 1408

LANES = 128
VMEM_CAP = 60000 * 1024
BF16 = jnp.bfloat16
F32 = jnp.float32

E_CQ, E_CKV, E_KRA, E_KRB, E_HQ, E_FF, E_FB, E_HI, E_HG, E_END = 0, 4, 8, 9, 10, 18, 26, 34, 42, 50
O_QD, O_QDS, O_QC, O_KD, O_KDS, O_VD, O_KC, O_VC, O_END = 0, 8, 16, 24, 26, 28, 30, 32, 36


def _cparams(sem, *block_bytes):
    need = int(sum(block_bytes)) + (6 << 20)
    return pltpu.CompilerParams(dimension_semantics=sem, vmem_limit_bytes=min(max(need, 16 << 20), VMEM_CAP))


def _nbytes(shape, dtype):
    return int(np.prod(shape)) * jnp.dtype(dtype).itemsize


def _rms_rows(x, g):
    return x * lax.rsqrt(jnp.mean(x * x, axis=-1, keepdims=True) + EPS) * g


def _norm_into(dst_ref, x_ref, g_ref, chunk=256):
    rows = x_ref.shape[0]
    chunk = min(chunk, rows)

    def body(c, carry):
        r = pl.ds(pl.multiple_of(c * chunk, chunk), chunk)
        dst_ref[r, :] = _rms_rows(x_ref[r, :].astype(F32), g_ref[...]).astype(dst_ref.dtype)
        return carry

    lax.fori_loop(0, rows // chunk, body, 0)


def _norm_matmul_kernel(x_ref, g_ref, w_ref, o_ref, xn_ref):
    @pl.when(pl.program_id(1) == 0)
    def _():
        _norm_into(xn_ref, x_ref, g_ref)

    o_ref[...] = jnp.dot(xn_ref[...], w_ref[...], preferred_element_type=F32).astype(o_ref.dtype)


def _norm_matmul(x, g, w, tn, out_dtype, tm=1024):
    n, k = x.shape
    nout = w.shape[1]
    tm = min(tm, n)
    return pl.pallas_call(
        _norm_matmul_kernel,
        grid=(n // tm, nout // tn),
        in_specs=[pl.BlockSpec((tm, k), lambda i, j: (i, 0)),
                  pl.BlockSpec((1, k), lambda i, j: (0, 0)),
                  pl.BlockSpec((k, tn), lambda i, j: (0, j))],
        out_specs=pl.BlockSpec((tm, tn), lambda i, j: (i, j)),
        out_shape=jax.ShapeDtypeStruct((n, nout), out_dtype),
        scratch_shapes=[pltpu.VMEM((tm, k), BF16)],
        compiler_params=_cparams(("parallel", "arbitrary"), 2 * _nbytes((tm, k), x.dtype), _nbytes((tm, k), BF16),
                                 2 * _nbytes((k, tn), BF16), 3 * _nbytes((tm, tn), F32)),
        name="norm_matmul",
    )(x, g.reshape(1, k), w)


def _mla_up_kernel(cq_ref, ckv_ref, kra_ref, krb_ref, qn_ref, kvn_ref, wq_ref, wkv_ref, cos_ref, sin_ref,
                   q_ref, k_ref, v_ref, cqn_ref, ckvn_ref, kr_ref, *, scale):
    @pl.when(pl.program_id(1) == 0)
    def _():
        _norm_into(cqn_ref, cq_ref, qn_ref)
        _norm_into(ckvn_ref, ckv_ref, kvn_ref)
        kr_ref[...] = (kra_ref[...] * cos_ref[...] + krb_ref[...] * sin_ref[...]).astype(BF16)

    q = jnp.dot(cqn_ref[...], wq_ref[...], preferred_element_type=F32)
    q_rope = q[:, LANES:2 * LANES] * cos_ref[...] + q[:, 2 * LANES:] * sin_ref[...]
    q_ref[:, :LANES] = (q[:, :LANES] * scale).astype(BF16)
    q_ref[:, LANES:] = (q_rope * scale).astype(BF16)
    kv = jnp.dot(ckvn_ref[...], wkv_ref[...], preferred_element_type=F32)
    k_ref[:, :LANES] = kv[:, :LANES].astype(BF16)
    k_ref[:, LANES:] = kr_ref[...]
    v_ref[...] = kv[:, LANES:].astype(BF16)


def _mla_up(proj, qn, kvn, wq, wkv, cos_t, sin_t, seq, tm=512):
    n = proj.shape[0]
    tm = min(tm, seq)
    nt = seq // tm
    lora = MLA_LORA
    scale = float((MLA_NOPE + MLA_ROPE) ** -0.5)
    return pl.pallas_call(
        functools.partial(_mla_up_kernel, scale=scale),
        grid=(n // tm, MLA_H),
        in_specs=[pl.BlockSpec((tm, lora), lambda i, h: (i, 0)),
                  pl.BlockSpec((tm, lora), lambda i, h: (i, 1)),
                  pl.BlockSpec((tm, LANES), lambda i, h: (i, E_KRA)),
                  pl.BlockSpec((tm, LANES), lambda i, h: (i, E_KRB)),
                  pl.BlockSpec((1, lora), lambda i, h: (0, 0)),
                  pl.BlockSpec((1, lora), lambda i, h: (0, 0)),
                  pl.BlockSpec((lora, 3 * LANES), lambda i, h: (0, h)),
                  pl.BlockSpec((lora, 2 * LANES), lambda i, h: (0, h)),
                  pl.BlockSpec((tm, LANES), lambda i, h: (i % nt, 0)),
                  pl.BlockSpec((tm, LANES), lambda i, h: (i % nt, 0))],
        out_specs=[pl.BlockSpec((tm, 2 * LANES), lambda i, h: (i, h)),
                   pl.BlockSpec((tm, 2 * LANES), lambda i, h: (i, h)),
                   pl.BlockSpec((tm, LANES), lambda i, h: (i, h))],
        out_shape=[jax.ShapeDtypeStruct((n, MLA_H * 2 * LANES), BF16),
                   jax.ShapeDtypeStruct((n, MLA_H * 2 * LANES), BF16),
                   jax.ShapeDtypeStruct((n, MLA_H * LANES), BF16)],
        scratch_shapes=[pltpu.VMEM((tm, lora), BF16), pltpu.VMEM((tm, lora), BF16), pltpu.VMEM((tm, LANES), BF16)],
        compiler_params=_cparams(("parallel", "arbitrary"), 4 * _nbytes((tm, lora), F32), 12 * _nbytes((tm, LANES), F32),
                                 4 * _nbytes((lora, 5 * LANES), BF16), 8 * _nbytes((tm, 3 * LANES), F32)),
        name="mla_up",
    )(proj, proj, proj, proj, qn.reshape(1, lora), kvn.reshape(1, lora), wq, wkv, cos_t, sin_t)


def _attention_kernel(q_ref, k_ref, v_ref, o_ref):
    s = lax.dot_general(q_ref[0], k_ref[0], (((1,), (1,)), ((), ())), preferred_element_type=F32)
    p = jnp.exp(s - jnp.max(s, axis=-1, keepdims=True))
    l = jnp.sum(p, axis=-1, keepdims=True)
    o = jnp.dot(p.astype(BF16), v_ref[0], preferred_element_type=F32)
    o_ref[0] = (o / l).astype(o_ref.dtype)


def _attention(q, k, v, heads, kv_heads, tq=512):
    b, t, _ = q.shape
    dq = q.shape[2] // heads
    dv = v.shape[2] // kv_heads
    g = heads // kv_heads
    tq = min(tq, t)
    return pl.pallas_call(
        _attention_kernel,
        grid=(b, heads, t // tq),
        in_specs=[pl.BlockSpec((1, tq, dq), lambda bi, h, qi: (bi, qi, h)),
                  pl.BlockSpec((1, t, dq), lambda bi, h, qi: (bi, 0, h // g)),
                  pl.BlockSpec((1, t, dv), lambda bi, h, qi: (bi, 0, h // g))],
        out_specs=pl.BlockSpec((1, tq, dv), lambda bi, h, qi: (bi, qi, h)),
        out_shape=jax.ShapeDtypeStruct((b, t, heads * dv), BF16),
        compiler_params=_cparams(("parallel", "parallel", "arbitrary"), 2 * _nbytes((tq, dq), BF16),
                                 2 * _nbytes((t, dq + dv), BF16), 2 * _nbytes((tq, dv), BF16),
                                 3 * _nbytes((tq, t), F32)),
        name="attention",
    )(q, k, v)


HG_CHUNK = 128
HG_DIRECT_CHUNK = 32
HG_MAX_LOG_RANGE = 80.0


def _split3(x):
    a = x.astype(BF16)
    r = x - a.astype(F32)
    b = r.astype(BF16)
    c = (r - b.astype(F32)).astype(BF16)
    return a, b, c


def _tri_masks(c_sz, reverse):
    row = lax.broadcasted_iota(jnp.int32, (c_sz, c_sz), 0)
    col = lax.broadcasted_iota(jnp.int32, (c_sz, c_sz), 1)
    keep = (row <= col) if reverse else (row >= col)
    return keep, jnp.where(keep, 1.0, 0.0).astype(BF16)


def _hgrn2_kernel(hq_ref, ff_ref, fb_ref, hi_ref, hg_ref, lbf_ref, lbb_ref, on_ref, o_ref,
                  accf_ref, accb_ref, sf_ref, sb_ref, dev_ref):
    t = hq_ref.shape[1]

    def chunk_inputs(f_ref, lb, rows):
        gate = lb + (1.0 - lb) * jax.nn.sigmoid(f_ref[0, rows, :])
        hq = hq_ref[0, rows, :]
        return hq * jax.nn.sigmoid(hq), 1.0 - gate, hi_ref[0, rows, :], jnp.log(gate)

    def log_decay(lf, tri):
        p0, p1, p2 = _split3(lf)
        return (jnp.dot(tri, p0, preferred_element_type=F32) + jnp.dot(tri, p1, preferred_element_type=F32)
                + jnp.dot(tri, p2, preferred_element_type=F32))

    def state_step(s_ref, q, k, v, b, b_last):
        s_t = s_ref[...]
        inter = lax.dot_general((q * jnp.exp(b)).astype(BF16), s_t.astype(BF16), (((1,), (1,)), ((), ())),
                                preferred_element_type=F32)
        kd = (k * jnp.exp(b_last - b)).astype(BF16)
        upd = lax.dot_general(v.astype(BF16), kd, (((0,), (0,)), ((), ())), preferred_element_type=F32)
        s_ref[...] = jnp.exp(b_last) * s_t + upd
        return inter

    def fast_step(f_ref, lb, s_ref, acc_ref, c, masks, reverse):
        c_sz = HG_CHUNK
        keep, tri = masks
        rows = pl.ds(pl.multiple_of(c * c_sz, c_sz), c_sz)
        q, k, v, lf = chunk_inputs(f_ref, lb, rows)
        b = log_decay(lf, tri)
        r = b[c_sz // 2:c_sz // 2 + 1, :]
        b_last = b[0:1, :] if reverse else b[c_sz - 1:c_sz, :]
        b_first = b[c_sz - 1:c_sz, :] if reverse else b[0:1, :]
        dev_ref[...] = jnp.maximum(dev_ref[...], jnp.maximum(jnp.abs(b_first - r), jnp.abs(b_last - r)))
        qe = (q * jnp.exp(b - r)).astype(BF16)
        ke = (k * jnp.exp(r - b)).astype(BF16)
        a = lax.dot_general(qe, ke, (((1,), (1,)), ((), ())), preferred_element_type=F32)
        a = jnp.where(keep, a, 0.0).astype(BF16)
        intra = jnp.dot(a, v.astype(BF16), preferred_element_type=F32)
        acc_ref[rows, :] = intra + state_step(s_ref, q, k, v, b, b_last)

    def direct_step(f_ref, lb, s_ref, acc_ref, c, masks, reverse):
        c_sz = HG_DIRECT_CHUNK
        _, tri = masks
        row1 = lax.broadcasted_iota(jnp.int32, (c_sz, 1), 0)
        rows = pl.ds(pl.multiple_of(c * c_sz, c_sz), c_sz)
        q, k, v, lf = chunk_inputs(f_ref, lb, rows)
        b = log_decay(lf, tri)
        b_last = b[0:1, :] if reverse else b[c_sz - 1:c_sz, :]
        o = state_step(s_ref, q, k, v, b, b_last)
        for s in range(c_sz):
            e = jnp.exp(jnp.minimum(b - b[s:s + 1, :], 0.0))
            a = jnp.sum(q * k[s:s + 1, :] * e, axis=-1, keepdims=True)
            a = jnp.where((row1 <= s) if reverse else (row1 >= s), a, 0.0)
            o = o + a * v[s:s + 1, :]
        acc_ref[rows, :] = o

    def scan_both(step, c_sz):
        nc = t // c_sz
        masks_f = _tri_masks(c_sz, False)
        masks_b = _tri_masks(c_sz, True)
        sf_ref[...] = jnp.zeros_like(sf_ref)
        sb_ref[...] = jnp.zeros_like(sb_ref)

        def body(ci, carry):
            step(ff_ref, lbf_ref[...], sf_ref, accf_ref, ci, masks_f, False)
            step(fb_ref, lbb_ref[...], sb_ref, accb_ref, nc - 1 - ci, masks_b, True)
            return carry

        lax.fori_loop(0, nc, body, 0)

    dev_ref[...] = jnp.zeros_like(dev_ref)
    scan_both(fast_step, HG_CHUNK)

    @pl.when(jnp.logical_not(jnp.max(dev_ref[...]) <= HG_MAX_LOG_RANGE))
    def _():
        scan_both(direct_step, HG_DIRECT_CHUNK)

    def finish(c, carry):
        rows = pl.ds(pl.multiple_of(c * 256, 256), 256)
        hg = hg_ref[0, rows, :]
        y = _rms_rows(accf_ref[rows, :] + accb_ref[rows, :], on_ref[...]) * (hg * jax.nn.sigmoid(hg))
        o_ref[0, rows, :] = y.astype(o_ref.dtype)
        return carry

    lax.fori_loop(0, t // 256, finish, 0)


def _hgrn2(proj, lb_f, lb_b, out_norm):
    b, t, _ = proj.shape

    def col(base):
        return pl.BlockSpec((1, t, LANES), lambda bi, h: (bi, 0, base + h))

    return pl.pallas_call(
        _hgrn2_kernel,
        grid=(b, HG_H),
        in_specs=[col(E_HQ), col(E_FF), col(E_FB), col(E_HI), col(E_HG),
                  pl.BlockSpec((1, LANES), lambda bi, h: (0, h)),
                  pl.BlockSpec((1, LANES), lambda bi, h: (0, h)),
                  pl.BlockSpec((1, LANES), lambda bi, h: (0, 0))],
        out_specs=pl.BlockSpec((1, t, LANES), lambda bi, h: (bi, 0, h)),
        out_shape=jax.ShapeDtypeStruct((b, t, HG_H * HG_DV), BF16),
        scratch_shapes=[pltpu.VMEM((t, HG_DV), F32), pltpu.VMEM((t, HG_DV), F32),
                        pltpu.VMEM((HG_DV, HG_DK), F32), pltpu.VMEM((HG_DV, HG_DK), F32), pltpu.VMEM((1, HG_DK), F32)],
        compiler_params=_cparams(("parallel", "parallel"), 10 * _nbytes((t, LANES), F32), 4 * _nbytes((t, LANES), F32)),
        name="hgrn2",
    )(proj, proj, proj, proj, proj, lb_f, lb_b, out_norm.reshape(1, HG_DV))


def _out_proj_kernel(a_ref, b_ref, wa_ref, wb_ref, x_ref, o_ref):
    acc = jnp.dot(a_ref[...], wa_ref[...], preferred_element_type=F32)
    acc = acc + jnp.dot(b_ref[...], wb_ref[...], preferred_element_type=F32)
    o_ref[...] = x_ref[...] + acc


def _out_proj(a, b, w, x, tm=1024, tn=512):
    n, ka = a.shape
    d = w.shape[1]
    tm = min(tm, n)
    return pl.pallas_call(
        _out_proj_kernel,
        grid=(n // tm, d // tn),
        in_specs=[pl.BlockSpec((tm, ka), lambda i, j: (i, 0)),
                  pl.BlockSpec((tm, ka), lambda i, j: (i, 0)),
                  pl.BlockSpec((ka, tn), lambda i, j: (0, j)),
                  pl.BlockSpec((ka, tn), lambda i, j: (1, j)),
                  pl.BlockSpec((tm, tn), lambda i, j: (i, j))],
        out_specs=pl.BlockSpec((tm, tn), lambda i, j: (i, j)),
        out_shape=jax.ShapeDtypeStruct((n, d), F32),
        compiler_params=_cparams(("parallel", "arbitrary"), 4 * _nbytes((tm, ka), BF16), 4 * _nbytes((ka, tn), BF16),
                                 5 * _nbytes((tm, tn), F32)),
        name="out_proj",
    )(a, b, w, w, x)


def _ffn_kernel(x_ref, g_ref, wg_ref, wu_ref, wd_ref, o_ref, xn_ref):
    f = pl.program_id(1)

    @pl.when(f == 0)
    def _():
        _norm_into(xn_ref, x_ref, g_ref)
        o_ref[...] = x_ref[...]

    xn = xn_ref[...]
    gt = jnp.dot(xn, wg_ref[...], preferred_element_type=F32)
    up = jnp.dot(xn, wu_ref[...], preferred_element_type=F32)
    act = (gt * jax.nn.sigmoid(gt) * up).astype(BF16)
    o_ref[...] += jnp.dot(act, wd_ref[...], preferred_element_type=F32)


def _ffn(x, g, wg, wu, wd, tm=512, tf=512):
    n, d = x.shape
    ff = wg.shape[1]
    tm = min(tm, n)
    return pl.pallas_call(
        _ffn_kernel,
        grid=(n // tm, ff // tf),
        in_specs=[pl.BlockSpec((tm, d), lambda i, f: (i, 0)),
                  pl.BlockSpec((1, d), lambda i, f: (0, 0)),
                  pl.BlockSpec((d, tf), lambda i, f: (0, f)),
                  pl.BlockSpec((d, tf), lambda i, f: (0, f)),
                  pl.BlockSpec((tf, d), lambda i, f: (f, 0))],
        out_specs=pl.BlockSpec((tm, d), lambda i, f: (i, 0)),
        out_shape=jax.ShapeDtypeStruct((n, d), F32),
        scratch_shapes=[pltpu.VMEM((tm, d), BF16)],
        compiler_params=_cparams(("parallel", "arbitrary"), 4 * _nbytes((tm, d), F32), _nbytes((tm, d), BF16),
                                 6 * _nbytes((d, tf), BF16), 4 * _nbytes((tm, tf), F32), _nbytes((tm, d), F32)),
        name="ffn",
    )(x, g.reshape(1, d), wg, wu, wd)


def _axial_prep_kernel(qd_ref, qs_ref, kd_ref, ks_ref, vd_ref, cq_ref, sq_ref, ck_ref, sk_ref,
                       q_ref, k_ref, v_ref, *, scale):
    def rope(x_ref, xs_ref, c_ref, s_ref, h, mul):
        sl = slice(h * LANES, (h + 1) * LANES)
        x = x_ref[:, sl]
        r = lax.rsqrt(jnp.mean(x * x, axis=-1, keepdims=True) + EPS)
        return ((x * c_ref[...] + xs_ref[:, sl] * s_ref[...]) * (r * mul)).astype(BF16)

    for h in range(AX_H):
        q_ref[:, h * LANES:(h + 1) * LANES] = rope(qd_ref, qs_ref, cq_ref, sq_ref, h, scale)
    for h in range(AX_KV):
        k_ref[:, h * LANES:(h + 1) * LANES] = rope(kd_ref, ks_ref, ck_ref, sk_ref, h, 1.0)
    v_ref[...] = vd_ref[...].astype(BF16)


def _axial_prep(proj, cq, sq, ck, sk, seq, tm=512):
    n = proj.shape[0]
    tm = min(tm, seq)
    nt = seq // tm
    qw, kw = AX_H * AX_HD, AX_KV * AX_HD
    tab = pl.BlockSpec((tm, LANES), lambda i: (i % nt, 0))
    return pl.pallas_call(
        functools.partial(_axial_prep_kernel, scale=float(AX_HD ** -0.5)),
        grid=(n // tm,),
        in_specs=[pl.BlockSpec((tm, qw), lambda i: (i, O_QD * LANES // qw)),
                  pl.BlockSpec((tm, qw), lambda i: (i, O_QDS * LANES // qw)),
                  pl.BlockSpec((tm, kw), lambda i: (i, O_KD * LANES // kw)),
                  pl.BlockSpec((tm, kw), lambda i: (i, O_KDS * LANES // kw)),
                  pl.BlockSpec((tm, kw), lambda i: (i, O_VD * LANES // kw)),
                  tab, tab, tab, tab],
        out_specs=[pl.BlockSpec((tm, qw), lambda i: (i, 0)),
                   pl.BlockSpec((tm, kw), lambda i: (i, 0)),
                   pl.BlockSpec((tm, kw), lambda i: (i, 0))],
        out_shape=[jax.ShapeDtypeStruct((n, qw), BF16), jax.ShapeDtypeStruct((n, kw), BF16),
                   jax.ShapeDtypeStruct((n, kw), BF16)],
        compiler_params=_cparams(("parallel",), 4 * _nbytes((tm, qw), F32), 6 * _nbytes((tm, kw), F32),
                                 8 * _nbytes((tm, LANES), F32), 2 * _nbytes((tm, qw + 2 * kw), BF16)),
        name="axial_prep",
    )(proj, proj, proj, proj, proj, cq, sq, ck, sk)


WIN_BLOCK = 128


def _window_kernel(sink_ref, slope_ref, q_ref, k_ref, v_ref, o_ref, *, scale):
    t = q_ref.shape[1]
    wb = WIN_BLOCK
    span = 3 * wb
    pair = pl.program_id(1)
    lane = lax.broadcasted_iota(jnp.int32, (wb, LANES), 1)
    low = lane < WIN_HD
    delta = lax.broadcasted_iota(jnp.int32, (wb, span), 0) - lax.broadcasted_iota(jnp.int32, (wb, span), 1)

    def body(qb, carry):
        start = jnp.clip((qb - 1) * wb, 0, t - span)
        start = pl.multiple_of(start, wb)
        kwin = k_ref[0, pl.ds(start, span), :].astype(BF16)
        vwin = v_ref[0, pl.ds(start, span), :].astype(BF16)
        qrows = pl.ds(pl.multiple_of(qb * wb, wb), wb)
        q2 = q_ref[0, qrows, :] * scale
        dist = jnp.abs(delta + (qb * wb - start))
        valid = dist <= WINDOW
        distf = dist.astype(F32)
        outs = []
        for j in range(2):
            head = 2 * pair + j
            sink = sink_ref[head]
            qh = jnp.where(low if j == 0 else jnp.logical_not(low), q2, 0.0).astype(BF16)
            s = lax.dot_general(qh, kwin, (((1,), (1,)), ((), ())), preferred_element_type=F32)
            s = jnp.where(valid, s - slope_ref[head] * distf, -jnp.inf)
            m = jnp.maximum(jnp.max(s, axis=-1, keepdims=True), sink)
            e = jnp.exp(s - m)
            den = jnp.sum(e, axis=-1, keepdims=True) + jnp.exp(sink - m)
            outs.append(jnp.dot(e.astype(BF16), vwin, preferred_element_type=F32) / den)
        o_ref[0, qrows, :] = jnp.where(low, outs[0], outs[1]).astype(o_ref.dtype)
        return carry

    lax.fori_loop(0, t // wb, body, 0)


def _window_attention(proj, sink, slopes):
    b, t, _ = proj.shape
    pairs = WIN_H // 2
    per_kv = pairs // WIN_KV
    smem = pl.BlockSpec(memory_space=pltpu.SMEM)
    return pl.pallas_call(
        functools.partial(_window_kernel, scale=float(WIN_HD ** -0.5)),
        grid=(b, pairs),
        in_specs=[smem, smem,
                  pl.BlockSpec((1, t, LANES), lambda bi, p: (bi, 0, O_QC + p)),
                  pl.BlockSpec((1, t, LANES), lambda bi, p: (bi, 0, O_KC + p // per_kv)),
                  pl.BlockSpec((1, t, LANES), lambda bi, p: (bi, 0, O_VC + p // per_kv))],
        out_specs=pl.BlockSpec((1, t, LANES), lambda bi, p: (bi, 0, p)),
        out_shape=jax.ShapeDtypeStruct((b, t, WIN_H * WIN_HD), BF16),
        compiler_params=_cparams(("parallel", "parallel"), 6 * _nbytes((t, LANES), F32), 2 * _nbytes((t, LANES), BF16)),
        name="window_attention",
    )(sink, slopes, proj, proj, proj)


MOE_TM = 512
R_E1, R_E2, R_W1, R_W2, R_RANK1, R_RANK2 = range(6)


def _router_kernel(x_ref, g_ref, r_ref, route_ref, cnt_ref, carry_ref):
    @pl.when(pl.program_id(0) == 0)
    def _():
        carry_ref[...] = jnp.zeros_like(carry_ref)

    tm = x_ref.shape[0]
    xn = _rms_rows(x_ref[...], g_ref[...])
    logits = [jnp.sum(xn * r_ref[e:e + 1, :], axis=-1, keepdims=True) for e in range(N_EXPERTS)]

    def top(ls):
        m = functools.reduce(jnp.maximum, ls)
        idx = jnp.full_like(m, N_EXPERTS).astype(jnp.int32)
        for e in reversed(range(N_EXPERTS)):
            idx = jnp.where(ls[e] == m, e, idx)
        return m, idx

    m1, i1 = top(logits)
    m2, i2 = top([jnp.where(i1 == e, -jnp.inf, logits[e]) for e in range(N_EXPERTS)])
    e2 = jnp.exp(m2 - m1)
    w1 = 1.0 / (1.0 + e2)
    w2 = e2 / (1.0 + e2)
    lane = lax.broadcasted_iota(jnp.int32, route_ref.shape, 1)
    sel1, sel2 = lane == i1, lane == i2
    onehot = jnp.where(jnp.logical_or(sel1, sel2), 1.0, 0.0)
    row = lax.broadcasted_iota(jnp.int32, (tm, tm), 0)
    col = lax.broadcasted_iota(jnp.int32, (tm, tm), 1)
    earlier = jnp.where(row > col, 1.0, 0.0).astype(BF16)
    before = carry_ref[...] + jnp.dot(earlier, onehot.astype(BF16), preferred_element_type=F32)
    rank1 = jnp.sum(jnp.where(sel1, before, 0.0), axis=-1, keepdims=True)
    rank2 = jnp.sum(jnp.where(sel2, before, 0.0), axis=-1, keepdims=True)
    carry_ref[...] = carry_ref[...] + jnp.sum(onehot, axis=0, keepdims=True)
    cnt_ref[...] = carry_ref[...]
    rec = jnp.zeros(route_ref.shape, F32)
    for pos, val in ((R_E1, i1.astype(F32)), (R_E2, i2.astype(F32)), (R_W1, w1), (R_W2, w2),
                     (R_RANK1, rank1), (R_RANK2, rank2)):
        rec = jnp.where(lane == pos, val, rec)
    route_ref[...] = rec


def _router(x, g, router_t, tm=256):
    n, d = x.shape
    tm = min(tm, n)
    return pl.pallas_call(
        _router_kernel,
        grid=(n // tm,),
        in_specs=[pl.BlockSpec((tm, d), lambda i: (i, 0)),
                  pl.BlockSpec((1, d), lambda i: (0, 0)),
                  pl.BlockSpec((N_EXPERTS, d), lambda i: (0, 0))],
        out_specs=[pl.BlockSpec((tm, LANES), lambda i: (i, 0)),
                   pl.BlockSpec((1, LANES), lambda i: (0, 0))],
        out_shape=[jax.ShapeDtypeStruct((n, LANES), F32), jax.ShapeDtypeStruct((1, LANES), F32)],
        scratch_shapes=[pltpu.VMEM((1, LANES), F32)],
        compiler_params=_cparams(("arbitrary",), 6 * _nbytes((tm, d), F32)),
        name="router",
    )(x, g.reshape(1, d), router_t)


def _row_copy(src_hbm, src_row, dst, dst_row, sem):
    return pltpu.make_async_copy(src_hbm.at[pl.ds(src_row, 1)], dst.at[pl.ds(dst_row, 1)], sem)


def _moe_scatter_kernel(fill_ref, p1_ref, p2_ref, x_hbm, xs_hbm, zero_ref, sem, zsem, *, tile):
    i = pl.program_id(0)
    rows = p1_ref.shape[-1]

    @pl.when(i == 0)
    def _():
        zero_ref[...] = jnp.zeros_like(zero_ref)
        for k in range(fill_ref.shape[0]):
            @pl.when(fill_ref[k] >= 0)
            def _():
                start = pl.multiple_of(fill_ref[k], tile)
                fill = pltpu.make_async_copy(zero_ref, xs_hbm.at[pl.ds(start, tile)], zsem)
                fill.start()
                fill.wait()

    def issue(r, carry):
        _row_copy(x_hbm, i * rows + r, xs_hbm, p1_ref[0, 0, r], sem).start()
        _row_copy(x_hbm, i * rows + r, xs_hbm, p2_ref[0, 0, r], sem).start()
        return carry

    lax.fori_loop(0, rows, issue, 0)
    for _ in range(2):
        pltpu.make_async_copy(x_hbm.at[pl.ds(0, rows)], xs_hbm.at[pl.ds(0, rows)], sem).wait()


def _moe_scatter(x, pos1, pos2, fill_rows, total_rows, tile, rows=256):
    n, d = x.shape
    rows = min(rows, n)
    smem_blk = pl.BlockSpec((1, 1, rows), lambda i, fill: (i, 0, 0), memory_space=pltpu.SMEM)
    return pl.pallas_call(
        functools.partial(_moe_scatter_kernel, tile=tile),
        grid_spec=pltpu.PrefetchScalarGridSpec(
            num_scalar_prefetch=1,
            grid=(n // rows,),
            in_specs=[smem_blk, smem_blk, pl.BlockSpec(memory_space=pl.ANY)],
            out_specs=pl.BlockSpec(memory_space=pl.ANY),
            scratch_shapes=[pltpu.VMEM((tile, d), x.dtype), pltpu.SemaphoreType.DMA, pltpu.SemaphoreType.DMA]),
        out_shape=jax.ShapeDtypeStruct((total_rows, d), x.dtype),
        compiler_params=_cparams(("arbitrary",), _nbytes((tile, d), x.dtype)),
        name="moe_scatter",
    )(fill_rows, pos1.reshape(n // rows, 1, rows), pos2.reshape(n // rows, 1, rows), x)


def _moe_experts_kernel(te_ref, nv_ref, xs_ref, g_ref, wg_ref, wu_ref, wd_ref, y_ref, xn_ref):
    valid = pl.program_id(0) < nv_ref[0]

    @pl.when(valid)
    def _():
        _norm_into(xn_ref, xs_ref, g_ref)
        xn = xn_ref[...]
        gt = jnp.dot(xn, wg_ref[0], preferred_element_type=F32)
        up = jnp.dot(xn, wu_ref[0], preferred_element_type=F32)
        act = (gt * jax.nn.sigmoid(gt) * up).astype(BF16)
        y_ref[...] = jnp.dot(act, wd_ref[0], preferred_element_type=F32)

    @pl.when(jnp.logical_not(valid))
    def _():
        y_ref[...] = jnp.zeros_like(y_ref)


def _moe_experts(xs, g, wg, wu, wd, tile_expert, n_valid, tile):
    rows, d = xs.shape
    ff = wg.shape[2]
    once = dict(pipeline_mode=pl.Buffered(1))

    def row_blk(i, te, nv):
        return (jnp.minimum(i, nv[0] - 1), 0)

    def w_blk(i, te, nv):
        return (te[i], 0, 0)

    return pl.pallas_call(
        _moe_experts_kernel,
        grid_spec=pltpu.PrefetchScalarGridSpec(
            num_scalar_prefetch=2,
            grid=(rows // tile,),
            in_specs=[pl.BlockSpec((tile, d), row_blk),
                      pl.BlockSpec((1, d), lambda i, te, nv: (0, 0)),
                      pl.BlockSpec((1, d, ff), w_blk, **once),
                      pl.BlockSpec((1, d, ff), w_blk, **once),
                      pl.BlockSpec((1, ff, d), w_blk, **once)],
            out_specs=pl.BlockSpec((tile, d), lambda i, te, nv: (i, 0)),
            scratch_shapes=[pltpu.VMEM((tile, d), BF16)]),
        out_shape=jax.ShapeDtypeStruct((rows, d), F32),
        compiler_params=_cparams(("arbitrary",), 4 * _nbytes((tile, d), F32), _nbytes((tile, d), BF16),
                                 3 * _nbytes((d, ff), BF16), 4 * _nbytes((tile, ff), F32), _nbytes((tile, d), F32)),
        name="moe_experts",
    )(tile_expert, n_valid, xs, g.reshape(1, d), wg, wu, wd)


def _moe_combine_kernel(p1_ref, p2_ref, x_ref, route_ref, g_ref, y_hbm, o_ref, ya_ref, yb_ref, sem, *, final):
    rows = x_ref.shape[0]

    def issue(r, carry):
        _row_copy(y_hbm, p1_ref[0, 0, r], ya_ref, r, sem).start()
        _row_copy(y_hbm, p2_ref[0, 0, r], yb_ref, r, sem).start()
        return carry

    lax.fori_loop(0, rows, issue, 0)
    for dst in (ya_ref, yb_ref):
        pltpu.make_async_copy(y_hbm.at[pl.ds(0, rows)], dst, sem).wait()
    lane = lax.broadcasted_iota(jnp.int32, route_ref.shape, 1)
    w1 = jnp.sum(jnp.where(lane == R_W1, route_ref[...], 0.0), axis=-1, keepdims=True)
    w2 = jnp.sum(jnp.where(lane == R_W2, route_ref[...], 0.0), axis=-1, keepdims=True)
    out = x_ref[...] + w1 * ya_ref[...] + w2 * yb_ref[...]
    o_ref[...] = _rms_rows(out, g_ref[...]) if final else out


def _moe_combine(x, route, pos1, pos2, y, gain, row0, nrows, final, rows=256):
    n, d = x.shape
    rows = min(rows, nrows)
    off = row0 // rows
    smem_blk = pl.BlockSpec((1, 1, rows), lambda i: (i + off, 0, 0), memory_space=pltpu.SMEM)
    return pl.pallas_call(
        functools.partial(_moe_combine_kernel, final=final),
        grid=(nrows // rows,),
        in_specs=[smem_blk, smem_blk,
                  pl.BlockSpec((rows, d), lambda i: (i + off, 0)),
                  pl.BlockSpec((rows, LANES), lambda i: (i + off, 0)),
                  pl.BlockSpec((1, d), lambda i: (0, 0)),
                  pl.BlockSpec(memory_space=pl.ANY)],
        out_specs=pl.BlockSpec((rows, d), lambda i: (i, 0)),
        out_shape=jax.ShapeDtypeStruct((nrows, d), F32),
        scratch_shapes=[pltpu.VMEM((rows, d), F32), pltpu.VMEM((rows, d), F32), pltpu.SemaphoreType.DMA],
        compiler_params=_cparams(("arbitrary",), 8 * _nbytes((rows, d), F32)),
        name="moe_combine",
    )(pos1.reshape(n // rows, 1, rows), pos2.reshape(n // rows, 1, rows), x, route, gain.reshape(1, d), y)


def _swap_halves(w, width):
    lead = w.shape[:-1]
    return jnp.flip(w.reshape(lead + (-1, 2, width // 2)), axis=-2).reshape(w.shape)


def _pad_cols(w, width):
    return jnp.pad(w, [(0, 0)] * (w.ndim - 1) + [(0, width - w.shape[-1])])


def _even_in_weight(w):
    kr = w[:, 2 * MLA_LORA:2 * MLA_LORA + MLA_ROPE]
    return jnp.concatenate([w[:, :2 * MLA_LORA], _pad_cols(kr, LANES), _pad_cols(_swap_halves(kr, MLA_ROPE), LANES),
                            w[:, 2 * MLA_LORA + MLA_ROPE:]], axis=1).astype(BF16)


def _mla_q_weight(w):
    w = w.reshape(MLA_LORA, MLA_H, MLA_NOPE + MLA_ROPE)
    rope = w[..., MLA_NOPE:]
    w = jnp.concatenate([w[..., :MLA_NOPE], _pad_cols(rope, LANES), _pad_cols(_swap_halves(rope, MLA_ROPE), LANES)], -1)
    return w.reshape(MLA_LORA, MLA_H * 3 * LANES).astype(BF16)


def _odd_in_weight(w):
    d = w.shape[0]
    o = 0
    parts = {}
    for name, width in (("qc", WIN_H * WIN_HD), ("kc", WIN_KV * WIN_HD), ("vc", WIN_KV * WIN_HD),
                        ("qd", AX_H * AX_HD), ("kd", AX_KV * AX_HD), ("vd", AX_KV * AX_HD)):
        parts[name] = w[:, o:o + width]
        o += width

    def dup(x):
        x = x.reshape(d, WIN_KV, WIN_HD)
        return jnp.concatenate([x, x], axis=-1).reshape(d, WIN_KV * LANES)

    half = AX_HD // 2
    out = jnp.concatenate([parts["qd"], _swap_halves(parts["qd"], half), parts["qc"], parts["kd"],
                           _swap_halves(parts["kd"], half), parts["vd"], dup(parts["kc"]), dup(parts["vc"])], axis=1)
    return _pad_cols(out, O_END * LANES).astype(BF16)


def _rope_tables(pos, dim):
    inv = ROPE_THETA ** (-jnp.arange(0, dim, 2, dtype=F32) / dim)
    ang = pos[:, None] * inv[None, :]
    cos, sin = jnp.cos(ang), jnp.sin(ang)
    return jnp.concatenate([cos, cos], -1), jnp.concatenate([-sin, sin], -1)


def _alibi_slopes(n):
    return jnp.asarray(2.0 ** (-8.0 * np.arange(1, n + 1) / n), dtype=F32)


def _mixer_even(x, bsz, seq, j, norm_g, w_in, q_norm, w_uq, kv_norm, w_ukv, lb_fwd, lb_bwd, out_norm, w_out):
    n = x.shape[0]
    proj = _norm_matmul(x, norm_g, _even_in_weight(w_in), tn=5 * LANES, out_dtype=F32)
    cos, sin = _rope_tables(jnp.arange(seq, dtype=F32), MLA_ROPE)
    q, k, v = _mla_up(proj, q_norm, kv_norm, _mla_q_weight(w_uq), w_ukv.astype(BF16),
                      _pad_cols(cos, LANES), _pad_cols(sin, LANES), seq)
    o_a = _attention(q.reshape(bsz, seq, -1), k.reshape(bsz, seq, -1), v.reshape(bsz, seq, -1), MLA_H, MLA_H)

    def lower_bound(tab):
        return jnp.cumsum(jax.nn.softmax(tab.astype(F32), axis=0), axis=0)[j].reshape(1, HG_H * HG_DK)

    o_b = _hgrn2(proj.reshape(bsz, seq, -1), lower_bound(lb_fwd), lower_bound(lb_bwd), out_norm)
    return _out_proj(o_a.reshape(n, -1), o_b.reshape(n, -1), w_out.astype(BF16), x)


def _mixer_odd(x, bsz, seq, norm_g, w_in, sink, q_norm, k_norm, w_out):
    n = x.shape[0]
    proj = _norm_matmul(x, norm_g, _odd_in_weight(w_in), tn=4 * LANES, out_dtype=F32)
    o_c = _window_attention(proj.reshape(bsz, seq, -1), sink.astype(F32), _alibi_slopes(WIN_H))
    pos = jnp.arange(seq)
    half = AX_HD // 2
    c_row, s_row = _rope_tables((pos // GRID_W).astype(F32), half)
    c_col, s_col = _rope_tables((pos % GRID_W).astype(F32), half)
    cos = jnp.concatenate([c_row, c_col], -1)
    sin = jnp.concatenate([s_row, s_col], -1)

    def tables(g):
        g = g.astype(F32)
        return cos * g[None, :], sin * _swap_halves(g, half)[None, :]

    cq, sq = tables(q_norm)
    ck, sk = tables(k_norm)
    q, k, v = _axial_prep(proj, cq, sq, ck, sk, seq)
    o_d = _attention(q.reshape(bsz, seq, -1), k.reshape(bsz, seq, -1), v.reshape(bsz, seq, -1), AX_H, AX_KV)
    return _out_proj(o_c.reshape(n, -1), o_d.reshape(n, -1), w_out.astype(BF16), x)


def _moe(x, norm_g, router, w_gate, w_up, w_down, out_gain, out_splits):
    n = x.shape[0]
    tile = MOE_TM
    route, counts = _router(x, norm_g, router.astype(F32).T)
    counts = counts[0, :N_EXPERTS].astype(jnp.int32)
    padded = (counts + tile - 1) // tile * tile
    ends = jnp.cumsum(padded)
    starts = ends - padded
    e1, e2 = route[:, R_E1].astype(jnp.int32), route[:, R_E2].astype(jnp.int32)
    pos1 = starts[e1] + route[:, R_RANK1].astype(jnp.int32)
    pos2 = starts[e2] + route[:, R_RANK2].astype(jnp.int32)
    n_tiles = 2 * n // tile + N_EXPERTS
    n_valid = ends[-1] // tile
    tile_row = jnp.minimum(jnp.arange(n_tiles), n_valid - 1) * tile
    tile_expert = jnp.sum(tile_row[:, None] >= ends[None, :], axis=1).astype(jnp.int32)
    group_tail = jnp.where(padded > 0, ends - tile, -1)
    spare = (n_valid + jnp.arange(N_EXPERTS)) * tile
    fill_rows = jnp.concatenate([group_tail, jnp.where(spare < n_tiles * tile, spare, -1)]).astype(jnp.int32)
    xs = _moe_scatter(x, pos1, pos2, fill_rows, n_tiles * tile, tile)
    y = _moe_experts(xs, norm_g, w_gate.astype(BF16), w_up.astype(BF16), w_down.astype(BF16),
                     tile_expert, n_valid.reshape(1).astype(jnp.int32), tile)
    final = out_gain is not None
    gain = out_gain if final else jnp.ones((x.shape[1],), F32)
    return [_moe_combine(x, route, pos1, pos2, y, gain, row0, nrows, final) for row0, nrows in out_splits]


def _trunk(x, bsz, seq, norm_mix_e, w_in_e, mla_q_norm, mla_w_uq, mla_kv_norm, mla_w_ukv, hg_lb_fwd, hg_lb_bwd,
           hg_out_norm, w_out_e, norm_ffn_e, ffn_w_gate, ffn_w_up, ffn_w_down, norm_mix_o, w_in_o, win_sink,
           ax_q_norm, ax_k_norm, w_out_o, norm_ffn_o, moe_router, moe_w_gate, moe_w_up, moe_w_down, final_norm,
           out_splits):
    assert DEPTH % 2 == 0
    for l in range(DEPTH):
        j = l // 2
        if l % 2 == 0:
            x = _mixer_even(x, bsz, seq, j, norm_mix_e[j], w_in_e[j], mla_q_norm[j], mla_w_uq[j], mla_kv_norm[j],
                            mla_w_ukv[j], hg_lb_fwd, hg_lb_bwd, hg_out_norm[j], w_out_e[j])
            x = _ffn(x, norm_ffn_e[j], ffn_w_gate[j].astype(BF16), ffn_w_up[j].astype(BF16),
                     ffn_w_down[j].astype(BF16))
        else:
            x = _mixer_odd(x, bsz, seq, norm_mix_o[j], w_in_o[j], win_sink[j], ax_q_norm[j], ax_k_norm[j], w_out_o[j])
            last = l == DEPTH - 1
            outs = _moe(x, norm_ffn_o[j], moe_router[j], moe_w_gate[j], moe_w_up[j], moe_w_down[j],
                        final_norm if last else None, out_splits if last else [(0, x.shape[0])])
            if last:
                return outs
            x = outs[0]


def kernel(x_prompt, x_sample, norm_mix_e, w_in_e, mla_q_norm, mla_w_uq, mla_kv_norm, mla_w_ukv, hg_lb_fwd, hg_lb_bwd, hg_out_norm, w_out_e, norm_ffn_e, ffn_w_gate, ffn_w_up, ffn_w_down, norm_mix_o, w_in_o, win_sink, ax_q_norm, ax_k_norm, w_out_o, norm_ffn_o, moe_router, moe_w_gate, moe_w_up, moe_w_down, final_norm):
    bp, seq, d = x_prompt.shape
    bs = x_sample.shape[0]
    assert x_sample.shape[1:] == (seq, d)
    x = jnp.concatenate([x_prompt.reshape(bp * seq, d), x_sample.reshape(bs * seq, d)], axis=0)
    y_prompt, y_sample = _trunk(
        x, bp + bs, seq, norm_mix_e, w_in_e, mla_q_norm, mla_w_uq, mla_kv_norm, mla_w_ukv, hg_lb_fwd, hg_lb_bwd,
        hg_out_norm, w_out_e, norm_ffn_e, ffn_w_gate, ffn_w_up, ffn_w_down, norm_mix_o, w_in_o, win_sink, ax_q_norm,
        ax_k_norm, w_out_o, norm_ffn_o, moe_router, moe_w_gate, moe_w_up, moe_w_down, final_norm,
        [(0, bp * seq), (bp * seq, bs * seq)])
    return (y_prompt.reshape(bp, seq, d), y_sample.reshape(bs, seq, d))
```

```python
import functools

import jax
import jax.numpy as jnp
import numpy as np
from jax import lax
from jax.experimental import pallas as pl
from jax.experimental.pallas import tpu as pltpu

D_MODEL = 2048
DEPTH = 2
GRID_W = 64
EPS = 1e-6
ROPE_THETA = 10000.0

MLA_H = 8
MLA_NOPE = 128
MLA_ROPE = 64
MLA_V = 128
MLA_LORA = D_MODEL // 4

HG_H = 8
HG_DK = 128
HG_DV = 128

WIN_H = 16
WIN_KV = 2
WIN_HD = 64
WINDOW = 128

AX_H = 8
AX_KV = 2
AX_HD = 128

FF_DENSE = 5632
N_EXPERTS = 8
FF_EXPERT = 1408

LANES = 128
VMEM_CAP = 60000 * 1024
BF16 = jnp.bfloat16
F32 = jnp.float32

E_CQ, E_CKV, E_KRA, E_KRB, E_HQ, E_FF, E_FB, E_HI, E_HG, E_END = 0, 4, 8, 9, 10, 18, 26, 34, 42, 50
O_QD, O_QDS, O_QC, O_KD, O_KDS, O_VD, O_KC, O_VC, O_END = 0, 8, 16, 24, 26, 28, 30, 32, 36


def _cparams(sem, *block_bytes):
    need = int(sum(block_bytes)) + (6 << 20)
    return pltpu.CompilerParams(dimension_semantics=sem, vmem_limit_bytes=min(max(need, 16 << 20), VMEM_CAP))


def _nbytes(shape, dtype):
    return int(np.prod(shape)) * jnp.dtype(dtype).itemsize


def _rms_rows(x, g):
    return x * lax.rsqrt(jnp.mean(x * x, axis=-1, keepdims=True) + EPS) * g


def _norm_into(dst_ref, x_ref, g_ref, chunk=256):
    rows = x_ref.shape[0]
    chunk = min(chunk, rows)

    def body(c, carry):
        r = pl.ds(pl.multiple_of(c * chunk, chunk), chunk)
        dst_ref[r, :] = _rms_rows(x_ref[r, :].astype(F32), g_ref[...]).astype(dst_ref.dtype)
        return carry

    lax.fori_loop(0, rows // chunk, body, 0)


def _norm_matmul_kernel(x_ref, g_ref, w_ref, o_ref, xn_ref):
    @pl.when(pl.program_id(1) == 0)
    def _():
        _norm_into(xn_ref, x_ref, g_ref)

    o_ref[...] = jnp.dot(xn_ref[...], w_ref[...], preferred_element_type=F32).astype(o_ref.dtype)


def _norm_matmul(x, g, w, tn, out_dtype, tm=1024):
    n, k = x.shape
    nout = w.shape[1]
    tm = min(tm, n)
    return pl.pallas_call(
        _norm_matmul_kernel,
        grid=(n // tm, nout // tn),
        in_specs=[pl.BlockSpec((tm, k), lambda i, j: (i, 0)),
                  pl.BlockSpec((1, k), lambda i, j: (0, 0)),
                  pl.BlockSpec((k, tn), lambda i, j: (0, j))],
        out_specs=pl.BlockSpec((tm, tn), lambda i, j: (i, j)),
        out_shape=jax.ShapeDtypeStruct((n, nout), out_dtype),
        scratch_shapes=[pltpu.VMEM((tm, k), BF16)],
        compiler_params=_cparams(("parallel", "arbitrary"), 2 * _nbytes((tm, k), x.dtype), _nbytes((tm, k), BF16),
                                 2 * _nbytes((k, tn), BF16), 3 * _nbytes((tm, tn), F32)),
        name="norm_matmul",
    )(x, g.reshape(1, k), w)


def _mla_up_kernel(cq_ref, ckv_ref, kra_ref, krb_ref, qn_ref, kvn_ref, wq_ref, wkv_ref, cos_ref, sin_ref,
                   q_ref, k_ref, v_ref, cqn_ref, ckvn_ref, kr_ref, *, scale):
    @pl.when(pl.program_id(1) == 0)
    def _():
        _norm_into(cqn_ref, cq_ref, qn_ref)
        _norm_into(ckvn_ref, ckv_ref, kvn_ref)
        kr_ref[...] = (kra_ref[...] * cos_ref[...] + krb_ref[...] * sin_ref[...]).astype(BF16)

    q = jnp.dot(cqn_ref[...], wq_ref[...], preferred_element_type=F32)
    q_rope = q[:, LANES:2 * LANES] * cos_ref[...] + q[:, 2 * LANES:] * sin_ref[...]
    q_ref[:, :LANES] = (q[:, :LANES] * scale).astype(BF16)
    q_ref[:, LANES:] = (q_rope * scale).astype(BF16)
    kv = jnp.dot(ckvn_ref[...], wkv_ref[...], preferred_element_type=F32)
    k_ref[:, :LANES] = kv[:, :LANES].astype(BF16)
    k_ref[:, LANES:] = kr_ref[...]
    v_ref[...] = kv[:, LANES:].astype(BF16)


def _mla_up(proj, qn, kvn, wq, wkv, cos_t, sin_t, seq, tm=512):
    n = proj.shape[0]
    tm = min(tm, seq)
    nt = seq // tm
    lora = MLA_LORA
    scale = float((MLA_NOPE + MLA_ROPE) ** -0.5)
    return pl.pallas_call(
        functools.partial(_mla_up_kernel, scale=scale),
        grid=(n // tm, MLA_H),
        in_specs=[pl.BlockSpec((tm, lora), lambda i, h: (i, 0)),
                  pl.BlockSpec((tm, lora), lambda i, h: (i, 1)),
                  pl.BlockSpec((tm, LANES), lambda i, h: (i, E_KRA)),
                  pl.BlockSpec((tm, LANES), lambda i, h: (i, E_KRB)),
                  pl.BlockSpec((1, lora), lambda i, h: (0, 0)),
                  pl.BlockSpec((1, lora), lambda i, h: (0, 0)),
                  pl.BlockSpec((lora, 3 * LANES), lambda i, h: (0, h)),
                  pl.BlockSpec((lora, 2 * LANES), lambda i, h: (0, h)),
                  pl.BlockSpec((tm, LANES), lambda i, h: (i % nt, 0)),
                  pl.BlockSpec((tm, LANES), lambda i, h: (i % nt, 0))],
        out_specs=[pl.BlockSpec((tm, 2 * LANES), lambda i, h: (i, h)),
                   pl.BlockSpec((tm, 2 * LANES), lambda i, h: (i, h)),
                   pl.BlockSpec((tm, LANES), lambda i, h: (i, h))],
        out_shape=[jax.ShapeDtypeStruct((n, MLA_H * 2 * LANES), BF16),
                   jax.ShapeDtypeStruct((n, MLA_H * 2 * LANES), BF16),
                   jax.ShapeDtypeStruct((n, MLA_H * LANES), BF16)],
        scratch_shapes=[pltpu.VMEM((tm, lora), BF16), pltpu.VMEM((tm, lora), BF16), pltpu.VMEM((tm, LANES), BF16)],
        compiler_params=_cparams(("parallel", "arbitrary"), 4 * _nbytes((tm, lora), F32), 12 * _nbytes((tm, LANES), F32),
                                 4 * _nbytes((lora, 5 * LANES), BF16), 8 * _nbytes((tm, 3 * LANES), F32)),
        name="mla_up",
    )(proj, proj, proj, proj, qn.reshape(1, lora), kvn.reshape(1, lora), wq, wkv, cos_t, sin_t)


def _attention_kernel(q_ref, k_ref, v_ref, o_ref):
    s = lax.dot_general(q_ref[0], k_ref[0], (((1,), (1,)), ((), ())), preferred_element_type=F32)
    p = jnp.exp(s - jnp.max(s, axis=-1, keepdims=True))
    l = jnp.sum(p, axis=-1, keepdims=True)
    o = jnp.dot(p.astype(BF16), v_ref[0], preferred_element_type=F32)
    o_ref[0] = (o / l).astype(o_ref.dtype)


def _attention(q, k, v, heads, kv_heads, tq=512):
    b, t, _ = q.shape
    dq = q.shape[2] // heads
    dv = v.shape[2] // kv_heads
    g = heads // kv_heads
    tq = min(tq, t)
    return pl.pallas_call(
        _attention_kernel,
        grid=(b, heads, t // tq),
        in_specs=[pl.BlockSpec((1, tq, dq), lambda bi, h, qi: (bi, qi, h)),
                  pl.BlockSpec((1, t, dq), lambda bi, h, qi: (bi, 0, h // g)),
                  pl.BlockSpec((1, t, dv), lambda bi, h, qi: (bi, 0, h // g))],
        out_specs=pl.BlockSpec((1, tq, dv), lambda bi, h, qi: (bi, qi, h)),
        out_shape=jax.ShapeDtypeStruct((b, t, heads * dv), BF16),
        compiler_params=_cparams(("parallel", "parallel", "arbitrary"), 2 * _nbytes((tq, dq), BF16),
                                 2 * _nbytes((t, dq + dv), BF16), 2 * _nbytes((tq, dv), BF16),
                                 3 * _nbytes((tq, t), F32)),
        name="attention",
    )(q, k, v)


HG_CHUNK = 128
HG_DIRECT_CHUNK = 32
HG_MAX_LOG_RANGE = 80.0


def _split3(x):
    a = x.astype(BF16)
    r = x - a.astype(F32)
    b = r.astype(BF16)
    c = (r - b.astype(F32)).astype(BF16)
    return a, b, c


def _tri_masks(c_sz, reverse):
    row = lax.broadcasted_iota(jnp.int32, (c_sz, c_sz), 0)
    col = lax.broadcasted_iota(jnp.int32, (c_sz, c_sz), 1)
    keep = (row <= col) if reverse else (row >= col)
    return keep, jnp.where(keep, 1.0, 0.0).astype(BF16)


def _hgrn2_kernel(hq_ref, ff_ref, fb_ref, hi_ref, hg_ref, lbf_ref, lbb_ref, on_ref, o_ref,
                  accf_ref, accb_ref, sf_ref, sb_ref, dev_ref):
    t = hq_ref.shape[1]

    def chunk_inputs(f_ref, lb, rows):
        gate = lb + (1.0 - lb) * jax.nn.sigmoid(f_ref[0, rows, :])
        hq = hq_ref[0, rows, :]
        return hq * jax.nn.sigmoid(hq), 1.0 - gate, hi_ref[0, rows, :], jnp.log(gate)

    def log_decay(lf, tri):
        p0, p1, p2 = _split3(lf)
        return (jnp.dot(tri, p0, preferred_element_type=F32) + jnp.dot(tri, p1, preferred_element_type=F32)
                + jnp.dot(tri, p2, preferred_element_type=F32))

    def state_step(s_ref, q, k, v, b, b_last):
        s_t = s_ref[...]
        inter = lax.dot_general((q * jnp.exp(b)).astype(BF16), s_t.astype(BF16), (((1,), (1,)), ((), ())),
                                preferred_element_type=F32)
        kd = (k * jnp.exp(b_last - b)).astype(BF16)
        upd = lax.dot_general(v.astype(BF16), kd, (((0,), (0,)), ((), ())), preferred_element_type=F32)
        s_ref[...] = jnp.exp(b_last) * s_t + upd
        return inter

    def fast_step(f_ref, lb, s_ref, acc_ref, c, masks, reverse):
        c_sz = HG_CHUNK
        keep, tri = masks
        rows = pl.ds(pl.multiple_of(c * c_sz, c_sz), c_sz)
        q, k, v, lf = chunk_inputs(f_ref, lb, rows)
        b = log_decay(lf, tri)
        r = b[c_sz // 2:c_sz // 2 + 1, :]
        b_last = b[0:1, :] if reverse else b[c_sz - 1:c_sz, :]
        b_first = b[c_sz - 1:c_sz, :] if reverse else b[0:1, :]
        dev_ref[...] = jnp.maximum(dev_ref[...], jnp.maximum(jnp.abs(b_first - r), jnp.abs(b_last - r)))
        qe = (q * jnp.exp(b - r)).astype(BF16)
        ke = (k * jnp.exp(r - b)).astype(BF16)
        a = lax.dot_general(qe, ke, (((1,), (1,)), ((), ())), preferred_element_type=F32)
        a = jnp.where(keep, a, 0.0).astype(BF16)
        intra = jnp.dot(a, v.astype(BF16), preferred_element_type=F32)
        acc_ref[rows, :] = intra + state_step(s_ref, q, k, v, b, b_last)

    def direct_step(f_ref, lb, s_ref, acc_ref, c, masks, reverse):
        c_sz = HG_DIRECT_CHUNK
        _, tri = masks
        row1 = lax.broadcasted_iota(jnp.int32, (c_sz, 1), 0)
        rows = pl.ds(pl.multiple_of(c * c_sz, c_sz), c_sz)
        q, k, v, lf = chunk_inputs(f_ref, lb, rows)
        b = log_decay(lf, tri)
        b_last = b[0:1, :] if reverse else b[c_sz - 1:c_sz, :]
        o = state_step(s_ref, q, k, v, b, b_last)
        for s in range(c_sz):
            e = jnp.exp(jnp.minimum(b - b[s:s + 1, :], 0.0))
            a = jnp.sum(q * k[s:s + 1, :] * e, axis=-1, keepdims=True)
            a = jnp.where((row1 <= s) if reverse else (row1 >= s), a, 0.0)
            o = o + a * v[s:s + 1, :]
        acc_ref[rows, :] = o

    def scan_both(step, c_sz):
        nc = t // c_sz
        masks_f = _tri_masks(c_sz, False)
        masks_b = _tri_masks(c_sz, True)
        sf_ref[...] = jnp.zeros_like(sf_ref)
        sb_ref[...] = jnp.zeros_like(sb_ref)

        def body(ci, carry):
            step(ff_ref, lbf_ref[...], sf_ref, accf_ref, ci, masks_f, False)
            step(fb_ref, lbb_ref[...], sb_ref, accb_ref, nc - 1 - ci, masks_b, True)
            return carry

        lax.fori_loop(0, nc, body, 0)

    dev_ref[...] = jnp.zeros_like(dev_ref)
    scan_both(fast_step, HG_CHUNK)

    @pl.when(jnp.logical_not(jnp.max(dev_ref[...]) <= HG_MAX_LOG_RANGE))
    def _():
        scan_both(direct_step, HG_DIRECT_CHUNK)

    def finish(c, carry):
        rows = pl.ds(pl.multiple_of(c * 256, 256), 256)
        hg = hg_ref[0, rows, :]
        y = _rms_rows(accf_ref[rows, :] + accb_ref[rows, :], on_ref[...]) * (hg * jax.nn.sigmoid(hg))
        o_ref[0, rows, :] = y.astype(o_ref.dtype)
        return carry

    lax.fori_loop(0, t // 256, finish, 0)


def _hgrn2(proj, lb_f, lb_b, out_norm):
    b, t, _ = proj.shape

    def col(base):
        return pl.BlockSpec((1, t, LANES), lambda bi, h: (bi, 0, base + h))

    return pl.pallas_call(
        _hgrn2_kernel,
        grid=(b, HG_H),
        in_specs=[col(E_HQ), col(E_FF), col(E_FB), col(E_HI), col(E_HG),
                  pl.BlockSpec((1, LANES), lambda bi, h: (0, h)),
                  pl.BlockSpec((1, LANES), lambda bi, h: (0, h)),
                  pl.BlockSpec((1, LANES), lambda bi, h: (0, 0))],
        out_specs=pl.BlockSpec((1, t, LANES), lambda bi, h: (bi, 0, h)),
        out_shape=jax.ShapeDtypeStruct((b, t, HG_H * HG_DV), BF16),
        scratch_shapes=[pltpu.VMEM((t, HG_DV), F32), pltpu.VMEM((t, HG_DV), F32),
                        pltpu.VMEM((HG_DV, HG_DK), F32), pltpu.VMEM((HG_DV, HG_DK), F32), pltpu.VMEM((1, HG_DK), F32)],
        compiler_params=_cparams(("parallel", "parallel"), 10 * _nbytes((t, LANES), F32), 4 * _nbytes((t, LANES), F32)),
        name="hgrn2",
    )(proj, proj, proj, proj, proj, lb_f, lb_b, out_norm.reshape(1, HG_DV))


def _out_proj_kernel(a_ref, b_ref, wa_ref, wb_ref, x_ref, o_ref):
    acc = jnp.dot(a_ref[...], wa_ref[...], preferred_element_type=F32)
    acc = acc + jnp.dot(b_ref[...], wb_ref[...], preferred_element_type=F32)
    o_ref[...] = x_ref[...] + acc


def _out_proj(a, b, w, x, tm=1024, tn=512):
    n, ka = a.shape
    d = w.shape[1]
    tm = min(tm, n)
    return pl.pallas_call(
        _out_proj_kernel,
        grid=(n // tm, d // tn),
        in_specs=[pl.BlockSpec((tm, ka), lambda i, j: (i, 0)),
                  pl.BlockSpec((tm, ka), lambda i, j: (i, 0)),
                  pl.BlockSpec((ka, tn), lambda i, j: (0, j)),
                  pl.BlockSpec((ka, tn), lambda i, j: (1, j)),
                  pl.BlockSpec((tm, tn), lambda i, j: (i, j))],
        out_specs=pl.BlockSpec((tm, tn), lambda i, j: (i, j)),
        out_shape=jax.ShapeDtypeStruct((n, d), F32),
        compiler_params=_cparams(("parallel", "arbitrary"), 4 * _nbytes((tm, ka), BF16), 4 * _nbytes((ka, tn), BF16),
                                 5 * _nbytes((tm, tn), F32)),
        name="out_proj",
    )(a, b, w, w, x)


def _ffn_kernel(x_ref, g_ref, wg_ref, wu_ref, wd_ref, o_ref, xn_ref):
    f = pl.program_id(1)

    @pl.when(f == 0)
    def _():
        _norm_into(xn_ref, x_ref, g_ref)
        o_ref[...] = x_ref[...]

    xn = xn_ref[...]
    gt = jnp.dot(xn, wg_ref[...], preferred_element_type=F32)
    up = jnp.dot(xn, wu_ref[...], preferred_element_type=F32)
    act = (gt * jax.nn.sigmoid(gt) * up).astype(BF16)
    o_ref[...] += jnp.dot(act, wd_ref[...], preferred_element_type=F32)


def _ffn(x, g, wg, wu, wd, tm=512, tf=512):
    n, d = x.shape
    ff = wg.shape[1]
    tm = min(tm, n)
    return pl.pallas_call(
        _ffn_kernel,
        grid=(n // tm, ff // tf),
        in_specs=[pl.BlockSpec((tm, d), lambda i, f: (i, 0)),
                  pl.BlockSpec((1, d), lambda i, f: (0, 0)),
                  pl.BlockSpec((d, tf), lambda i, f: (0, f)),
                  pl.BlockSpec((d, tf), lambda i, f: (0, f)),
                  pl.BlockSpec((tf, d), lambda i, f: (f, 0))],
        out_specs=pl.BlockSpec((tm, d), lambda i, f: (i, 0)),
        out_shape=jax.ShapeDtypeStruct((n, d), F32),
        scratch_shapes=[pltpu.VMEM((tm, d), BF16)],
        compiler_params=_cparams(("parallel", "arbitrary"), 4 * _nbytes((tm, d), F32), _nbytes((tm, d), BF16),
                                 6 * _nbytes((d, tf), BF16), 4 * _nbytes((tm, tf), F32), _nbytes((tm, d), F32)),
        name="ffn",
    )(x, g.reshape(1, d), wg, wu, wd)


def _axial_prep_kernel(qd_ref, qs_ref, kd_ref, ks_ref, vd_ref, cq_ref, sq_ref, ck_ref, sk_ref,
                       q_ref, k_ref, v_ref, *, scale):
    def rope(x_ref, xs_ref, c_ref, s_ref, h, mul):
        sl = slice(h * LANES, (h + 1) * LANES)
        x = x_ref[:, sl]
        r = lax.rsqrt(jnp.mean(x * x, axis=-1, keepdims=True) + EPS)
        return ((x * c_ref[...] + xs_ref[:, sl] * s_ref[...]) * (r * mul)).astype(BF16)

    for h in range(AX_H):
        q_ref[:, h * LANES:(h + 1) * LANES] = rope(qd_ref, qs_ref, cq_ref, sq_ref, h, scale)
    for h in range(AX_KV):
        k_ref[:, h * LANES:(h + 1) * LANES] = rope(kd_ref, ks_ref, ck_ref, sk_ref, h, 1.0)
    v_ref[...] = vd_ref[...].astype(BF16)


def _axial_prep(proj, cq, sq, ck, sk, seq, tm=512):
    n = proj.shape[0]
    tm = min(tm, seq)
    nt = seq // tm
    qw, kw = AX_H * AX_HD, AX_KV * AX_HD
    tab = pl.BlockSpec((tm, LANES), lambda i: (i % nt, 0))
    return pl.pallas_call(
        functools.partial(_axial_prep_kernel, scale=float(AX_HD ** -0.5)),
        grid=(n // tm,),
        in_specs=[pl.BlockSpec((tm, qw), lambda i: (i, O_QD * LANES // qw)),
                  pl.BlockSpec((tm, qw), lambda i: (i, O_QDS * LANES // qw)),
                  pl.BlockSpec((tm, kw), lambda i: (i, O_KD * LANES // kw)),
                  pl.BlockSpec((tm, kw), lambda i: (i, O_KDS * LANES // kw)),
                  pl.BlockSpec((tm, kw), lambda i: (i, O_VD * LANES // kw)),
                  tab, tab, tab, tab],
        out_specs=[pl.BlockSpec((tm, qw), lambda i: (i, 0)),
                   pl.BlockSpec((tm, kw), lambda i: (i, 0)),
                   pl.BlockSpec((tm, kw), lambda i: (i, 0))],
        out_shape=[jax.ShapeDtypeStruct((n, qw), BF16), jax.ShapeDtypeStruct((n, kw), BF16),
                   jax.ShapeDtypeStruct((n, kw), BF16)],
        compiler_params=_cparams(("parallel",), 4 * _nbytes((tm, qw), F32), 6 * _nbytes((tm, kw), F32),
                                 8 * _nbytes((tm, LANES), F32), 2 * _nbytes((tm, qw + 2 * kw), BF16)),
        name="axial_prep",
    )(proj, proj, proj, proj, proj, cq, sq, ck, sk)


WIN_BLOCK = 128


def _window_kernel(sink_ref, slope_ref, q_ref, k_ref, v_ref, o_ref, *, scale):
    t = q_ref.shape[1]
    wb = WIN_BLOCK
    span = 3 * wb
    pair = pl.program_id(1)
    lane = lax.broadcasted_iota(jnp.int32, (wb, LANES), 1)
    low = lane < WIN_HD
    delta = lax.broadcasted_iota(jnp.int32, (wb, span), 0) - lax.broadcasted_iota(jnp.int32, (wb, span), 1)

    def body(qb, carry):
        start = jnp.clip((qb - 1) * wb, 0, t - span)
        start = pl.multiple_of(start, wb)
        kwin = k_ref[0, pl.ds(start, span), :].astype(BF16)
        vwin = v_ref[0, pl.ds(start, span), :].astype(BF16)
        qrows = pl.ds(pl.multiple_of(qb * wb, wb), wb)
        q2 = q_ref[0, qrows, :] * scale
        dist = jnp.abs(delta + (qb * wb - start))
        valid = dist <= WINDOW
        distf = dist.astype(F32)
        outs = []
        for j in range(2):
            head = 2 * pair + j
            sink = sink_ref[head]
            qh = jnp.where(low if j == 0 else jnp.logical_not(low), q2, 0.0).astype(BF16)
            s = lax.dot_general(qh, kwin, (((1,), (1,)), ((), ())), preferred_element_type=F32)
            s = jnp.where(valid, s - slope_ref[head] * distf, -jnp.inf)
            m = jnp.maximum(jnp.max(s, axis=-1, keepdims=True), sink)
            e = jnp.exp(s - m)
            den = jnp.sum(e, axis=-1, keepdims=True) + jnp.exp(sink - m)
            outs.append(jnp.dot(e.astype(BF16), vwin, preferred_element_type=F32) / den)
        o_ref[0, qrows, :] = jnp.where(low, outs[0], outs[1]).astype(o_ref.dtype)
        return carry

    lax.fori_loop(0, t // wb, body, 0)


def _window_attention(proj, sink, slopes):
    b, t, _ = proj.shape
    pairs = WIN_H // 2
    per_kv = pairs // WIN_KV
    smem = pl.BlockSpec(memory_space=pltpu.SMEM)
    return pl.pallas_call(
        functools.partial(_window_kernel, scale=float(WIN_HD ** -0.5)),
        grid=(b, pairs),
        in_specs=[smem, smem,
                  pl.BlockSpec((1, t, LANES), lambda bi, p: (bi, 0, O_QC + p)),
                  pl.BlockSpec((1, t, LANES), lambda bi, p: (bi, 0, O_KC + p // per_kv)),
                  pl.BlockSpec((1, t, LANES), lambda bi, p: (bi, 0, O_VC + p // per_kv))],
        out_specs=pl.BlockSpec((1, t, LANES), lambda bi, p: (bi, 0, p)),
        out_shape=jax.ShapeDtypeStruct((b, t, WIN_H * WIN_HD), BF16),
        compiler_params=_cparams(("parallel", "parallel"), 6 * _nbytes((t, LANES), F32), 2 * _nbytes((t, LANES), BF16)),
        name="window_attention",
    )(sink, slopes, proj, proj, proj)


MOE_TM = 512
R_E1, R_E2, R_W1, R_W2, R_RANK1, R_RANK2 = range(6)


def _router_kernel(x_ref, g_ref, r_ref, route_ref, cnt_ref, carry_ref):
    @pl.when(pl.program_id(0) == 0)
    def _():
        carry_ref[...] = jnp.zeros_like(carry_ref)

    tm = x_ref.shape[0]
    xn = _rms_rows(x_ref[...], g_ref[...])
    logits = [jnp.sum(xn * r_ref[e:e + 1, :], axis=-1, keepdims=True) for e in range(N_EXPERTS)]

    def top(ls):
        m = functools.reduce(jnp.maximum, ls)
        idx = jnp.full_like(m, N_EXPERTS).astype(jnp.int32)
        for e in reversed(range(N_EXPERTS)):
            idx = jnp.where(ls[e] == m, e, idx)
        return m, idx

    m1, i1 = top(logits)
    m2, i2 = top([jnp.where(i1 == e, -jnp.inf, logits[e]) for e in range(N_EXPERTS)])
    e2 = jnp.exp(m2 - m1)
    w1 = 1.0 / (1.0 + e2)
    w2 = e2 / (1.0 + e2)
    lane = lax.broadcasted_iota(jnp.int32, route_ref.shape, 1)
    sel1, sel2 = lane == i1, lane == i2
    onehot = jnp.where(jnp.logical_or(sel1, sel2), 1.0, 0.0)
    row = lax.broadcasted_iota(jnp.int32, (tm, tm), 0)
    col = lax.broadcasted_iota(jnp.int32, (tm, tm), 1)
    earlier = jnp.where(row > col, 1.0, 0.0).astype(BF16)
    before = carry_ref[...] + jnp.dot(earlier, onehot.astype(BF16), preferred_element_type=F32)
    rank1 = jnp.sum(jnp.where(sel1, before, 0.0), axis=-1, keepdims=True)
    rank2 = jnp.sum(jnp.where(sel2, before, 0.0), axis=-1, keepdims=True)
    carry_ref[...] = carry_ref[...] + jnp.sum(onehot, axis=0, keepdims=True)
    cnt_ref[...] = carry_ref[...]
    rec = jnp.zeros(route_ref.shape, F32)
    for pos, val in ((R_E1, i1.astype(F32)), (R_E2, i2.astype(F32)), (R_W1, w1), (R_W2, w2),
                     (R_RANK1, rank1), (R_RANK2, rank2)):
        rec = jnp.where(lane == pos, val, rec)
    route_ref[...] = rec


def _router(x, g, router_t, tm=256):
    n, d = x.shape
    tm = min(tm, n)
    return pl.pallas_call(
        _router_kernel,
        grid=(n // tm,),
        in_specs=[pl.BlockSpec((tm, d), lambda i: (i, 0)),
                  pl.BlockSpec((1, d), lambda i: (0, 0)),
                  pl.BlockSpec((N_EXPERTS, d), lambda i: (0, 0))],
        out_specs=[pl.BlockSpec((tm, LANES), lambda i: (i, 0)),
                   pl.BlockSpec((1, LANES), lambda i: (0, 0))],
        out_shape=[jax.ShapeDtypeStruct((n, LANES), F32), jax.ShapeDtypeStruct((1, LANES), F32)],
        scratch_shapes=[pltpu.VMEM((1, LANES), F32)],
        compiler_params=_cparams(("arbitrary",), 6 * _nbytes((tm, d), F32)),
        name="router",
    )(x, g.reshape(1, d), router_t)


def _row_copy(src_hbm, src_row, dst, dst_row, sem):
    return pltpu.make_async_copy(src_hbm.at[pl.ds(src_row, 1)], dst.at[pl.ds(dst_row, 1)], sem)


def _moe_scatter_kernel(fill_ref, p1_ref, p2_ref, x_ref, xs_hbm, zero_ref, sem, zsem, *, tile):
    i = pl.program_id(0)
    rows = p1_ref.shape[-1]

    @pl.when(i == 0)
    def _():
        zero_ref[...] = jnp.zeros_like(zero_ref)
        for k in range(fill_ref.shape[0]):
            @pl.when(fill_ref[k] >= 0)
            def _():
                start = pl.multiple_of(fill_ref[k], tile)
                fill = pltpu.make_async_copy(zero_ref, xs_hbm.at[pl.ds(start, tile)], zsem)
                fill.start()
                fill.wait()

    def issue(r, carry):
        _row_copy(x_ref, r, xs_hbm, p1_ref[0, 0, r], sem).start()
        _row_copy(x_ref, r, xs_hbm, p2_ref[0, 0, r], sem).start()
        return carry

    lax.fori_loop(0, rows, issue, 0)
    for _ in range(2):
        pltpu.make_async_copy(x_ref, xs_hbm.at[pl.ds(0, rows)], sem).wait()


def _moe_scatter(x, pos1, pos2, fill_rows, total_rows, tile, rows=256):
    n, d = x.shape
    rows = min(rows, n)
    smem_blk = pl.BlockSpec((1, 1, rows), lambda i, fill: (i, 0, 0), memory_space=pltpu.SMEM)
    return pl.pallas_call(
        functools.partial(_moe_scatter_kernel, tile=tile),
        grid_spec=pltpu.PrefetchScalarGridSpec(
            num_scalar_prefetch=1,
            grid=(n // rows,),
            in_specs=[smem_blk, smem_blk, pl.BlockSpec((rows, d), lambda i, fill: (i, 0))],
            out_specs=pl.BlockSpec(memory_space=pl.ANY),
            scratch_shapes=[pltpu.VMEM((tile, d), x.dtype), pltpu.SemaphoreType.DMA, pltpu.SemaphoreType.DMA]),
        out_shape=jax.ShapeDtypeStruct((total_rows, d), x.dtype),
        compiler_params=_cparams(("arbitrary",), _nbytes((tile, d), x.dtype), 2 * _nbytes((rows, d), x.dtype)),
        name="moe_scatter",
    )(fill_rows, pos1.reshape(n // rows, 1, rows), pos2.reshape(n // rows, 1, rows), x)


def _moe_experts_kernel(te_ref, nv_ref, xs_ref, g_ref, wg_ref, wu_ref, wd_ref, y_ref, xn_ref):
    valid = pl.program_id(0) < nv_ref[0]

    @pl.when(valid)
    def _():
        _norm_into(xn_ref, xs_ref, g_ref)
        xn = xn_ref[...]
        gt = jnp.dot(xn, wg_ref[0], preferred_element_type=F32)
        up = jnp.dot(xn, wu_ref[0], preferred_element_type=F32)
        act = (gt * jax.nn.sigmoid(gt) * up).astype(BF16)
        y_ref[...] = jnp.dot(act, wd_ref[0], preferred_element_type=F32)

    @pl.when(jnp.logical_not(valid))
    def _():
        y_ref[...] = jnp.zeros_like(y_ref)


def _moe_experts(xs, g, wg, wu, wd, tile_expert, n_valid, tile):
    rows, d = xs.shape
    ff = wg.shape[2]
    once = dict(pipeline_mode=pl.Buffered(1))

    def row_blk(i, te, nv):
        return (jnp.minimum(i, nv[0] - 1), 0)

    def w_blk(i, te, nv):
        return (te[i], 0, 0)

    return pl.pallas_call(
        _moe_experts_kernel,
        grid_spec=pltpu.PrefetchScalarGridSpec(
            num_scalar_prefetch=2,
            grid=(rows // tile,),
            in_specs=[pl.BlockSpec((tile, d), row_blk),
                      pl.BlockSpec((1, d), lambda i, te, nv: (0, 0)),
                      pl.BlockSpec((1, d, ff), w_blk, **once),
                      pl.BlockSpec((1, d, ff), w_blk, **once),
                      pl.BlockSpec((1, ff, d), w_blk, **once)],
            out_specs=pl.BlockSpec((tile, d), lambda i, te, nv: (i, 0)),
            scratch_shapes=[pltpu.VMEM((tile, d), BF16)]),
        out_shape=jax.ShapeDtypeStruct((rows, d), F32),
        compiler_params=_cparams(("arbitrary",), 4 * _nbytes((tile, d), F32), _nbytes((tile, d), BF16),
                                 3 * _nbytes((d, ff), BF16), 4 * _nbytes((tile, ff), F32), _nbytes((tile, d), F32)),
        name="moe_experts",
    )(tile_expert, n_valid, xs, g.reshape(1, d), wg, wu, wd)


def _moe_combine_kernel(p1_ref, p2_ref, x_ref, route_ref, g_ref, y_hbm, o_ref, ya_ref, yb_ref, sem, *, final):
    rows = x_ref.shape[0]

    def issue(r, carry):
        _row_copy(y_hbm, p1_ref[0, 0, r], ya_ref, r, sem).start()
        _row_copy(y_hbm, p2_ref[0, 0, r], yb_ref, r, sem).start()
        return carry

    lax.fori_loop(0, rows, issue, 0)
    for dst in (ya_ref, yb_ref):
        pltpu.make_async_copy(y_hbm.at[pl.ds(0, rows)], dst, sem).wait()
    lane = lax.broadcasted_iota(jnp.int32, route_ref.shape, 1)
    w1 = jnp.sum(jnp.where(lane == R_W1, route_ref[...], 0.0), axis=-1, keepdims=True)
    w2 = jnp.sum(jnp.where(lane == R_W2, route_ref[...], 0.0), axis=-1, keepdims=True)
    out = x_ref[...] + w1 * ya_ref[...] + w2 * yb_ref[...]
    o_ref[...] = _rms_rows(out, g_ref[...]) if final else out


def _moe_combine(x, route, pos1, pos2, y, gain, row0, nrows, final, rows=256):
    n, d = x.shape
    rows = min(rows, nrows)
    off = row0 // rows
    smem_blk = pl.BlockSpec((1, 1, rows), lambda i: (i + off, 0, 0), memory_space=pltpu.SMEM)
    return pl.pallas_call(
        functools.partial(_moe_combine_kernel, final=final),
        grid=(nrows // rows,),
        in_specs=[smem_blk, smem_blk,
                  pl.BlockSpec((rows, d), lambda i: (i + off, 0)),
                  pl.BlockSpec((rows, LANES), lambda i: (i + off, 0)),
                  pl.BlockSpec((1, d), lambda i: (0, 0)),
                  pl.BlockSpec(memory_space=pl.ANY)],
        out_specs=pl.BlockSpec((rows, d), lambda i: (i, 0)),
        out_shape=jax.ShapeDtypeStruct((nrows, d), F32),
        scratch_shapes=[pltpu.VMEM((rows, d), F32), pltpu.VMEM((rows, d), F32), pltpu.SemaphoreType.DMA],
        compiler_params=_cparams(("arbitrary",), 8 * _nbytes((rows, d), F32)),
        name="moe_combine",
    )(pos1.reshape(n // rows, 1, rows), pos2.reshape(n // rows, 1, rows), x, route, gain.reshape(1, d), y)


def _swap_halves(w, width):
    lead = w.shape[:-1]
    return jnp.flip(w.reshape(lead + (-1, 2, width // 2)), axis=-2).reshape(w.shape)


def _pad_cols(w, width):
    return jnp.pad(w, [(0, 0)] * (w.ndim - 1) + [(0, width - w.shape[-1])])


def _even_in_weight(w):
    kr = w[:, 2 * MLA_LORA:2 * MLA_LORA + MLA_ROPE]
    return jnp.concatenate([w[:, :2 * MLA_LORA], _pad_cols(kr, LANES), _pad_cols(_swap_halves(kr, MLA_ROPE), LANES),
                            w[:, 2 * MLA_LORA + MLA_ROPE:]], axis=1).astype(BF16)


def _mla_q_weight(w):
    w = w.reshape(MLA_LORA, MLA_H, MLA_NOPE + MLA_ROPE)
    rope = w[..., MLA_NOPE:]
    w = jnp.concatenate([w[..., :MLA_NOPE], _pad_cols(rope, LANES), _pad_cols(_swap_halves(rope, MLA_ROPE), LANES)], -1)
    return w.reshape(MLA_LORA, MLA_H * 3 * LANES).astype(BF16)


def _odd_in_weight(w):
    d = w.shape[0]
    o = 0
    parts = {}
    for name, width in (("qc", WIN_H * WIN_HD), ("kc", WIN_KV * WIN_HD), ("vc", WIN_KV * WIN_HD),
                        ("qd", AX_H * AX_HD), ("kd", AX_KV * AX_HD), ("vd", AX_KV * AX_HD)):
        parts[name] = w[:, o:o + width]
        o += width

    def dup(x):
        x = x.reshape(d, WIN_KV, WIN_HD)
        return jnp.concatenate([x, x], axis=-1).reshape(d, WIN_KV * LANES)

    half = AX_HD // 2
    out = jnp.concatenate([parts["qd"], _swap_halves(parts["qd"], half), parts["qc"], parts["kd"],
                           _swap_halves(parts["kd"], half), parts["vd"], dup(parts["kc"]), dup(parts["vc"])], axis=1)
    return _pad_cols(out, O_END * LANES).astype(BF16)


def _rope_tables(pos, dim):
    inv = ROPE_THETA ** (-jnp.arange(0, dim, 2, dtype=F32) / dim)
    ang = pos[:, None] * inv[None, :]
    cos, sin = jnp.cos(ang), jnp.sin(ang)
    return jnp.concatenate([cos, cos], -1), jnp.concatenate([-sin, sin], -1)


def _alibi_slopes(n):
    return jnp.asarray(2.0 ** (-8.0 * np.arange(1, n + 1) / n), dtype=F32)


def _mixer_even(x, bsz, seq, j, norm_g, w_in, q_norm, w_uq, kv_norm, w_ukv, lb_fwd, lb_bwd, out_norm, w_out):
    n = x.shape[0]
    proj = _norm_matmul(x, norm_g, _even_in_weight(w_in), tn=5 * LANES, out_dtype=F32)
    cos, sin = _rope_tables(jnp.arange(seq, dtype=F32), MLA_ROPE)
    q, k, v = _mla_up(proj, q_norm, kv_norm, _mla_q_weight(w_uq), w_ukv.astype(BF16),
                      _pad_cols(cos, LANES), _pad_cols(sin, LANES), seq)
    o_a = _attention(q.reshape(bsz, seq, -1), k.reshape(bsz, seq, -1), v.reshape(bsz, seq, -1), MLA_H, MLA_H)

    def lower_bound(tab):
        return jnp.cumsum(jax.nn.softmax(tab.astype(F32), axis=0), axis=0)[j].reshape(1, HG_H * HG_DK)

    o_b = _hgrn2(proj.reshape(bsz, seq, -1), lower_bound(lb_fwd), lower_bound(lb_bwd), out_norm)
    return _out_proj(o_a.reshape(n, -1), o_b.reshape(n, -1), w_out.astype(BF16), x)


def _mixer_odd(x, bsz, seq, norm_g, w_in, sink, q_norm, k_norm, w_out):
    n = x.shape[0]
    proj = _norm_matmul(x, norm_g, _odd_in_weight(w_in), tn=4 * LANES, out_dtype=F32)
    o_c = _window_attention(proj.reshape(bsz, seq, -1), sink.astype(F32), _alibi_slopes(WIN_H))
    pos = jnp.arange(seq)
    half = AX_HD // 2
    c_row, s_row = _rope_tables((pos // GRID_W).astype(F32), half)
    c_col, s_col = _rope_tables((pos % GRID_W).astype(F32), half)
    cos = jnp.concatenate([c_row, c_col], -1)
    sin = jnp.concatenate([s_row, s_col], -1)

    def tables(g):
        g = g.astype(F32)
        return cos * g[None, :], sin * _swap_halves(g, half)[None, :]

    cq, sq = tables(q_norm)
    ck, sk = tables(k_norm)
    q, k, v = _axial_prep(proj, cq, sq, ck, sk, seq)
    o_d = _attention(q.reshape(bsz, seq, -1), k.reshape(bsz, seq, -1), v.reshape(bsz, seq, -1), AX_H, AX_KV)
    return _out_proj(o_c.reshape(n, -1), o_d.reshape(n, -1), w_out.astype(BF16), x)


def _moe(x, norm_g, router, w_gate, w_up, w_down, out_gain, out_splits):
    n = x.shape[0]
    tile = MOE_TM
    route, counts = _router(x, norm_g, router.astype(F32).T)
    counts = counts[0, :N_EXPERTS].astype(jnp.int32)
    padded = (counts + tile - 1) // tile * tile
    ends = jnp.cumsum(padded)
    starts = ends - padded
    e1, e2 = route[:, R_E1].astype(jnp.int32), route[:, R_E2].astype(jnp.int32)
    pos1 = starts[e1] + route[:, R_RANK1].astype(jnp.int32)
    pos2 = starts[e2] + route[:, R_RANK2].astype(jnp.int32)
    n_tiles = 2 * n // tile + N_EXPERTS
    n_valid = ends[-1] // tile
    tile_row = jnp.minimum(jnp.arange(n_tiles), n_valid - 1) * tile
    tile_expert = jnp.sum(tile_row[:, None] >= ends[None, :], axis=1).astype(jnp.int32)
    group_tail = jnp.where(padded > 0, ends - tile, -1)
    spare = (n_valid + jnp.arange(N_EXPERTS)) * tile
    fill_rows = jnp.concatenate([group_tail, jnp.where(spare < n_tiles * tile, spare, -1)]).astype(jnp.int32)
    xs = _moe_scatter(x, pos1, pos2, fill_rows, n_tiles * tile, tile)
    y = _moe_experts(xs, norm_g, w_gate.astype(BF16), w_up.astype(BF16), w_down.astype(BF16),
                     tile_expert, n_valid.reshape(1).astype(jnp.int32), tile)
    final = out_gain is not None
    gain = out_gain if final else jnp.ones((x.shape[1],), F32)
    return [_moe_combine(x, route, pos1, pos2, y, gain, row0, nrows, final) for row0, nrows in out_splits]


def _trunk(x, bsz, seq, norm_mix_e, w_in_e, mla_q_norm, mla_w_uq, mla_kv_norm, mla_w_ukv, hg_lb_fwd, hg_lb_bwd,
           hg_out_norm, w_out_e, norm_ffn_e, ffn_w_gate, ffn_w_up, ffn_w_down, norm_mix_o, w_in_o, win_sink,
           ax_q_norm, ax_k_norm, w_out_o, norm_ffn_o, moe_router, moe_w_gate, moe_w_up, moe_w_down, final_norm,
           out_splits):
    assert DEPTH % 2 == 0
    for l in range(DEPTH):
        j = l // 2
        if l % 2 == 0:
            x = _mixer_even(x, bsz, seq, j, norm_mix_e[j], w_in_e[j], mla_q_norm[j], mla_w_uq[j], mla_kv_norm[j],
                            mla_w_ukv[j], hg_lb_fwd, hg_lb_bwd, hg_out_norm[j], w_out_e[j])
            x = _ffn(x, norm_ffn_e[j], ffn_w_gate[j].astype(BF16), ffn_w_up[j].astype(BF16),
                     ffn_w_down[j].astype(BF16))
        else:
            x = _mixer_odd(x, bsz, seq, norm_mix_o[j], w_in_o[j], win_sink[j], ax_q_norm[j], ax_k_norm[j], w_out_o[j])
            last = l == DEPTH - 1
            outs = _moe(x, norm_ffn_o[j], moe_router[j], moe_w_gate[j], moe_w_up[j], moe_w_down[j],
                        final_norm if last else None, out_splits if last else [(0, x.shape[0])])
            if last:
                return outs
            x = outs[0]


def kernel(x_prompt, x_sample, norm_mix_e, w_in_e, mla_q_norm, mla_w_uq, mla_kv_norm, mla_w_ukv, hg_lb_fwd, hg_lb_bwd, hg_out_norm, w_out_e, norm_ffn_e, ffn_w_gate, ffn_w_up, ffn_w_down, norm_mix_o, w_in_o, win_sink, ax_q_norm, ax_k_norm, w_out_o, norm_ffn_o, moe_router, moe_w_gate, moe_w_up, moe_w_down, final_norm):
    bp, seq, d = x_prompt.shape
    bs = x_sample.shape[0]
    assert x_sample.shape[1:] == (seq, d)
    x = jnp.concatenate([x_prompt.reshape(bp * seq, d), x_sample.reshape(bs * seq, d)], axis=0)
    y_prompt, y_sample = _trunk(
        x, bp + bs, seq, norm_mix_e, w_in_e, mla_q_norm, mla_w_uq, mla_kv_norm, mla_w_ukv, hg_lb_fwd, hg_lb_bwd,
        hg_out_norm, w_out_e, norm_ffn_e, ffn_w_gate, ffn_w_up, ffn_w_down, norm_mix_o, w_in_o, win_sink, ax_q_norm,
        ax_k_norm, w_out_o, norm_ffn_o, moe_router, moe_w_gate, moe_w_up, moe_w_down, final_norm,
        [(0, bp * seq), (bp * seq, bs * seq)])
    return (y_prompt.reshape(bp, seq, d), y_sample.reshape(bs, seq, d))
```

```python
import functools

import jax
import jax.numpy as jnp
import numpy as np
from jax import lax
from jax.experimental import pallas as pl
from jax.experimental.pallas import tpu as pltpu

D_MODEL = 2048
DEPTH = 2
GRID_W = 64
EPS = 1e-6
ROPE_THETA = 10000.0

MLA_H = 8
MLA_NOPE = 128
MLA_ROPE = 64
MLA_V = 128
MLA_LORA = D_MODEL // 4

HG_H = 8
HG_DK = 128
HG_DV = 128

WIN_H = 16
WIN_KV = 2
WIN_HD = 64
WINDOW = 128

AX_H = 8
AX_KV = 2
AX_HD = 128

FF_DENSE = 5632
N_EXPERTS = 8
FF_EXPERT = 1408

LANES = 128
VMEM_CAP = 60000 * 1024
BF16 = jnp.bfloat16
F32 = jnp.float32

E_CQ, E_CKV, E_KRA, E_KRB, E_HQ, E_FF, E_FB, E_HI, E_HG, E_END = 0, 4, 8, 9, 10, 18, 26, 34, 42, 50
O_QD, O_QDS, O_QC, O_KD, O_KDS, O_VD, O_KC, O_VC, O_END = 0, 8, 16, 24, 26, 28, 30, 32, 36


def _cparams(sem, *block_bytes):
    need = int(sum(block_bytes)) + (6 << 20)
    return pltpu.CompilerParams(dimension_semantics=sem, vmem_limit_bytes=min(max(need, 16 << 20), VMEM_CAP))


def _nbytes(shape, dtype):
    return int(np.prod(shape)) * jnp.dtype(dtype).itemsize


def _rms_rows(x, g):
    return x * lax.rsqrt(jnp.mean(x * x, axis=-1, keepdims=True) + EPS) * g


def _norm_into(dst_ref, x_ref, g_ref, chunk=256):
    rows = x_ref.shape[0]
    chunk = min(chunk, rows)

    def body(c, carry):
        r = pl.ds(pl.multiple_of(c * chunk, chunk), chunk)
        dst_ref[r, :] = _rms_rows(x_ref[r, :].astype(F32), g_ref[...]).astype(dst_ref.dtype)
        return carry

    lax.fori_loop(0, rows // chunk, body, 0)


def _norm_matmul_kernel(x_ref, g_ref, w_ref, o_ref, xn_ref):
    @pl.when(pl.program_id(1) == 0)
    def _():
        _norm_into(xn_ref, x_ref, g_ref)

    o_ref[...] = jnp.dot(xn_ref[...], w_ref[...], preferred_element_type=F32).astype(o_ref.dtype)


def _norm_matmul(x, g, w, tn, out_dtype, tm=1024):
    n, k = x.shape
    nout = w.shape[1]
    tm = min(tm, n)
    return pl.pallas_call(
        _norm_matmul_kernel,
        grid=(n // tm, nout // tn),
        in_specs=[pl.BlockSpec((tm, k), lambda i, j: (i, 0)),
                  pl.BlockSpec((1, k), lambda i, j: (0, 0)),
                  pl.BlockSpec((k, tn), lambda i, j: (0, j))],
        out_specs=pl.BlockSpec((tm, tn), lambda i, j: (i, j)),
        out_shape=jax.ShapeDtypeStruct((n, nout), out_dtype),
        scratch_shapes=[pltpu.VMEM((tm, k), BF16)],
        compiler_params=_cparams(("parallel", "arbitrary"), 2 * _nbytes((tm, k), x.dtype), _nbytes((tm, k), BF16),
                                 2 * _nbytes((k, tn), BF16), 3 * _nbytes((tm, tn), F32)),
        name="norm_matmul",
    )(x, g.reshape(1, k), w)


def _mla_up_kernel(cq_ref, ckv_ref, kra_ref, krb_ref, qn_ref, kvn_ref, wq_ref, wkv_ref, cos_ref, sin_ref,
                   q_ref, k_ref, v_ref, cqn_ref, ckvn_ref, kr_ref, *, scale):
    @pl.when(pl.program_id(1) == 0)
    def _():
        _norm_into(cqn_ref, cq_ref, qn_ref)
        _norm_into(ckvn_ref, ckv_ref, kvn_ref)
        kr_ref[...] = (kra_ref[...] * cos_ref[...] + krb_ref[...] * sin_ref[...]).astype(BF16)

    q = jnp.dot(cqn_ref[...], wq_ref[...], preferred_element_type=F32)
    q_rope = q[:, LANES:2 * LANES] * cos_ref[...] + q[:, 2 * LANES:] * sin_ref[...]
    q_ref[:, :LANES] = (q[:, :LANES] * scale).astype(BF16)
    q_ref[:, LANES:] = (q_rope * scale).astype(BF16)
    kv = jnp.dot(ckvn_ref[...], wkv_ref[...], preferred_element_type=F32)
    k_ref[:, :LANES] = kv[:, :LANES].astype(BF16)
    k_ref[:, LANES:] = kr_ref[...]
    v_ref[...] = kv[:, LANES:].astype(BF16)


def _mla_up(proj, qn, kvn, wq, wkv, cos_t, sin_t, seq, tm=512):
    n = proj.shape[0]
    tm = min(tm, seq)
    nt = seq // tm
    lora = MLA_LORA
    scale = float((MLA_NOPE + MLA_ROPE) ** -0.5) * LOG2E
    return pl.pallas_call(
        functools.partial(_mla_up_kernel, scale=scale),
        grid=(n // tm, MLA_H),
        in_specs=[pl.BlockSpec((tm, lora), lambda i, h: (i, 0)),
                  pl.BlockSpec((tm, lora), lambda i, h: (i, 1)),
                  pl.BlockSpec((tm, LANES), lambda i, h: (i, E_KRA)),
                  pl.BlockSpec((tm, LANES), lambda i, h: (i, E_KRB)),
                  pl.BlockSpec((1, lora), lambda i, h: (0, 0)),
                  pl.BlockSpec((1, lora), lambda i, h: (0, 0)),
                  pl.BlockSpec((lora, 3 * LANES), lambda i, h: (0, h)),
                  pl.BlockSpec((lora, 2 * LANES), lambda i, h: (0, h)),
                  pl.BlockSpec((tm, LANES), lambda i, h: (i % nt, 0)),
                  pl.BlockSpec((tm, LANES), lambda i, h: (i % nt, 0))],
        out_specs=[pl.BlockSpec((tm, 2 * LANES), lambda i, h: (i, h)),
                   pl.BlockSpec((tm, 2 * LANES), lambda i, h: (i, h)),
                   pl.BlockSpec((tm, LANES), lambda i, h: (i, h))],
        out_shape=[jax.ShapeDtypeStruct((n, MLA_H * 2 * LANES), BF16),
                   jax.ShapeDtypeStruct((n, MLA_H * 2 * LANES), BF16),
                   jax.ShapeDtypeStruct((n, MLA_H * LANES), BF16)],
        scratch_shapes=[pltpu.VMEM((tm, lora), BF16), pltpu.VMEM((tm, lora), BF16), pltpu.VMEM((tm, LANES), BF16)],
        compiler_params=_cparams(("parallel", "arbitrary"), 4 * _nbytes((tm, lora), F32), 12 * _nbytes((tm, LANES), F32),
                                 4 * _nbytes((lora, 5 * LANES), BF16), 8 * _nbytes((tm, 3 * LANES), F32)),
        name="mla_up",
    )(proj, proj, proj, proj, qn.reshape(1, lora), kvn.reshape(1, lora), wq, wkv, cos_t, sin_t)


ATT_TK = 1024
ATT_SUB = 1024
LOG2E = 1.4426950408889634


def _attention_kernel(q_ref, k_ref, v_ref, o_ref):
    tq = q_ref.shape[1]
    t = k_ref.shape[1]
    for r0 in range(0, tq, ATT_SUB):
        q = q_ref[0, r0:r0 + ATT_SUB, :]
        m = l = acc = None
        for c0 in range(0, t, ATT_TK):
            s = lax.dot_general(q, k_ref[0, c0:c0 + ATT_TK, :], (((1,), (1,)), ((), ())),
                                preferred_element_type=F32)
            m_c = jnp.max(s, axis=-1, keepdims=True)
            m_new = m_c if m is None else jnp.maximum(m, m_c)
            p = jnp.exp2(s - m_new)
            pv = jnp.dot(p.astype(BF16), v_ref[0, c0:c0 + ATT_TK, :], preferred_element_type=F32)
            l_c = jnp.sum(p, axis=-1, keepdims=True)
            if m is None:
                l, acc = l_c, pv
            else:
                alpha = jnp.exp2(m - m_new)
                l, acc = alpha * l + l_c, alpha * acc + pv
            m = m_new
        o_ref[0, r0:r0 + ATT_SUB, :] = (acc / l).astype(o_ref.dtype)


def _attention(q, k, v, heads, kv_heads, tq=1024):
    b, t, _ = q.shape
    dq = q.shape[2] // heads
    dv = v.shape[2] // kv_heads
    g = heads // kv_heads
    tq = min(tq, t)
    return pl.pallas_call(
        _attention_kernel,
        grid=(b, heads, t // tq),
        in_specs=[pl.BlockSpec((1, tq, dq), lambda bi, h, qi: (bi, qi, h)),
                  pl.BlockSpec((1, t, dq), lambda bi, h, qi: (bi, 0, h // g)),
                  pl.BlockSpec((1, t, dv), lambda bi, h, qi: (bi, 0, h // g))],
        out_specs=pl.BlockSpec((1, tq, dv), lambda bi, h, qi: (bi, qi, h)),
        out_shape=jax.ShapeDtypeStruct((b, t, heads * dv), BF16),
        compiler_params=_cparams(("parallel", "parallel", "arbitrary"), 2 * _nbytes((tq, dq), BF16),
                                 2 * _nbytes((t, dq + dv), BF16), 2 * _nbytes((tq, dv), BF16),
                                 6 * _nbytes((ATT_SUB, ATT_TK), F32)),
        name="attention",
    )(q, k, v)


HG_CHUNK = 128
HG_DIRECT_CHUNK = 32
HG_MAX_LOG_RANGE = 80.0


def _split3(x):
    a = x.astype(BF16)
    r = x - a.astype(F32)
    b = r.astype(BF16)
    c = (r - b.astype(F32)).astype(BF16)
    return a, b, c


def _tri_masks(c_sz, reverse):
    row = lax.broadcasted_iota(jnp.int32, (c_sz, c_sz), 0)
    col = lax.broadcasted_iota(jnp.int32, (c_sz, c_sz), 1)
    keep = (row <= col) if reverse else (row >= col)
    return keep, jnp.where(keep, 1.0, 0.0).astype(BF16)


def _hgrn2_kernel(hq_ref, ff_ref, fb_ref, hi_ref, hg_ref, lbf_ref, lbb_ref, on_ref, o_ref,
                  accf_ref, accb_ref, sf_ref, sb_ref, dev_ref):
    t = hq_ref.shape[1]

    def chunk_inputs(f_ref, lb, rows):
        gate = lb + (1.0 - lb) * jax.nn.sigmoid(f_ref[0, rows, :])
        hq = hq_ref[0, rows, :]
        return hq * jax.nn.sigmoid(hq), 1.0 - gate, hi_ref[0, rows, :], jnp.log(gate)

    def log_decay(lf, tri):
        p0, p1, p2 = _split3(lf)
        return (jnp.dot(tri, p0, preferred_element_type=F32) + jnp.dot(tri, p1, preferred_element_type=F32)
                + jnp.dot(tri, p2, preferred_element_type=F32))

    def state_step(s_ref, q, k, v, b, b_last):
        s_t = s_ref[...]
        inter = lax.dot_general((q * jnp.exp(b)).astype(BF16), s_t.astype(BF16), (((1,), (1,)), ((), ())),
                                preferred_element_type=F32)
        kd = (k * jnp.exp(b_last - b)).astype(BF16)
        upd = lax.dot_general(v.astype(BF16), kd, (((0,), (0,)), ((), ())), preferred_element_type=F32)
        s_ref[...] = jnp.exp(b_last) * s_t + upd
        return inter

    def fast_step(f_ref, lb, s_ref, acc_ref, c, masks, reverse):
        c_sz = HG_CHUNK
        keep, tri = masks
        rows = pl.ds(pl.multiple_of(c * c_sz, c_sz), c_sz)
        q, k, v, lf = chunk_inputs(f_ref, lb, rows)
        b = log_decay(lf, tri)
        r = b[c_sz // 2:c_sz // 2 + 1, :]
        b_last = b[0:1, :] if reverse else b[c_sz - 1:c_sz, :]
        b_first = b[c_sz - 1:c_sz, :] if reverse else b[0:1, :]
        dev_ref[...] = jnp.maximum(dev_ref[...], jnp.maximum(jnp.abs(b_first - r), jnp.abs(b_last - r)))
        qe = (q * jnp.exp(b - r)).astype(BF16)
        ke = (k * jnp.exp(r - b)).astype(BF16)
        a = lax.dot_general(qe, ke, (((1,), (1,)), ((), ())), preferred_element_type=F32)
        a = jnp.where(keep, a, 0.0).astype(BF16)
        intra = jnp.dot(a, v.astype(BF16), preferred_element_type=F32)
        acc_ref[rows, :] = intra + state_step(s_ref, q, k, v, b, b_last)

    def direct_step(f_ref, lb, s_ref, acc_ref, c, masks, reverse):
        c_sz = HG_DIRECT_CHUNK
        _, tri = masks
        row1 = lax.broadcasted_iota(jnp.int32, (c_sz, 1), 0)
        rows = pl.ds(pl.multiple_of(c * c_sz, c_sz), c_sz)
        q, k, v, lf = chunk_inputs(f_ref, lb, rows)
        b = log_decay(lf, tri)
        b_last = b[0:1, :] if reverse else b[c_sz - 1:c_sz, :]
        o = state_step(s_ref, q, k, v, b, b_last)
        for s in range(c_sz):
            e = jnp.exp(jnp.minimum(b - b[s:s + 1, :], 0.0))
            a = jnp.sum(q * k[s:s + 1, :] * e, axis=-1, keepdims=True)
            a = jnp.where((row1 <= s) if reverse else (row1 >= s), a, 0.0)
            o = o + a * v[s:s + 1, :]
        acc_ref[rows, :] = o

    def scan_both(step, c_sz):
        nc = t // c_sz
        masks_f = _tri_masks(c_sz, False)
        masks_b = _tri_masks(c_sz, True)
        sf_ref[...] = jnp.zeros_like(sf_ref)
        sb_ref[...] = jnp.zeros_like(sb_ref)

        def body(ci, carry):
            step(ff_ref, lbf_ref[...], sf_ref, accf_ref, ci, masks_f, False)
            step(fb_ref, lbb_ref[...], sb_ref, accb_ref, nc - 1 - ci, masks_b, True)
            return carry

        lax.fori_loop(0, nc, body, 0, unroll=4)

    dev_ref[...] = jnp.zeros_like(dev_ref)
    scan_both(fast_step, HG_CHUNK)

    @pl.when(jnp.logical_not(jnp.max(dev_ref[...]) <= HG_MAX_LOG_RANGE))
    def _():
        scan_both(direct_step, HG_DIRECT_CHUNK)

    def finish(c, carry):
        rows = pl.ds(pl.multiple_of(c * 256, 256), 256)
        hg = hg_ref[0, rows, :]
        y = _rms_rows(accf_ref[rows, :] + accb_ref[rows, :], on_ref[...]) * (hg * jax.nn.sigmoid(hg))
        o_ref[0, rows, :] = y.astype(o_ref.dtype)
        return carry

    lax.fori_loop(0, t // 256, finish, 0)


def _hgrn2(proj, lb_f, lb_b, out_norm):
    b, t, _ = proj.shape

    def col(base):
        return pl.BlockSpec((1, t, LANES), lambda bi, h: (bi, 0, base + h))

    return pl.pallas_call(
        _hgrn2_kernel,
        grid=(b, HG_H),
        in_specs=[col(E_HQ), col(E_FF), col(E_FB), col(E_HI), col(E_HG),
                  pl.BlockSpec((1, LANES), lambda bi, h: (0, h)),
                  pl.BlockSpec((1, LANES), lambda bi, h: (0, h)),
                  pl.BlockSpec((1, LANES), lambda bi, h: (0, 0))],
        out_specs=pl.BlockSpec((1, t, LANES), lambda bi, h: (bi, 0, h)),
        out_shape=jax.ShapeDtypeStruct((b, t, HG_H * HG_DV), BF16),
        scratch_shapes=[pltpu.VMEM((t, HG_DV), F32), pltpu.VMEM((t, HG_DV), F32),
                        pltpu.VMEM((HG_DV, HG_DK), F32), pltpu.VMEM((HG_DV, HG_DK), F32), pltpu.VMEM((1, HG_DK), F32)],
        compiler_params=_cparams(("parallel", "parallel"), 10 * _nbytes((t, LANES), F32), 4 * _nbytes((t, LANES), F32)),
        name="hgrn2",
    )(proj, proj, proj, proj, proj, lb_f, lb_b, out_norm.reshape(1, HG_DV))


def _out_proj_kernel(a_ref, b_ref, wa_ref, wb_ref, x_ref, o_ref):
    acc = jnp.dot(a_ref[...], wa_ref[...], preferred_element_type=F32)
    acc = acc + jnp.dot(b_ref[...], wb_ref[...], preferred_element_type=F32)
    o_ref[...] = x_ref[...] + acc


def _out_proj(a, b, w, x, tm=1024, tn=512):
    n, ka = a.shape
    d = w.shape[1]
    tm = min(tm, n)
    return pl.pallas_call(
        _out_proj_kernel,
        grid=(n // tm, d // tn),
        in_specs=[pl.BlockSpec((tm, ka), lambda i, j: (i, 0)),
                  pl.BlockSpec((tm, ka), lambda i, j: (i, 0)),
                  pl.BlockSpec((ka, tn), lambda i, j: (0, j)),
                  pl.BlockSpec((ka, tn), lambda i, j: (1, j)),
                  pl.BlockSpec((tm, tn), lambda i, j: (i, j))],
        out_specs=pl.BlockSpec((tm, tn), lambda i, j: (i, j)),
        out_shape=jax.ShapeDtypeStruct((n, d), F32),
        compiler_params=_cparams(("parallel", "arbitrary"), 4 * _nbytes((tm, ka), BF16), 4 * _nbytes((ka, tn), BF16),
                                 5 * _nbytes((tm, tn), F32)),
        name="out_proj",
    )(a, b, w, w, x)


def _ffn_kernel(x_ref, g_ref, wg_ref, wu_ref, wd_ref, o_ref, xn_ref):
    f = pl.program_id(1)

    @pl.when(f == 0)
    def _():
        _norm_into(xn_ref, x_ref, g_ref)
        o_ref[...] = x_ref[...]

    xn = xn_ref[...]
    gt = jnp.dot(xn, wg_ref[...], preferred_element_type=F32)
    up = jnp.dot(xn, wu_ref[...], preferred_element_type=F32)
    act = (gt * jax.nn.sigmoid(gt) * up).astype(BF16)
    o_ref[...] += jnp.dot(act, wd_ref[...], preferred_element_type=F32)


def _ffn(x, g, wg, wu, wd, tm=512, tf=512):
    n, d = x.shape
    ff = wg.shape[1]
    tm = min(tm, n)
    return pl.pallas_call(
        _ffn_kernel,
        grid=(n // tm, ff // tf),
        in_specs=[pl.BlockSpec((tm, d), lambda i, f: (i, 0)),
                  pl.BlockSpec((1, d), lambda i, f: (0, 0)),
                  pl.BlockSpec((d, tf), lambda i, f: (0, f)),
                  pl.BlockSpec((d, tf), lambda i, f: (0, f)),
                  pl.BlockSpec((tf, d), lambda i, f: (f, 0))],
        out_specs=pl.BlockSpec((tm, d), lambda i, f: (i, 0)),
        out_shape=jax.ShapeDtypeStruct((n, d), F32),
        scratch_shapes=[pltpu.VMEM((tm, d), BF16)],
        compiler_params=_cparams(("parallel", "arbitrary"), 4 * _nbytes((tm, d), F32), _nbytes((tm, d), BF16),
                                 6 * _nbytes((d, tf), BF16), 4 * _nbytes((tm, tf), F32), _nbytes((tm, d), F32)),
        name="ffn",
    )(x, g.reshape(1, d), wg, wu, wd)


def _axial_prep_kernel(qd_ref, qs_ref, kd_ref, ks_ref, vd_ref, cq_ref, sq_ref, ck_ref, sk_ref,
                       q_ref, k_ref, v_ref, *, scale):
    def rope(x_ref, xs_ref, c_ref, s_ref, h, mul):
        sl = slice(h * LANES, (h + 1) * LANES)
        x = x_ref[:, sl]
        r = lax.rsqrt(jnp.mean(x * x, axis=-1, keepdims=True) + EPS)
        return ((x * c_ref[...] + xs_ref[:, sl] * s_ref[...]) * (r * mul)).astype(BF16)

    for h in range(AX_H):
        q_ref[:, h * LANES:(h + 1) * LANES] = rope(qd_ref, qs_ref, cq_ref, sq_ref, h, scale)
    for h in range(AX_KV):
        k_ref[:, h * LANES:(h + 1) * LANES] = rope(kd_ref, ks_ref, ck_ref, sk_ref, h, 1.0)
    v_ref[...] = vd_ref[...].astype(BF16)


def _axial_prep(proj, cq, sq, ck, sk, seq, tm=512):
    n = proj.shape[0]
    tm = min(tm, seq)
    nt = seq // tm
    qw, kw = AX_H * AX_HD, AX_KV * AX_HD
    tab = pl.BlockSpec((tm, LANES), lambda i: (i % nt, 0))
    return pl.pallas_call(
        functools.partial(_axial_prep_kernel, scale=float(AX_HD ** -0.5) * LOG2E),
        grid=(n // tm,),
        in_specs=[pl.BlockSpec((tm, qw), lambda i: (i, O_QD * LANES // qw)),
                  pl.BlockSpec((tm, qw), lambda i: (i, O_QDS * LANES // qw)),
                  pl.BlockSpec((tm, kw), lambda i: (i, O_KD * LANES // kw)),
                  pl.BlockSpec((tm, kw), lambda i: (i, O_KDS * LANES // kw)),
                  pl.BlockSpec((tm, kw), lambda i: (i, O_VD * LANES // kw)),
                  tab, tab, tab, tab],
        out_specs=[pl.BlockSpec((tm, qw), lambda i: (i, 0)),
                   pl.BlockSpec((tm, kw), lambda i: (i, 0)),
                   pl.BlockSpec((tm, kw), lambda i: (i, 0))],
        out_shape=[jax.ShapeDtypeStruct((n, qw), BF16), jax.ShapeDtypeStruct((n, kw), BF16),
                   jax.ShapeDtypeStruct((n, kw), BF16)],
        compiler_params=_cparams(("parallel",), 4 * _nbytes((tm, qw), F32), 6 * _nbytes((tm, kw), F32),
                                 8 * _nbytes((tm, LANES), F32), 2 * _nbytes((tm, qw + 2 * kw), BF16)),
        name="axial_prep",
    )(proj, proj, proj, proj, proj, cq, sq, ck, sk)


WIN_BLOCK = 128


def _window_kernel(sink_ref, slope_ref, q_ref, k_ref, v_ref, o_ref, *, scale):
    t = q_ref.shape[1]
    wb = WIN_BLOCK
    span = 3 * wb
    pair = pl.program_id(1)
    low = lax.broadcasted_iota(jnp.int32, (wb, LANES), 1) < WIN_HD
    row = lax.broadcasted_iota(jnp.int32, (2 * wb, span), 0)
    delta = (row & (wb - 1)) - lax.broadcasted_iota(jnp.int32, (2 * wb, span), 1)
    top = lax.broadcasted_iota(jnp.int32, (2 * wb, 1), 0) < wb
    slope = jnp.where(top, slope_ref[2 * pair], slope_ref[2 * pair + 1]) * LOG2E
    sink = jnp.where(top, sink_ref[2 * pair], sink_ref[2 * pair + 1]) * LOG2E

    def body(qb, carry):
        start = pl.multiple_of(jnp.clip((qb - 1) * wb, 0, t - span), wb)
        kwin = k_ref[0, pl.ds(start, span), :].astype(BF16)
        vwin = v_ref[0, pl.ds(start, span), :].astype(BF16)
        qrows = pl.ds(pl.multiple_of(qb * wb, wb), wb)
        q2 = q_ref[0, qrows, :] * (scale * LOG2E)
        qs = jnp.concatenate([jnp.where(low, q2, 0.0), jnp.where(low, 0.0, q2)], axis=0).astype(BF16)
        dist = jnp.abs(delta + (qb * wb - start))
        s = lax.dot_general(qs, kwin, (((1,), (1,)), ((), ())), preferred_element_type=F32)
        s = jnp.where(dist <= WINDOW, s - slope * dist.astype(F32), -jnp.inf)
        m = jnp.maximum(jnp.max(s, axis=-1, keepdims=True), sink)
        e = jnp.exp2(s - m)
        den = jnp.sum(e, axis=-1, keepdims=True) + jnp.exp2(sink - m)
        o = jnp.dot(e.astype(BF16), vwin, preferred_element_type=F32) / den
        o_ref[0, qrows, :] = jnp.where(low, o[:wb], o[wb:]).astype(o_ref.dtype)
        return carry

    lax.fori_loop(0, t // wb, body, 0, unroll=4)


def _window_attention(proj, sink, slopes):
    b, t, _ = proj.shape
    pairs = WIN_H // 2
    per_kv = pairs // WIN_KV
    smem = pl.BlockSpec(memory_space=pltpu.SMEM)
    return pl.pallas_call(
        functools.partial(_window_kernel, scale=float(WIN_HD ** -0.5)),
        grid=(b, pairs),
        in_specs=[smem, smem,
                  pl.BlockSpec((1, t, LANES), lambda bi, p: (bi, 0, O_QC + p)),
                  pl.BlockSpec((1, t, LANES), lambda bi, p: (bi, 0, O_KC + p // per_kv)),
                  pl.BlockSpec((1, t, LANES), lambda bi, p: (bi, 0, O_VC + p // per_kv))],
        out_specs=pl.BlockSpec((1, t, LANES), lambda bi, p: (bi, 0, p)),
        out_shape=jax.ShapeDtypeStruct((b, t, WIN_H * WIN_HD), BF16),
        compiler_params=_cparams(("parallel", "parallel"), 6 * _nbytes((t, LANES), F32), 2 * _nbytes((t, LANES), BF16)),
        name="window_attention",
    )(sink, slopes, proj, proj, proj)


MOE_TM = 512
R_E1, R_E2, R_W1, R_W2, R_RANK1, R_RANK2 = range(6)


def _router_kernel(x_ref, g_ref, r_ref, route_ref, cnt_ref, carry_ref):
    @pl.when(pl.program_id(0) == 0)
    def _():
        carry_ref[...] = jnp.zeros_like(carry_ref)

    tm = x_ref.shape[0]
    xn = _rms_rows(x_ref[...], g_ref[...])
    logits = [jnp.sum(xn * r_ref[e:e + 1, :], axis=-1, keepdims=True) for e in range(N_EXPERTS)]

    def top(ls):
        m = functools.reduce(jnp.maximum, ls)
        idx = jnp.full_like(m, N_EXPERTS).astype(jnp.int32)
        for e in reversed(range(N_EXPERTS)):
            idx = jnp.where(ls[e] == m, e, idx)
        return m, idx

    m1, i1 = top(logits)
    m2, i2 = top([jnp.where(i1 == e, -jnp.inf, logits[e]) for e in range(N_EXPERTS)])
    e2 = jnp.exp(m2 - m1)
    w1 = 1.0 / (1.0 + e2)
    w2 = e2 / (1.0 + e2)
    lane = lax.broadcasted_iota(jnp.int32, route_ref.shape, 1)
    sel1, sel2 = lane == i1, lane == i2
    onehot = jnp.where(jnp.logical_or(sel1, sel2), 1.0, 0.0)
    row = lax.broadcasted_iota(jnp.int32, (tm, tm), 0)
    col = lax.broadcasted_iota(jnp.int32, (tm, tm), 1)
    earlier = jnp.where(row > col, 1.0, 0.0).astype(BF16)
    before = carry_ref[...] + jnp.dot(earlier, onehot.astype(BF16), preferred_element_type=F32)
    rank1 = jnp.sum(jnp.where(sel1, before, 0.0), axis=-1, keepdims=True)
    rank2 = jnp.sum(jnp.where(sel2, before, 0.0), axis=-1, keepdims=True)
    carry_ref[...] = carry_ref[...] + jnp.sum(onehot, axis=0, keepdims=True)
    cnt_ref[...] = carry_ref[...]
    rec = jnp.zeros(route_ref.shape, F32)
    for pos, val in ((R_E1, i1.astype(F32)), (R_E2, i2.astype(F32)), (R_W1, w1), (R_W2, w2),
                     (R_RANK1, rank1), (R_RANK2, rank2)):
        rec = jnp.where(lane == pos, val, rec)
    route_ref[...] = rec


def _router(x, g, router_t, tm=256):
    n, d = x.shape
    tm = min(tm, n)
    return pl.pallas_call(
        _router_kernel,
        grid=(n // tm,),
        in_specs=[pl.BlockSpec((tm, d), lambda i: (i, 0)),
                  pl.BlockSpec((1, d), lambda i: (0, 0)),
                  pl.BlockSpec((N_EXPERTS, d), lambda i: (0, 0))],
        out_specs=[pl.BlockSpec((tm, LANES), lambda i: (i, 0)),
                   pl.BlockSpec((1, LANES), lambda i: (0, 0))],
        out_shape=[jax.ShapeDtypeStruct((n, LANES), F32), jax.ShapeDtypeStruct((1, LANES), F32)],
        scratch_shapes=[pltpu.VMEM((1, LANES), F32)],
        compiler_params=_cparams(("arbitrary",), 6 * _nbytes((tm, d), F32)),
        name="router",
    )(x, g.reshape(1, d), router_t)


def _row_copy(src_hbm, src_row, dst, dst_row, sem):
    return pltpu.make_async_copy(src_hbm.at[pl.ds(src_row, 1)], dst.at[pl.ds(dst_row, 1)], sem)


def _moe_scatter_kernel(fill_ref, p1_ref, p2_ref, x_ref, xs_hbm, zero_ref, sem, zsem, *, tile):
    i = pl.program_id(0)
    rows = p1_ref.shape[-1]

    @pl.when(i == 0)
    def _():
        zero_ref[...] = jnp.zeros_like(zero_ref)
        for k in range(fill_ref.shape[0]):
            @pl.when(fill_ref[k] >= 0)
            def _():
                start = pl.multiple_of(fill_ref[k], tile)
                fill = pltpu.make_async_copy(zero_ref, xs_hbm.at[pl.ds(start, tile)], zsem)
                fill.start()
                fill.wait()

    def issue(r, carry):
        _row_copy(x_ref, r, xs_hbm, p1_ref[0, 0, r], sem).start()
        _row_copy(x_ref, r, xs_hbm, p2_ref[0, 0, r], sem).start()
        return carry

    lax.fori_loop(0, rows, issue, 0)
    for _ in range(2):
        pltpu.make_async_copy(x_ref, xs_hbm.at[pl.ds(0, rows)], sem).wait()


def _moe_scatter(x, pos1, pos2, fill_rows, total_rows, tile, rows=256):
    n, d = x.shape
    rows = min(rows, n)
    smem_blk = pl.BlockSpec((1, 1, rows), lambda i, fill: (i, 0, 0), memory_space=pltpu.SMEM)
    return pl.pallas_call(
        functools.partial(_moe_scatter_kernel, tile=tile),
        grid_spec=pltpu.PrefetchScalarGridSpec(
            num_scalar_prefetch=1,
            grid=(n // rows,),
            in_specs=[smem_blk, smem_blk, pl.BlockSpec((rows, d), lambda i, fill: (i, 0))],
            out_specs=pl.BlockSpec(memory_space=pl.ANY),
            scratch_shapes=[pltpu.VMEM((tile, d), x.dtype), pltpu.SemaphoreType.DMA, pltpu.SemaphoreType.DMA]),
        out_shape=jax.ShapeDtypeStruct((total_rows, d), x.dtype),
        compiler_params=_cparams(("arbitrary",), _nbytes((tile, d), x.dtype), 2 * _nbytes((rows, d), x.dtype)),
        name="moe_scatter",
    )(fill_rows, pos1.reshape(n // rows, 1, rows), pos2.reshape(n // rows, 1, rows), x)


def _moe_experts_kernel(te_ref, nv_ref, xs_ref, g_ref, wg_ref, wu_ref, wd_ref, y_ref, xn_ref):
    valid = pl.program_id(0) < nv_ref[0]

    @pl.when(valid)
    def _():
        _norm_into(xn_ref, xs_ref, g_ref)
        xn = xn_ref[...]
        gt = jnp.dot(xn, wg_ref[0], preferred_element_type=F32)
        up = jnp.dot(xn, wu_ref[0], preferred_element_type=F32)
        act = (gt * jax.nn.sigmoid(gt) * up).astype(BF16)
        y_ref[...] = jnp.dot(act, wd_ref[0], preferred_element_type=F32)

    @pl.when(jnp.logical_not(valid))
    def _():
        y_ref[...] = jnp.zeros_like(y_ref)


def _moe_experts(xs, g, wg, wu, wd, tile_expert, n_valid, tile):
    rows, d = xs.shape
    ff = wg.shape[2]
    once = dict(pipeline_mode=pl.Buffered(1))

    def row_blk(i, te, nv):
        return (jnp.minimum(i, nv[0] - 1), 0)

    def w_blk(i, te, nv):
        return (te[i], 0, 0)

    return pl.pallas_call(
        _moe_experts_kernel,
        grid_spec=pltpu.PrefetchScalarGridSpec(
            num_scalar_prefetch=2,
            grid=(rows // tile,),
            in_specs=[pl.BlockSpec((tile, d), row_blk),
                      pl.BlockSpec((1, d), lambda i, te, nv: (0, 0)),
                      pl.BlockSpec((1, d, ff), w_blk, **once),
                      pl.BlockSpec((1, d, ff), w_blk, **once),
                      pl.BlockSpec((1, ff, d), w_blk, **once)],
            out_specs=pl.BlockSpec((tile, d), lambda i, te, nv: (i, 0)),
            scratch_shapes=[pltpu.VMEM((tile, d), BF16)]),
        out_shape=jax.ShapeDtypeStruct((rows, d), F32),
        compiler_params=_cparams(("arbitrary",), 4 * _nbytes((tile, d), F32), _nbytes((tile, d), BF16),
                                 3 * _nbytes((d, ff), BF16), 4 * _nbytes((tile, ff), F32), _nbytes((tile, d), F32)),
        name="moe_experts",
    )(tile_expert, n_valid, xs, g.reshape(1, d), wg, wu, wd)


def _moe_combine_kernel(p1_ref, p2_ref, x_ref, route_ref, g_ref, y_hbm, o_ref, ya_ref, yb_ref, sem, *, final):
    rows = x_ref.shape[0]

    def issue(r, carry):
        _row_copy(y_hbm, p1_ref[0, 0, r], ya_ref, r, sem).start()
        _row_copy(y_hbm, p2_ref[0, 0, r], yb_ref, r, sem).start()
        return carry

    lax.fori_loop(0, rows, issue, 0)
    for dst in (ya_ref, yb_ref):
        pltpu.make_async_copy(y_hbm.at[pl.ds(0, rows)], dst, sem).wait()
    lane = lax.broadcasted_iota(jnp.int32, route_ref.shape, 1)
    w1 = jnp.sum(jnp.where(lane == R_W1, route_ref[...], 0.0), axis=-1, keepdims=True)
    w2 = jnp.sum(jnp.where(lane == R_W2, route_ref[...], 0.0), axis=-1, keepdims=True)
    out = x_ref[...] + w1 * ya_ref[...] + w2 * yb_ref[...]
    o_ref[...] = _rms_rows(out, g_ref[...]) if final else out


def _moe_combine(x, route, pos1, pos2, y, gain, row0, nrows, final, rows=256):
    n, d = x.shape
    rows = min(rows, nrows)
    off = row0 // rows
    smem_blk = pl.BlockSpec((1, 1, rows), lambda i: (i + off, 0, 0), memory_space=pltpu.SMEM)
    return pl.pallas_call(
        functools.partial(_moe_combine_kernel, final=final),
        grid=(nrows // rows,),
        in_specs=[smem_blk, smem_blk,
                  pl.BlockSpec((rows, d), lambda i: (i + off, 0)),
                  pl.BlockSpec((rows, LANES), lambda i: (i + off, 0)),
                  pl.BlockSpec((1, d), lambda i: (0, 0)),
                  pl.BlockSpec(memory_space=pl.ANY)],
        out_specs=pl.BlockSpec((rows, d), lambda i: (i, 0)),
        out_shape=jax.ShapeDtypeStruct((nrows, d), F32),
        scratch_shapes=[pltpu.VMEM((rows, d), F32), pltpu.VMEM((rows, d), F32), pltpu.SemaphoreType.DMA],
        compiler_params=_cparams(("arbitrary",), 8 * _nbytes((rows, d), F32)),
        name="moe_combine",
    )(pos1.reshape(n // rows, 1, rows), pos2.reshape(n // rows, 1, rows), x, route, gain.reshape(1, d), y)


def _swap_halves(w, width):
    lead = w.shape[:-1]
    return jnp.flip(w.reshape(lead + (-1, 2, width // 2)), axis=-2).reshape(w.shape)


def _pad_cols(w, width):
    return jnp.pad(w, [(0, 0)] * (w.ndim - 1) + [(0, width - w.shape[-1])])


def _even_in_weight(w):
    kr = w[:, 2 * MLA_LORA:2 * MLA_LORA + MLA_ROPE]
    return jnp.concatenate([w[:, :2 * MLA_LORA], _pad_cols(kr, LANES), _pad_cols(_swap_halves(kr, MLA_ROPE), LANES),
                            w[:, 2 * MLA_LORA + MLA_ROPE:]], axis=1).astype(BF16)


def _mla_q_weight(w):
    w = w.reshape(MLA_LORA, MLA_H, MLA_NOPE + MLA_ROPE)
    rope = w[..., MLA_NOPE:]
    w = jnp.concatenate([w[..., :MLA_NOPE], _pad_cols(rope, LANES), _pad_cols(_swap_halves(rope, MLA_ROPE), LANES)], -1)
    return w.reshape(MLA_LORA, MLA_H * 3 * LANES).astype(BF16)


def _odd_in_weight(w):
    d = w.shape[0]
    o = 0
    parts = {}
    for name, width in (("qc", WIN_H * WIN_HD), ("kc", WIN_KV * WIN_HD), ("vc", WIN_KV * WIN_HD),
                        ("qd", AX_H * AX_HD), ("kd", AX_KV * AX_HD), ("vd", AX_KV * AX_HD)):
        parts[name] = w[:, o:o + width]
        o += width

    def dup(x):
        x = x.reshape(d, WIN_KV, WIN_HD)
        return jnp.concatenate([x, x], axis=-1).reshape(d, WIN_KV * LANES)

    half = AX_HD // 2
    out = jnp.concatenate([parts["qd"], _swap_halves(parts["qd"], half), parts["qc"], parts["kd"],
                           _swap_halves(parts["kd"], half), parts["vd"], dup(parts["kc"]), dup(parts["vc"])], axis=1)
    return _pad_cols(out, O_END * LANES).astype(BF16)


def _rope_tables(pos, dim):
    inv = ROPE_THETA ** (-jnp.arange(0, dim, 2, dtype=F32) / dim)
    ang = pos[:, None] * inv[None, :]
    cos, sin = jnp.cos(ang), jnp.sin(ang)
    return jnp.concatenate([cos, cos], -1), jnp.concatenate([-sin, sin], -1)


def _alibi_slopes(n):
    return jnp.asarray(2.0 ** (-8.0 * np.arange(1, n + 1) / n), dtype=F32)


def _mixer_even(x, bsz, seq, j, norm_g, w_in, q_norm, w_uq, kv_norm, w_ukv, lb_fwd, lb_bwd, out_norm, w_out):
    n = x.shape[0]
    proj = _norm_matmul(x, norm_g, _even_in_weight(w_in), tn=10 * LANES, out_dtype=F32)
    cos, sin = _rope_tables(jnp.arange(seq, dtype=F32), MLA_ROPE)
    q, k, v = _mla_up(proj, q_norm, kv_norm, _mla_q_weight(w_uq), w_ukv.astype(BF16),
                      _pad_cols(cos, LANES), _pad_cols(sin, LANES), seq)
    o_a = _attention(q.reshape(bsz, seq, -1), k.reshape(bsz, seq, -1), v.reshape(bsz, seq, -1), MLA_H, MLA_H)

    def lower_bound(tab):
        return jnp.cumsum(jax.nn.softmax(tab.astype(F32), axis=0), axis=0)[j].reshape(1, HG_H * HG_DK)

    o_b = _hgrn2(proj.reshape(bsz, seq, -1), lower_bound(lb_fwd), lower_bound(lb_bwd), out_norm)
    return _out_proj(o_a.reshape(n, -1), o_b.reshape(n, -1), w_out.astype(BF16), x)


def _mixer_odd(x, bsz, seq, norm_g, w_in, sink, q_norm, k_norm, w_out):
    n = x.shape[0]
    proj = _norm_matmul(x, norm_g, _odd_in_weight(w_in), tn=4 * LANES, out_dtype=F32)
    o_c = _window_attention(proj.reshape(bsz, seq, -1), sink.astype(F32), _alibi_slopes(WIN_H))
    pos = jnp.arange(seq)
    half = AX_HD // 2
    c_row, s_row = _rope_tables((pos // GRID_W).astype(F32), half)
    c_col, s_col = _rope_tables((pos % GRID_W).astype(F32), half)
    cos = jnp.concatenate([c_row, c_col], -1)
    sin = jnp.concatenate([s_row, s_col], -1)

    def tables(g):
        g = g.astype(F32)
        return cos * g[None, :], sin * _swap_halves(g, half)[None, :]

    cq, sq = tables(q_norm)
    ck, sk = tables(k_norm)
    q, k, v = _axial_prep(proj, cq, sq, ck, sk, seq)
    o_d = _attention(q.reshape(bsz, seq, -1), k.reshape(bsz, seq, -1), v.reshape(bsz, seq, -1), AX_H, AX_KV)
    return _out_proj(o_c.reshape(n, -1), o_d.reshape(n, -1), w_out.astype(BF16), x)


def _moe(x, norm_g, router, w_gate, w_up, w_down, out_gain, out_splits):
    n = x.shape[0]
    tile = MOE_TM
    route, counts = _router(x, norm_g, router.astype(F32).T)
    counts = counts[0, :N_EXPERTS].astype(jnp.int32)
    padded = (counts + tile - 1) // tile * tile
    ends = jnp.cumsum(padded)
    starts = ends - padded
    e1, e2 = route[:, R_E1].astype(jnp.int32), route[:, R_E2].astype(jnp.int32)
    pos1 = starts[e1] + route[:, R_RANK1].astype(jnp.int32)
    pos2 = starts[e2] + route[:, R_RANK2].astype(jnp.int32)
    n_tiles = 2 * n // tile + N_EXPERTS
    n_valid = ends[-1] // tile
    tile_row = jnp.minimum(jnp.arange(n_tiles), n_valid - 1) * tile
    tile_expert = jnp.sum(tile_row[:, None] >= ends[None, :], axis=1).astype(jnp.int32)
    group_tail = jnp.where(padded > 0, ends - tile, -1)
    spare = (n_valid + jnp.arange(N_EXPERTS)) * tile
    fill_rows = jnp.concatenate([group_tail, jnp.where(spare < n_tiles * tile, spare, -1)]).astype(jnp.int32)
    xs = _moe_scatter(x, pos1, pos2, fill_rows, n_tiles * tile, tile)
    y = _moe_experts(xs, norm_g, w_gate.astype(BF16), w_up.astype(BF16), w_down.astype(BF16),
                     tile_expert, n_valid.reshape(1).astype(jnp.int32), tile)
    final = out_gain is not None
    gain = out_gain if final else jnp.ones((x.shape[1],), F32)
    return [_moe_combine(x, route, pos1, pos2, y, gain, row0, nrows, final) for row0, nrows in out_splits]


def _trunk(x, bsz, seq, norm_mix_e, w_in_e, mla_q_norm, mla_w_uq, mla_kv_norm, mla_w_ukv, hg_lb_fwd, hg_lb_bwd,
           hg_out_norm, w_out_e, norm_ffn_e, ffn_w_gate, ffn_w_up, ffn_w_down, norm_mix_o, w_in_o, win_sink,
           ax_q_norm, ax_k_norm, w_out_o, norm_ffn_o, moe_router, moe_w_gate, moe_w_up, moe_w_down, final_norm,
           out_splits):
    assert DEPTH % 2 == 0
    for l in range(DEPTH):
        j = l // 2
        if l % 2 == 0:
            x = _mixer_even(x, bsz, seq, j, norm_mix_e[j], w_in_e[j], mla_q_norm[j], mla_w_uq[j], mla_kv_norm[j],
                            mla_w_ukv[j], hg_lb_fwd, hg_lb_bwd, hg_out_norm[j], w_out_e[j])
            x = _ffn(x, norm_ffn_e[j], ffn_w_gate[j].astype(BF16), ffn_w_up[j].astype(BF16),
                     ffn_w_down[j].astype(BF16))
        else:
            x = _mixer_odd(x, bsz, seq, norm_mix_o[j], w_in_o[j], win_sink[j], ax_q_norm[j], ax_k_norm[j], w_out_o[j])
            last = l == DEPTH - 1
            outs = _moe(x, norm_ffn_o[j], moe_router[j], moe_w_gate[j], moe_w_up[j], moe_w_down[j],
                        final_norm if last else None, out_splits if last else [(0, x.shape[0])])
            if last:
                return outs
            x = outs[0]


def kernel(x_prompt, x_sample, norm_mix_e, w_in_e, mla_q_norm, mla_w_uq, mla_kv_norm, mla_w_ukv, hg_lb_fwd, hg_lb_bwd, hg_out_norm, w_out_e, norm_ffn_e, ffn_w_gate, ffn_w_up, ffn_w_down, norm_mix_o, w_in_o, win_sink, ax_q_norm, ax_k_norm, w_out_o, norm_ffn_o, moe_router, moe_w_gate, moe_w_up, moe_w_down, final_norm):
    bp, seq, d = x_prompt.shape
    bs = x_sample.shape[0]
    assert x_sample.shape[1:] == (seq, d)
    x = jnp.concatenate([x_prompt.reshape(bp * seq, d), x_sample.reshape(bs * seq, d)], axis=0)
    y_prompt, y_sample = _trunk(
        x, bp + bs, seq, norm_mix_e, w_in_e, mla_q_norm, mla_w_uq, mla_kv_norm, mla_w_ukv, hg_lb_fwd, hg_lb_bwd,
        hg_out_norm, w_out_e, norm_ffn_e, ffn_w_gate, ffn_w_up, ffn_w_down, norm_mix_o, w_in_o, win_sink, ax_q_norm,
        ax_k_norm, w_out_o, norm_ffn_o, moe_router, moe_w_gate, moe_w_up, moe_w_down, final_norm,
        [(0, bp * seq), (bp * seq, bs * seq)])
    return (y_prompt.reshape(bp, seq, d), y_sample.reshape(bs, seq, d))
```

```python
import functools

import jax
import jax.numpy as jnp
import numpy as np
from jax import lax
from jax.experimental import pallas as pl
from jax.experimental.pallas import tpu as pltpu

D_MODEL = 2048
DEPTH = 2
GRID_W = 64
EPS = 1e-6
ROPE_THETA = 10000.0

MLA_H = 8
MLA_NOPE = 128
MLA_ROPE = 64
MLA_V = 128
MLA_LORA = D_MODEL // 4

HG_H = 8
HG_DK = 128
HG_DV = 128

WIN_H = 16
WIN_KV = 2
WIN_HD = 64
WINDOW = 128

AX_H = 8
AX_KV = 2
AX_HD = 128

FF_DENSE = 5632
N_EXPERTS = 8
FF_EXPERT = 1408

LANES = 128
VMEM_CAP = 60000 * 1024
BF16 = jnp.bfloat16
F32 = jnp.float32

E_CQ, E_CKV, E_KRA, E_KRB, E_HQ, E_FF, E_FB, E_HI, E_HG, E_END = 0, 4, 8, 9, 10, 18, 26, 34, 42, 50
O_QD, O_QC, O_KD, O_VD, O_KC, O_VC, O_END = 0, 8, 16, 18, 20, 22, 24


def _cparams(sem, *block_bytes):
    need = int(sum(block_bytes)) + (6 << 20)
    return pltpu.CompilerParams(dimension_semantics=sem, vmem_limit_bytes=min(max(need, 16 << 20), VMEM_CAP))


def _nbytes(shape, dtype):
    return int(np.prod(shape)) * jnp.dtype(dtype).itemsize


def _rms_rows(x, g):
    return x * lax.rsqrt(jnp.mean(x * x, axis=-1, keepdims=True) + EPS) * g


def _norm_into(dst_ref, x_ref, g_ref, chunk=256):
    rows = x_ref.shape[0]
    chunk = min(chunk, rows)

    def body(c, carry):
        r = pl.ds(pl.multiple_of(c * chunk, chunk), chunk)
        dst_ref[r, :] = _rms_rows(x_ref[r, :].astype(F32), g_ref[...]).astype(dst_ref.dtype)
        return carry

    lax.fori_loop(0, rows // chunk, body, 0)


def _norm_matmul_kernel(x_ref, g_ref, w_ref, o_ref, xn_ref):
    @pl.when(pl.program_id(1) == 0)
    def _():
        _norm_into(xn_ref, x_ref, g_ref)

    o_ref[...] = jnp.dot(xn_ref[...], w_ref[...], preferred_element_type=F32).astype(o_ref.dtype)


def _norm_matmul(x, g, w, tn, out_dtype, tm=1024):
    n, k = x.shape
    nout = w.shape[1]
    tm = min(tm, n)
    return pl.pallas_call(
        _norm_matmul_kernel,
        grid=(n // tm, nout // tn),
        in_specs=[pl.BlockSpec((tm, k), lambda i, j: (i, 0)),
                  pl.BlockSpec((1, k), lambda i, j: (0, 0)),
                  pl.BlockSpec((k, tn), lambda i, j: (0, j))],
        out_specs=pl.BlockSpec((tm, tn), lambda i, j: (i, j)),
        out_shape=jax.ShapeDtypeStruct((n, nout), out_dtype),
        scratch_shapes=[pltpu.VMEM((tm, k), BF16)],
        compiler_params=_cparams(("parallel", "arbitrary"), 2 * _nbytes((tm, k), x.dtype), _nbytes((tm, k), BF16),
                                 2 * _nbytes((k, tn), BF16), 3 * _nbytes((tm, tn), F32)),
        name="norm_matmul",
    )(x, g.reshape(1, k), w)


def _partner32(x):
    lane = lax.broadcasted_iota(jnp.int32, x.shape, 1)
    return jnp.where((lane & 32) == 0, pltpu.roll(x, LANES - 32, axis=1), pltpu.roll(x, 32, axis=1))


def _mla_up_kernel(cq_ref, ckv_ref, kra_ref, krb_ref, qn_ref, kvn_ref, wq_ref, wkv_ref, cos_ref, sin_ref,
                   q_ref, k_ref, v_ref, cqn_ref, ckvn_ref, *, scale):
    _norm_into(cqn_ref, cq_ref, qn_ref)
    _norm_into(ckvn_ref, ckv_ref, kvn_ref)
    cos, sin = cos_ref[...], sin_ref[...]
    k_rope = (kra_ref[...] * cos + krb_ref[...] * sin).astype(BF16)
    q = jnp.dot(cqn_ref[...], wq_ref[...], preferred_element_type=F32)
    kv = jnp.dot(ckvn_ref[...], wkv_ref[...], preferred_element_type=F32)
    for h in range(MLA_H):
        lo, mid, hi = 2 * h * LANES, (2 * h + 1) * LANES, (2 * h + 2) * LANES
        q_r = q[:, mid:hi]
        q_ref[:, lo:mid] = (q[:, lo:mid] * scale).astype(BF16)
        q_ref[:, mid:hi] = ((q_r * cos + _partner32(q_r) * sin) * scale).astype(BF16)
        k_ref[:, lo:mid] = kv[:, lo:mid].astype(BF16)
        k_ref[:, mid:hi] = k_rope
        v_ref[:, h * LANES:(h + 1) * LANES] = kv[:, mid:hi].astype(BF16)


def _mla_up(proj, qn, kvn, wq, wkv, cos_t, sin_t, seq, tm=512):
    n = proj.shape[0]
    tm = min(tm, seq)
    nt = seq // tm
    lora = MLA_LORA
    width = MLA_H * 2 * LANES
    scale = float((MLA_NOPE + MLA_ROPE) ** -0.5) * LOG2E
    return pl.pallas_call(
        functools.partial(_mla_up_kernel, scale=scale),
        grid=(n // tm,),
        in_specs=[pl.BlockSpec((tm, lora), lambda i: (i, 0)),
                  pl.BlockSpec((tm, lora), lambda i: (i, 1)),
                  pl.BlockSpec((tm, LANES), lambda i: (i, E_KRA)),
                  pl.BlockSpec((tm, LANES), lambda i: (i, E_KRB)),
                  pl.BlockSpec((1, lora), lambda i: (0, 0)),
                  pl.BlockSpec((1, lora), lambda i: (0, 0)),
                  pl.BlockSpec((lora, width), lambda i: (0, 0)),
                  pl.BlockSpec((lora, width), lambda i: (0, 0)),
                  pl.BlockSpec((tm, LANES), lambda i: (i % nt, 0)),
                  pl.BlockSpec((tm, LANES), lambda i: (i % nt, 0))],
        out_specs=[pl.BlockSpec((tm, width), lambda i: (i, 0)),
                   pl.BlockSpec((tm, width), lambda i: (i, 0)),
                   pl.BlockSpec((tm, MLA_H * LANES), lambda i: (i, 0))],
        out_shape=[jax.ShapeDtypeStruct((n, width), BF16),
                   jax.ShapeDtypeStruct((n, width), BF16),
                   jax.ShapeDtypeStruct((n, MLA_H * LANES), BF16)],
        scratch_shapes=[pltpu.VMEM((tm, lora), BF16), pltpu.VMEM((tm, lora), BF16)],
        compiler_params=_cparams(("parallel",), 4 * _nbytes((tm, lora), F32), 12 * _nbytes((tm, LANES), F32),
                                 4 * _nbytes((lora, width), BF16), 2 * _nbytes((tm, 5 * MLA_H * LANES), BF16),
                                 3 * _nbytes((tm, width), F32)),
        name="mla_up",
    )(proj, proj, proj, proj, qn.reshape(1, lora), kvn.reshape(1, lora), wq, wkv, cos_t, sin_t)


ATT_TK = 1024
ATT_SUB = 1024
LOG2E = 1.4426950408889634


def _attention_kernel(q_ref, k_ref, v_ref, o_ref):
    tq = q_ref.shape[1]
    t = k_ref.shape[1]
    for r0 in range(0, tq, ATT_SUB):
        q = q_ref[0, r0:r0 + ATT_SUB, :]
        m = l = acc = None
        for c0 in range(0, t, ATT_TK):
            s = lax.dot_general(q, k_ref[0, c0:c0 + ATT_TK, :], (((1,), (1,)), ((), ())),
                                preferred_element_type=F32)
            m_c = jnp.max(s, axis=-1, keepdims=True)
            m_new = m_c if m is None else jnp.maximum(m, m_c)
            p = jnp.exp2(s - m_new)
            pv = jnp.dot(p.astype(BF16), v_ref[0, c0:c0 + ATT_TK, :], preferred_element_type=F32)
            l_c = jnp.sum(p, axis=-1, keepdims=True)
            if m is None:
                l, acc = l_c, pv
            else:
                alpha = jnp.exp2(m - m_new)
                l, acc = alpha * l + l_c, alpha * acc + pv
            m = m_new
        o_ref[0, r0:r0 + ATT_SUB, :] = (acc / l).astype(o_ref.dtype)


def _attention(q, k, v, heads, kv_heads, tq=1024):
    b, t, _ = q.shape
    dq = q.shape[2] // heads
    dv = v.shape[2] // kv_heads
    g = heads // kv_heads
    tq = min(tq, t)
    return pl.pallas_call(
        _attention_kernel,
        grid=(b, heads, t // tq),
        in_specs=[pl.BlockSpec((1, tq, dq), lambda bi, h, qi: (bi, qi, h)),
                  pl.BlockSpec((1, t, dq), lambda bi, h, qi: (bi, 0, h // g)),
                  pl.BlockSpec((1, t, dv), lambda bi, h, qi: (bi, 0, h // g))],
        out_specs=pl.BlockSpec((1, tq, dv), lambda bi, h, qi: (bi, qi, h)),
        out_shape=jax.ShapeDtypeStruct((b, t, heads * dv), BF16),
        compiler_params=_cparams(("parallel", "parallel", "arbitrary"), 2 * _nbytes((tq, dq), BF16),
                                 2 * _nbytes((t, dq + dv), BF16), 2 * _nbytes((tq, dv), BF16),
                                 6 * _nbytes((ATT_SUB, ATT_TK), F32)),
        name="attention",
    )(q, k, v)


HG_CHUNK = 128
HG_DIRECT_CHUNK = 32
HG_MAX_LOG_RANGE = 80.0


def _split3(x):
    a = x.astype(BF16)
    r = x - a.astype(F32)
    b = r.astype(BF16)
    c = (r - b.astype(F32)).astype(BF16)
    return a, b, c


def _tri_masks(c_sz, reverse):
    row = lax.broadcasted_iota(jnp.int32, (c_sz, c_sz), 0)
    col = lax.broadcasted_iota(jnp.int32, (c_sz, c_sz), 1)
    keep = (row <= col) if reverse else (row >= col)
    return keep, jnp.where(keep, 1.0, 0.0).astype(BF16)


def _hgrn2_kernel(hq_ref, ff_ref, fb_ref, hi_ref, hg_ref, lbf_ref, lbb_ref, on_ref, o_ref,
                  accf_ref, accb_ref, sf_ref, sb_ref, dev_ref):
    t = hq_ref.shape[1]

    def chunk_inputs(f_ref, lb, rows):
        gate = lb + (1.0 - lb) * jax.nn.sigmoid(f_ref[0, rows, :])
        hq = hq_ref[0, rows, :]
        return hq * jax.nn.sigmoid(hq), 1.0 - gate, hi_ref[0, rows, :], jnp.log(gate)

    def log_decay(lf, tri):
        p0, p1, p2 = _split3(lf)
        return (jnp.dot(tri, p0, preferred_element_type=F32) + jnp.dot(tri, p1, preferred_element_type=F32)
                + jnp.dot(tri, p2, preferred_element_type=F32))

    def state_step(s_ref, q, k, v, b, b_last):
        s_t = s_ref[...]
        inter = lax.dot_general((q * jnp.exp(b)).astype(BF16), s_t.astype(BF16), (((1,), (1,)), ((), ())),
                                preferred_element_type=F32)
        kd = (k * jnp.exp(b_last - b)).astype(BF16)
        upd = lax.dot_general(v.astype(BF16), kd, (((0,), (0,)), ((), ())), preferred_element_type=F32)
        s_ref[...] = jnp.exp(b_last) * s_t + upd
        return inter

    def fast_step(f_ref, lb, s_ref, acc_ref, c, masks, reverse):
        c_sz = HG_CHUNK
        keep, tri = masks
        rows = pl.ds(pl.multiple_of(c * c_sz, c_sz), c_sz)
        q, k, v, lf = chunk_inputs(f_ref, lb, rows)
        b = log_decay(lf, tri)
        r = b[c_sz // 2:c_sz // 2 + 1, :]
        b_last = b[0:1, :] if reverse else b[c_sz - 1:c_sz, :]
        b_first = b[c_sz - 1:c_sz, :] if reverse else b[0:1, :]
        dev_ref[...] = jnp.maximum(dev_ref[...], jnp.maximum(jnp.abs(b_first - r), jnp.abs(b_last - r)))
        qe = (q * jnp.exp(b - r)).astype(BF16)
        ke = (k * jnp.exp(r - b)).astype(BF16)
        a = lax.dot_general(qe, ke, (((1,), (1,)), ((), ())), preferred_element_type=F32)
        a = jnp.where(keep, a, 0.0).astype(BF16)
        intra = jnp.dot(a, v.astype(BF16), preferred_element_type=F32)
        acc_ref[rows, :] = intra + state_step(s_ref, q, k, v, b, b_last)

    def direct_step(f_ref, lb, s_ref, acc_ref, c, masks, reverse):
        c_sz = HG_DIRECT_CHUNK
        _, tri = masks
        row1 = lax.broadcasted_iota(jnp.int32, (c_sz, 1), 0)
        rows = pl.ds(pl.multiple_of(c * c_sz, c_sz), c_sz)
        q, k, v, lf = chunk_inputs(f_ref, lb, rows)
        b = log_decay(lf, tri)
        b_last = b[0:1, :] if reverse else b[c_sz - 1:c_sz, :]
        o = state_step(s_ref, q, k, v, b, b_last)
        for s in range(c_sz):
            e = jnp.exp(jnp.minimum(b - b[s:s + 1, :], 0.0))
            a = jnp.sum(q * k[s:s + 1, :] * e, axis=-1, keepdims=True)
            a = jnp.where((row1 <= s) if reverse else (row1 >= s), a, 0.0)
            o = o + a * v[s:s + 1, :]
        acc_ref[rows, :] = o

    def scan_both(step, c_sz):
        nc = t // c_sz
        masks_f = _tri_masks(c_sz, False)
        masks_b = _tri_masks(c_sz, True)
        sf_ref[...] = jnp.zeros_like(sf_ref)
        sb_ref[...] = jnp.zeros_like(sb_ref)

        def body(ci, carry):
            step(ff_ref, lbf_ref[...], sf_ref, accf_ref, ci, masks_f, False)
            step(fb_ref, lbb_ref[...], sb_ref, accb_ref, nc - 1 - ci, masks_b, True)
            return carry

        lax.fori_loop(0, nc, body, 0, unroll=4)

    dev_ref[...] = jnp.zeros_like(dev_ref)
    scan_both(fast_step, HG_CHUNK)

    @pl.when(jnp.logical_not(jnp.max(dev_ref[...]) <= HG_MAX_LOG_RANGE))
    def _():
        scan_both(direct_step, HG_DIRECT_CHUNK)

    def finish(c, carry):
        rows = pl.ds(pl.multiple_of(c * 256, 256), 256)
        hg = hg_ref[0, rows, :]
        y = _rms_rows(accf_ref[rows, :] + accb_ref[rows, :], on_ref[...]) * (hg * jax.nn.sigmoid(hg))
        o_ref[0, rows, :] = y.astype(o_ref.dtype)
        return carry

    lax.fori_loop(0, t // 256, finish, 0)


def _hgrn2(proj, lb_f, lb_b, out_norm):
    b, t, _ = proj.shape

    def col(base):
        return pl.BlockSpec((1, t, LANES), lambda bi, h: (bi, 0, base + h))

    return pl.pallas_call(
        _hgrn2_kernel,
        grid=(b, HG_H),
        in_specs=[col(E_HQ), col(E_FF), col(E_FB), col(E_HI), col(E_HG),
                  pl.BlockSpec((1, LANES), lambda bi, h: (0, h)),
                  pl.BlockSpec((1, LANES), lambda bi, h: (0, h)),
                  pl.BlockSpec((1, LANES), lambda bi, h: (0, 0))],
        out_specs=pl.BlockSpec((1, t, LANES), lambda bi, h: (bi, 0, h)),
        out_shape=jax.ShapeDtypeStruct((b, t, HG_H * HG_DV), BF16),
        scratch_shapes=[pltpu.VMEM((t, HG_DV), F32), pltpu.VMEM((t, HG_DV), F32),
                        pltpu.VMEM((HG_DV, HG_DK), F32), pltpu.VMEM((HG_DV, HG_DK), F32), pltpu.VMEM((1, HG_DK), F32)],
        compiler_params=_cparams(("parallel", "parallel"), 10 * _nbytes((t, LANES), F32), 4 * _nbytes((t, LANES), F32)),
        name="hgrn2",
    )(proj, proj, proj, proj, proj, lb_f, lb_b, out_norm.reshape(1, HG_DV))


def _out_proj_kernel(a_ref, b_ref, wa_ref, wb_ref, x_ref, o_ref):
    acc = jnp.dot(a_ref[...], wa_ref[...], preferred_element_type=F32)
    acc = acc + jnp.dot(b_ref[...], wb_ref[...], preferred_element_type=F32)
    o_ref[...] = x_ref[...] + acc


def _out_proj(a, b, w, x, tm=1024, tn=512):
    n, ka = a.shape
    d = w.shape[1]
    tm = min(tm, n)
    return pl.pallas_call(
        _out_proj_kernel,
        grid=(n // tm, d // tn),
        in_specs=[pl.BlockSpec((tm, ka), lambda i, j: (i, 0)),
                  pl.BlockSpec((tm, ka), lambda i, j: (i, 0)),
                  pl.BlockSpec((ka, tn), lambda i, j: (0, j)),
                  pl.BlockSpec((ka, tn), lambda i, j: (1, j)),
                  pl.BlockSpec((tm, tn), lambda i, j: (i, j))],
        out_specs=pl.BlockSpec((tm, tn), lambda i, j: (i, j)),
        out_shape=jax.ShapeDtypeStruct((n, d), F32),
        compiler_params=_cparams(("parallel", "arbitrary"), 4 * _nbytes((tm, ka), BF16), 4 * _nbytes((ka, tn), BF16),
                                 5 * _nbytes((tm, tn), F32)),
        name="out_proj",
    )(a, b, w, w, x)


def _ffn_kernel(x_ref, g_ref, wg_ref, wu_ref, wd_ref, o_ref, xn_ref):
    f = pl.program_id(1)

    @pl.when(f == 0)
    def _():
        _norm_into(xn_ref, x_ref, g_ref)
        o_ref[...] = x_ref[...]

    xn = xn_ref[...]
    gt = jnp.dot(xn, wg_ref[...], preferred_element_type=F32)
    up = jnp.dot(xn, wu_ref[...], preferred_element_type=F32)
    act = (gt * jax.nn.sigmoid(gt) * up).astype(BF16)
    o_ref[...] += jnp.dot(act, wd_ref[...], preferred_element_type=F32)


def _ffn(x, g, wg, wu, wd, tm=512, tf=512):
    n, d = x.shape
    ff = wg.shape[1]
    tm = min(tm, n)
    return pl.pallas_call(
        _ffn_kernel,
        grid=(n // tm, ff // tf),
        in_specs=[pl.BlockSpec((tm, d), lambda i, f: (i, 0)),
                  pl.BlockSpec((1, d), lambda i, f: (0, 0)),
                  pl.BlockSpec((d, tf), lambda i, f: (0, f)),
                  pl.BlockSpec((d, tf), lambda i, f: (0, f)),
                  pl.BlockSpec((tf, d), lambda i, f: (f, 0))],
        out_specs=pl.BlockSpec((tm, d), lambda i, f: (i, 0)),
        out_shape=jax.ShapeDtypeStruct((n, d), F32),
        scratch_shapes=[pltpu.VMEM((tm, d), BF16)],
        compiler_params=_cparams(("parallel", "arbitrary"), 4 * _nbytes((tm, d), F32), _nbytes((tm, d), BF16),
                                 6 * _nbytes((d, tf), BF16), 4 * _nbytes((tm, tf), F32), _nbytes((tm, d), F32)),
        name="ffn",
    )(x, g.reshape(1, d), wg, wu, wd)


def _axial_prep_kernel(qd_ref, kd_ref, vd_ref, cq_ref, sq_ref, ck_ref, sk_ref, q_ref, k_ref, v_ref, *, scale):
    def rope(x_ref, c_ref, s_ref, h, mul):
        x = x_ref[:, h * LANES:(h + 1) * LANES]
        r = lax.rsqrt(jnp.mean(x * x, axis=-1, keepdims=True) + EPS)
        return ((x * c_ref[...] + _partner32(x) * s_ref[...]) * (r * mul)).astype(BF16)

    for h in range(AX_H):
        q_ref[:, h * LANES:(h + 1) * LANES] = rope(qd_ref, cq_ref, sq_ref, h, scale)
    for h in range(AX_KV):
        k_ref[:, h * LANES:(h + 1) * LANES] = rope(kd_ref, ck_ref, sk_ref, h, 1.0)
    v_ref[...] = vd_ref[...].astype(BF16)


def _axial_prep(proj, cq, sq, ck, sk, seq, tm=512):
    n = proj.shape[0]
    tm = min(tm, seq)
    nt = seq // tm
    qw, kw = AX_H * AX_HD, AX_KV * AX_HD
    tab = pl.BlockSpec((tm, LANES), lambda i: (i % nt, 0))
    return pl.pallas_call(
        functools.partial(_axial_prep_kernel, scale=float(AX_HD ** -0.5) * LOG2E),
        grid=(n // tm,),
        in_specs=[pl.BlockSpec((tm, qw), lambda i: (i, O_QD * LANES // qw)),
                  pl.BlockSpec((tm, kw), lambda i: (i, O_KD * LANES // kw)),
                  pl.BlockSpec((tm, kw), lambda i: (i, O_VD * LANES // kw)),
                  tab, tab, tab, tab],
        out_specs=[pl.BlockSpec((tm, qw), lambda i: (i, 0)),
                   pl.BlockSpec((tm, kw), lambda i: (i, 0)),
                   pl.BlockSpec((tm, kw), lambda i: (i, 0))],
        out_shape=[jax.ShapeDtypeStruct((n, qw), BF16), jax.ShapeDtypeStruct((n, kw), BF16),
                   jax.ShapeDtypeStruct((n, kw), BF16)],
        compiler_params=_cparams(("parallel",), 4 * _nbytes((tm, qw), F32), 6 * _nbytes((tm, kw), F32),
                                 8 * _nbytes((tm, LANES), F32), 2 * _nbytes((tm, qw + 2 * kw), BF16)),
        name="axial_prep",
    )(proj, proj, proj, cq, sq, ck, sk)


WIN_BLOCK = 128


def _window_kernel(sink_ref, slope_ref, q_ref, k_ref, v_ref, o_ref, *, scale):
    t = q_ref.shape[1]
    wb = WIN_BLOCK
    span = 3 * wb
    pair = pl.program_id(1)
    low = lax.broadcasted_iota(jnp.int32, (wb, LANES), 1) < WIN_HD
    row = lax.broadcasted_iota(jnp.int32, (2 * wb, span), 0)
    delta = (row & (wb - 1)) - lax.broadcasted_iota(jnp.int32, (2 * wb, span), 1)
    top = lax.broadcasted_iota(jnp.int32, (2 * wb, 1), 0) < wb
    slope = jnp.where(top, slope_ref[2 * pair], slope_ref[2 * pair + 1]) * LOG2E
    sink = jnp.where(top, sink_ref[2 * pair], sink_ref[2 * pair + 1]) * LOG2E

    def body(qb, carry):
        start = pl.multiple_of(jnp.clip((qb - 1) * wb, 0, t - span), wb)
        kwin = k_ref[0, pl.ds(start, span), :].astype(BF16)
        vwin = v_ref[0, pl.ds(start, span), :].astype(BF16)
        qrows = pl.ds(pl.multiple_of(qb * wb, wb), wb)
        q2 = q_ref[0, qrows, :] * (scale * LOG2E)
        qs = jnp.concatenate([jnp.where(low, q2, 0.0), jnp.where(low, 0.0, q2)], axis=0).astype(BF16)
        dist = jnp.abs(delta + (qb * wb - start))
        s = lax.dot_general(qs, kwin, (((1,), (1,)), ((), ())), preferred_element_type=F32)
        s = jnp.where(dist <= WINDOW, s - slope * dist.astype(F32), -jnp.inf)
        m = jnp.maximum(jnp.max(s, axis=-1, keepdims=True), sink)
        e = jnp.exp2(s - m)
        den = jnp.sum(e, axis=-1, keepdims=True) + jnp.exp2(sink - m)
        o = jnp.dot(e.astype(BF16), vwin, preferred_element_type=F32) / den
        o_ref[0, qrows, :] = jnp.where(low, o[:wb], o[wb:]).astype(o_ref.dtype)
        return carry

    lax.fori_loop(0, t // wb, body, 0, unroll=4)


def _window_attention(proj, sink, slopes):
    b, t, _ = proj.shape
    pairs = WIN_H // 2
    per_kv = pairs // WIN_KV
    smem = pl.BlockSpec(memory_space=pltpu.SMEM)
    return pl.pallas_call(
        functools.partial(_window_kernel, scale=float(WIN_HD ** -0.5)),
        grid=(b, pairs),
        in_specs=[smem, smem,
                  pl.BlockSpec((1, t, LANES), lambda bi, p: (bi, 0, O_QC + p)),
                  pl.BlockSpec((1, t, LANES), lambda bi, p: (bi, 0, O_KC + p // per_kv)),
                  pl.BlockSpec((1, t, LANES), lambda bi, p: (bi, 0, O_VC + p // per_kv))],
        out_specs=pl.BlockSpec((1, t, LANES), lambda bi, p: (bi, 0, p)),
        out_shape=jax.ShapeDtypeStruct((b, t, WIN_H * WIN_HD), BF16),
        compiler_params=_cparams(("parallel", "parallel"), 6 * _nbytes((t, LANES), F32), 2 * _nbytes((t, LANES), BF16)),
        name="window_attention",
    )(sink, slopes, proj, proj, proj)


MOE_TM = 512
R_E1, R_E2, R_W1, R_W2, R_RANK1, R_RANK2 = range(6)


def _router_kernel(x_ref, g_ref, r_ref, route_ref, cnt_ref, carry_ref):
    @pl.when(pl.program_id(0) == 0)
    def _():
        carry_ref[...] = jnp.zeros_like(carry_ref)

    tm = x_ref.shape[0]
    xn = _rms_rows(x_ref[...], g_ref[...])
    logits = [jnp.sum(xn * r_ref[e:e + 1, :], axis=-1, keepdims=True) for e in range(N_EXPERTS)]

    def top(ls):
        m = functools.reduce(jnp.maximum, ls)
        idx = jnp.full_like(m, N_EXPERTS).astype(jnp.int32)
        for e in reversed(range(N_EXPERTS)):
            idx = jnp.where(ls[e] == m, e, idx)
        return m, idx

    m1, i1 = top(logits)
    m2, i2 = top([jnp.where(i1 == e, -jnp.inf, logits[e]) for e in range(N_EXPERTS)])
    e2 = jnp.exp(m2 - m1)
    w1 = 1.0 / (1.0 + e2)
    w2 = e2 / (1.0 + e2)
    lane = lax.broadcasted_iota(jnp.int32, route_ref.shape, 1)
    sel1, sel2 = lane == i1, lane == i2
    onehot = jnp.where(jnp.logical_or(sel1, sel2), 1.0, 0.0)
    row = lax.broadcasted_iota(jnp.int32, (tm, tm), 0)
    col = lax.broadcasted_iota(jnp.int32, (tm, tm), 1)
    earlier = jnp.where(row > col, 1.0, 0.0).astype(BF16)
    before = carry_ref[...] + jnp.dot(earlier, onehot.astype(BF16), preferred_element_type=F32)
    rank1 = jnp.sum(jnp.where(sel1, before, 0.0), axis=-1, keepdims=True)
    rank2 = jnp.sum(jnp.where(sel2, before, 0.0), axis=-1, keepdims=True)
    carry_ref[...] = carry_ref[...] + jnp.sum(onehot, axis=0, keepdims=True)
    cnt_ref[...] = carry_ref[...]
    rec = jnp.zeros(route_ref.shape, F32)
    for pos, val in ((R_E1, i1.astype(F32)), (R_E2, i2.astype(F32)), (R_W1, w1), (R_W2, w2),
                     (R_RANK1, rank1), (R_RANK2, rank2)):
        rec = jnp.where(lane == pos, val, rec)
    route_ref[...] = rec


def _router(x, g, router_t, tm=256):
    n, d = x.shape
    tm = min(tm, n)
    return pl.pallas_call(
        _router_kernel,
        grid=(n // tm,),
        in_specs=[pl.BlockSpec((tm, d), lambda i: (i, 0)),
                  pl.BlockSpec((1, d), lambda i: (0, 0)),
                  pl.BlockSpec((N_EXPERTS, d), lambda i: (0, 0))],
        out_specs=[pl.BlockSpec((tm, LANES), lambda i: (i, 0)),
                   pl.BlockSpec((1, LANES), lambda i: (0, 0))],
        out_shape=[jax.ShapeDtypeStruct((n, LANES), F32), jax.ShapeDtypeStruct((1, LANES), F32)],
        scratch_shapes=[pltpu.VMEM((1, LANES), F32)],
        compiler_params=_cparams(("arbitrary",), 6 * _nbytes((tm, d), F32)),
        name="router",
    )(x, g.reshape(1, d), router_t)


def _row_copy(src_hbm, src_row, dst, dst_row, sem):
    return pltpu.make_async_copy(src_hbm.at[pl.ds(src_row, 1)], dst.at[pl.ds(dst_row, 1)], sem)


def _moe_scatter_kernel(fill_ref, p1_ref, p2_ref, x_ref, xs_hbm, zero_ref, sem, zsem, *, tile):
    i = pl.program_id(0)
    rows = p1_ref.shape[-1]

    @pl.when(i == 0)
    def _():
        zero_ref[...] = jnp.zeros_like(zero_ref)
        for k in range(fill_ref.shape[0]):
            @pl.when(fill_ref[k] >= 0)
            def _():
                start = pl.multiple_of(fill_ref[k], tile)
                fill = pltpu.make_async_copy(zero_ref, xs_hbm.at[pl.ds(start, tile)], zsem)
                fill.start()
                fill.wait()

    def issue(r, carry):
        _row_copy(x_ref, r, xs_hbm, p1_ref[0, 0, r], sem).start()
        _row_copy(x_ref, r, xs_hbm, p2_ref[0, 0, r], sem).start()
        return carry

    lax.fori_loop(0, rows, issue, 0, unroll=8)
    for _ in range(2):
        pltpu.make_async_copy(x_ref, xs_hbm.at[pl.ds(0, rows)], sem).wait()


def _moe_scatter(x, pos1, pos2, fill_rows, total_rows, tile, rows=256):
    n, d = x.shape
    rows = min(rows, n)
    smem_blk = pl.BlockSpec((1, 1, rows), lambda i, fill: (i, 0, 0), memory_space=pltpu.SMEM)
    return pl.pallas_call(
        functools.partial(_moe_scatter_kernel, tile=tile),
        grid_spec=pltpu.PrefetchScalarGridSpec(
            num_scalar_prefetch=1,
            grid=(n // rows,),
            in_specs=[smem_blk, smem_blk, pl.BlockSpec((rows, d), lambda i, fill: (i, 0))],
            out_specs=pl.BlockSpec(memory_space=pl.ANY),
            scratch_shapes=[pltpu.VMEM((tile, d), x.dtype), pltpu.SemaphoreType.DMA, pltpu.SemaphoreType.DMA]),
        out_shape=jax.ShapeDtypeStruct((total_rows, d), x.dtype),
        compiler_params=_cparams(("arbitrary",), _nbytes((tile, d), x.dtype), 2 * _nbytes((rows, d), x.dtype)),
        name="moe_scatter",
    )(fill_rows, pos1.reshape(n // rows, 1, rows), pos2.reshape(n // rows, 1, rows), x)


def _moe_experts_kernel(te_ref, nv_ref, xs_ref, g_ref, wg_ref, wu_ref, wd_ref, y_ref, xn_ref):
    valid = pl.program_id(0) < nv_ref[0]

    @pl.when(valid)
    def _():
        _norm_into(xn_ref, xs_ref, g_ref)
        xn = xn_ref[...]
        gt = jnp.dot(xn, wg_ref[0], preferred_element_type=F32)
        up = jnp.dot(xn, wu_ref[0], preferred_element_type=F32)
        act = (gt * jax.nn.sigmoid(gt) * up).astype(BF16)
        y_ref[...] = jnp.dot(act, wd_ref[0], preferred_element_type=F32)

    @pl.when(jnp.logical_not(valid))
    def _():
        y_ref[...] = jnp.zeros_like(y_ref)


def _moe_experts(xs, g, wg, wu, wd, tile_expert, n_valid, tile):
    rows, d = xs.shape
    ff = wg.shape[2]
    once = dict(pipeline_mode=pl.Buffered(1))

    def row_blk(i, te, nv):
        return (jnp.minimum(i, nv[0] - 1), 0)

    def w_blk(i, te, nv):
        return (te[i], 0, 0)

    return pl.pallas_call(
        _moe_experts_kernel,
        grid_spec=pltpu.PrefetchScalarGridSpec(
            num_scalar_prefetch=2,
            grid=(rows // tile,),
            in_specs=[pl.BlockSpec((tile, d), row_blk),
                      pl.BlockSpec((1, d), lambda i, te, nv: (0, 0)),
                      pl.BlockSpec((1, d, ff), w_blk, **once),
                      pl.BlockSpec((1, d, ff), w_blk, **once),
                      pl.BlockSpec((1, ff, d), w_blk, **once)],
            out_specs=pl.BlockSpec((tile, d), lambda i, te, nv: (i, 0)),
            scratch_shapes=[pltpu.VMEM((tile, d), BF16)]),
        out_shape=jax.ShapeDtypeStruct((rows, d), F32),
        compiler_params=_cparams(("arbitrary",), 4 * _nbytes((tile, d), F32), _nbytes((tile, d), BF16),
                                 3 * _nbytes((d, ff), BF16), 4 * _nbytes((tile, ff), F32), _nbytes((tile, d), F32)),
        name="moe_experts",
    )(tile_expert, n_valid, xs, g.reshape(1, d), wg, wu, wd)


def _moe_combine_kernel(p1_ref, p2_ref, x_ref, route_ref, g_ref, y_hbm, o_ref, ya_ref, yb_ref, sem, *, final):
    rows = x_ref.shape[0]

    def issue(r, carry):
        _row_copy(y_hbm, p1_ref[0, 0, r], ya_ref, r, sem).start()
        _row_copy(y_hbm, p2_ref[0, 0, r], yb_ref, r, sem).start()
        return carry

    lax.fori_loop(0, rows, issue, 0, unroll=8)
    for dst in (ya_ref, yb_ref):
        pltpu.make_async_copy(y_hbm.at[pl.ds(0, rows)], dst, sem).wait()
    lane = lax.broadcasted_iota(jnp.int32, route_ref.shape, 1)
    w1 = jnp.sum(jnp.where(lane == R_W1, route_ref[...], 0.0), axis=-1, keepdims=True)
    w2 = jnp.sum(jnp.where(lane == R_W2, route_ref[...], 0.0), axis=-1, keepdims=True)
    out = x_ref[...] + w1 * ya_ref[...] + w2 * yb_ref[...]
    o_ref[...] = _rms_rows(out, g_ref[...]) if final else out


def _moe_combine(x, route, pos1, pos2, y, gain, row0, nrows, final, rows=256):
    n, d = x.shape
    rows = min(rows, nrows)
    off = row0 // rows
    smem_blk = pl.BlockSpec((1, 1, rows), lambda i: (i + off, 0, 0), memory_space=pltpu.SMEM)
    return pl.pallas_call(
        functools.partial(_moe_combine_kernel, final=final),
        grid=(nrows // rows,),
        in_specs=[smem_blk, smem_blk,
                  pl.BlockSpec((rows, d), lambda i: (i + off, 0)),
                  pl.BlockSpec((rows, LANES), lambda i: (i + off, 0)),
                  pl.BlockSpec((1, d), lambda i: (0, 0)),
                  pl.BlockSpec(memory_space=pl.ANY)],
        out_specs=pl.BlockSpec((rows, d), lambda i: (i, 0)),
        out_shape=jax.ShapeDtypeStruct((nrows, d), F32),
        scratch_shapes=[pltpu.VMEM((rows, d), F32), pltpu.VMEM((rows, d), F32), pltpu.SemaphoreType.DMA],
        compiler_params=_cparams(("arbitrary",), 8 * _nbytes((rows, d), F32)),
        name="moe_combine",
    )(pos1.reshape(n // rows, 1, rows), pos2.reshape(n // rows, 1, rows), x, route, gain.reshape(1, d), y)


def _swap_halves(w, width):
    lead = w.shape[:-1]
    return jnp.flip(w.reshape(lead + (-1, 2, width // 2)), axis=-2).reshape(w.shape)


def _pad_cols(w, width):
    return jnp.pad(w, [(0, 0)] * (w.ndim - 1) + [(0, width - w.shape[-1])])


def _even_in_weight(w):
    kr = w[:, 2 * MLA_LORA:2 * MLA_LORA + MLA_ROPE]
    return jnp.concatenate([w[:, :2 * MLA_LORA], _pad_cols(kr, LANES), _pad_cols(_swap_halves(kr, MLA_ROPE), LANES),
                            w[:, 2 * MLA_LORA + MLA_ROPE:]], axis=1).astype(BF16)


def _mla_q_weight(w):
    w = w.reshape(MLA_LORA, MLA_H, MLA_NOPE + MLA_ROPE)
    rope = w[..., MLA_NOPE:]
    w = jnp.concatenate([w[..., :MLA_NOPE], _pad_cols(rope, LANES)], -1)
    return w.reshape(MLA_LORA, MLA_H * 2 * LANES).astype(BF16)


def _odd_in_weight(w):
    d = w.shape[0]
    o = 0
    parts = {}
    for name, width in (("qc", WIN_H * WIN_HD), ("kc", WIN_KV * WIN_HD), ("vc", WIN_KV * WIN_HD),
                        ("qd", AX_H * AX_HD), ("kd", AX_KV * AX_HD), ("vd", AX_KV * AX_HD)):
        parts[name] = w[:, o:o + width]
        o += width

    def dup(x):
        x = x.reshape(d, WIN_KV, WIN_HD)
        return jnp.concatenate([x, x], axis=-1).reshape(d, WIN_KV * LANES)

    out = jnp.concatenate([parts["qd"], parts["qc"], parts["kd"], parts["vd"], dup(parts["kc"]), dup(parts["vc"])],
                          axis=1)
    assert out.shape[1] == O_END * LANES
    return out.astype(BF16)


def _rope_tables(pos, dim):
    inv = ROPE_THETA ** (-jnp.arange(0, dim, 2, dtype=F32) / dim)
    ang = pos[:, None] * inv[None, :]
    cos, sin = jnp.cos(ang), jnp.sin(ang)
    return jnp.concatenate([cos, cos], -1), jnp.concatenate([-sin, sin], -1)


def _alibi_slopes(n):
    return jnp.asarray(2.0 ** (-8.0 * np.arange(1, n + 1) / n), dtype=F32)


def _mixer_even(x, bsz, seq, j, norm_g, w_in, q_norm, w_uq, kv_norm, w_ukv, lb_fwd, lb_bwd, out_norm, w_out):
    n = x.shape[0]
    proj = _norm_matmul(x, norm_g, _even_in_weight(w_in), tn=10 * LANES, out_dtype=F32)
    cos, sin = _rope_tables(jnp.arange(seq, dtype=F32), MLA_ROPE)
    q, k, v = _mla_up(proj, q_norm, kv_norm, _mla_q_weight(w_uq), w_ukv.astype(BF16),
                      _pad_cols(cos, LANES), _pad_cols(sin, LANES), seq)
    o_a = _attention(q.reshape(bsz, seq, -1), k.reshape(bsz, seq, -1), v.reshape(bsz, seq, -1), MLA_H, MLA_H)

    def lower_bound(tab):
        return jnp.cumsum(jax.nn.softmax(tab.astype(F32), axis=0), axis=0)[j].reshape(1, HG_H * HG_DK)

    o_b = _hgrn2(proj.reshape(bsz, seq, -1), lower_bound(lb_fwd), lower_bound(lb_bwd), out_norm)
    return _out_proj(o_a.reshape(n, -1), o_b.reshape(n, -1), w_out.astype(BF16), x)


def _mixer_odd(x, bsz, seq, norm_g, w_in, sink, q_norm, k_norm, w_out):
    n = x.shape[0]
    proj = _norm_matmul(x, norm_g, _odd_in_weight(w_in), tn=4 * LANES, out_dtype=F32)
    o_c = _window_attention(proj.reshape(bsz, seq, -1), sink.astype(F32), _alibi_slopes(WIN_H))
    pos = jnp.arange(seq)
    half = AX_HD // 2
    c_row, s_row = _rope_tables((pos // GRID_W).astype(F32), half)
    c_col, s_col = _rope_tables((pos % GRID_W).astype(F32), half)
    cos = jnp.concatenate([c_row, c_col], -1)
    sin = jnp.concatenate([s_row, s_col], -1)

    def tables(g):
        g = g.astype(F32)
        return cos * g[None, :], sin * _swap_halves(g, half)[None, :]

    cq, sq = tables(q_norm)
    ck, sk = tables(k_norm)
    q, k, v = _axial_prep(proj, cq, sq, ck, sk, seq)
    o_d = _attention(q.reshape(bsz, seq, -1), k.reshape(bsz, seq, -1), v.reshape(bsz, seq, -1), AX_H, AX_KV)
    return _out_proj(o_c.reshape(n, -1), o_d.reshape(n, -1), w_out.astype(BF16), x)


def _moe(x, norm_g, router, w_gate, w_up, w_down, out_gain, out_splits):
    n = x.shape[0]
    tile = MOE_TM
    route, counts = _router(x, norm_g, router.astype(F32).T)
    counts = counts[0, :N_EXPERTS].astype(jnp.int32)
    padded = (counts + tile - 1) // tile * tile
    ends = jnp.cumsum(padded)
    starts = ends - padded
    e1, e2 = route[:, R_E1].astype(jnp.int32), route[:, R_E2].astype(jnp.int32)
    pos1 = starts[e1] + route[:, R_RANK1].astype(jnp.int32)
    pos2 = starts[e2] + route[:, R_RANK2].astype(jnp.int32)
    n_tiles = 2 * n // tile + N_EXPERTS
    n_valid = ends[-1] // tile
    tile_row = jnp.minimum(jnp.arange(n_tiles), n_valid - 1) * tile
    tile_expert = jnp.sum(tile_row[:, None] >= ends[None, :], axis=1).astype(jnp.int32)
    group_tail = jnp.where(padded > 0, ends - tile, -1)
    spare = (n_valid + jnp.arange(N_EXPERTS)) * tile
    fill_rows = jnp.concatenate([group_tail, jnp.where(spare < n_tiles * tile, spare, -1)]).astype(jnp.int32)
    xs = _moe_scatter(x, pos1, pos2, fill_rows, n_tiles * tile, tile)
    y = _moe_experts(xs, norm_g, w_gate.astype(BF16), w_up.astype(BF16), w_down.astype(BF16),
                     tile_expert, n_valid.reshape(1).astype(jnp.int32), tile)
    final = out_gain is not None
    gain = out_gain if final else jnp.ones((x.shape[1],), F32)
    return [_moe_combine(x, route, pos1, pos2, y, gain, row0, nrows, final) for row0, nrows in out_splits]


def _trunk(x, bsz, seq, norm_mix_e, w_in_e, mla_q_norm, mla_w_uq, mla_kv_norm, mla_w_ukv, hg_lb_fwd, hg_lb_bwd,
           hg_out_norm, w_out_e, norm_ffn_e, ffn_w_gate, ffn_w_up, ffn_w_down, norm_mix_o, w_in_o, win_sink,
           ax_q_norm, ax_k_norm, w_out_o, norm_ffn_o, moe_router, moe_w_gate, moe_w_up, moe_w_down, final_norm,
           out_splits):
    assert DEPTH % 2 == 0
    for l in range(DEPTH):
        j = l // 2
        if l % 2 == 0:
            x = _mixer_even(x, bsz, seq, j, norm_mix_e[j], w_in_e[j], mla_q_norm[j], mla_w_uq[j], mla_kv_norm[j],
                            mla_w_ukv[j], hg_lb_fwd, hg_lb_bwd, hg_out_norm[j], w_out_e[j])
            x = _ffn(x, norm_ffn_e[j], ffn_w_gate[j].astype(BF16), ffn_w_up[j].astype(BF16),
                     ffn_w_down[j].astype(BF16))
        else:
            x = _mixer_odd(x, bsz, seq, norm_mix_o[j], w_in_o[j], win_sink[j], ax_q_norm[j], ax_k_norm[j], w_out_o[j])
            last = l == DEPTH - 1
            outs = _moe(x, norm_ffn_o[j], moe_router[j], moe_w_gate[j], moe_w_up[j], moe_w_down[j],
                        final_norm if last else None, out_splits if last else [(0, x.shape[0])])
            if last:
                return outs
            x = outs[0]


def kernel(x_prompt, x_sample, norm_mix_e, w_in_e, mla_q_norm, mla_w_uq, mla_kv_norm, mla_w_ukv, hg_lb_fwd, hg_lb_bwd, hg_out_norm, w_out_e, norm_ffn_e, ffn_w_gate, ffn_w_up, ffn_w_down, norm_mix_o, w_in_o, win_sink, ax_q_norm, ax_k_norm, w_out_o, norm_ffn_o, moe_router, moe_w_gate, moe_w_up, moe_w_down, final_norm):
    bp, seq, d = x_prompt.shape
    bs = x_sample.shape[0]
    assert x_sample.shape[1:] == (seq, d)
    x = jnp.concatenate([x_prompt.reshape(bp * seq, d), x_sample.reshape(bs * seq, d)], axis=0)
    y_prompt, y_sample = _trunk(
        x, bp + bs, seq, norm_mix_e, w_in_e, mla_q_norm, mla_w_uq, mla_kv_norm, mla_w_ukv, hg_lb_fwd, hg_lb_bwd,
        hg_out_norm, w_out_e, norm_ffn_e, ffn_w_gate, ffn_w_up, ffn_w_down, norm_mix_o, w_in_o, win_sink, ax_q_norm,
        ax_k_norm, w_out_o, norm_ffn_o, moe_router, moe_w_gate, moe_w_up, moe_w_down, final_norm,
        [(0, bp * seq), (bp * seq, bs * seq)])
    return (y_prompt.reshape(bp, seq, d), y_sample.reshape(bs, seq, d))
```

```python
import functools

import jax
import jax.numpy as jnp
import numpy as np
from jax import lax
from jax.experimental import pallas as pl
from jax.experimental.pallas import tpu as pltpu

D_MODEL = 2048
DEPTH = 2
GRID_W = 64
EPS = 1e-6
ROPE_THETA = 10000.0

MLA_H = 8
MLA_NOPE = 128
MLA_ROPE = 64
MLA_V = 128
MLA_LORA = D_MODEL // 4

HG_H = 8
HG_DK = 128
HG_DV = 128

WIN_H = 16
WIN_KV = 2
WIN_HD = 64
WINDOW = 128

AX_H = 8
AX_KV = 2
AX_HD = 128

FF_DENSE = 5632
N_EXPERTS = 8
FF_EXPERT = 1408

LANES = 128
VMEM_CAP = 60000 * 1024
BF16 = jnp.bfloat16
F32 = jnp.float32

E_CQ, E_CKV, E_KRA, E_KRB, E_HQ, E_FF, E_FB, E_HI, E_HG, E_END = 0, 4, 8, 9, 10, 18, 26, 34, 42, 50
O_QD, O_QC, O_KD, O_VD, O_KC, O_VC, O_END = 0, 8, 16, 18, 20, 22, 24


def _cparams(sem, *block_bytes):
    need = int(sum(block_bytes)) + (6 << 20)
    return pltpu.CompilerParams(dimension_semantics=sem, vmem_limit_bytes=min(max(need, 16 << 20), VMEM_CAP))


def _nbytes(shape, dtype):
    return int(np.prod(shape)) * jnp.dtype(dtype).itemsize


def _rms_rows(x, g):
    return x * lax.rsqrt(jnp.mean(x * x, axis=-1, keepdims=True) + EPS) * g


def _norm_into(dst_ref, x_ref, g_ref, chunk=256):
    rows = x_ref.shape[0]
    chunk = min(chunk, rows)

    def body(c, carry):
        r = pl.ds(pl.multiple_of(c * chunk, chunk), chunk)
        dst_ref[r, :] = _rms_rows(x_ref[r, :].astype(F32), g_ref[...]).astype(dst_ref.dtype)
        return carry

    lax.fori_loop(0, rows // chunk, body, 0)


def _row_parts(parts, tm, width, col, **spec_kwargs):
    specs, bounds, start = [], [], 0
    for p in parts:
        nt = p.shape[0] // tm

        def index(i, j, start=start, nt=nt):
            inside = jnp.logical_and(i >= start, i < start + nt)
            return (jnp.clip(i - start, 0, nt - 1), jnp.where(inside, col(j), 0))

        specs.append(pl.BlockSpec((tm, width), index, **spec_kwargs))
        bounds.append((start, start + nt))
        start += nt
    return specs, bounds


def _in_part(i, bound):
    return jnp.logical_and(i >= bound[0], i < bound[1])


def _norm_matmul_kernel(*refs, bounds):
    x_refs = refs[:len(bounds)]
    g_ref, w_ref, o_ref, xn_ref = refs[len(bounds):]
    for x_ref, bound in zip(x_refs, bounds):
        @pl.when(jnp.logical_and(pl.program_id(1) == 0, _in_part(pl.program_id(0), bound)))
        def _():
            _norm_into(xn_ref, x_ref, g_ref)

    o_ref[...] = jnp.dot(xn_ref[...], w_ref[...], preferred_element_type=F32).astype(o_ref.dtype)


def _norm_matmul(parts, g, w, tn, out_dtype, tm=1024):
    k = parts[0].shape[1]
    n = sum(p.shape[0] for p in parts)
    nout = w.shape[1]
    tm = min([tm] + [p.shape[0] for p in parts])
    x_bufs = 2 if len(parts) == 1 else 1
    x_specs, bounds = _row_parts(parts, tm, k, lambda j: 0, pipeline_mode=pl.Buffered(x_bufs))
    return pl.pallas_call(
        functools.partial(_norm_matmul_kernel, bounds=bounds),
        grid=(n // tm, nout // tn),
        in_specs=x_specs + [pl.BlockSpec((1, k), lambda i, j: (0, 0)),
                            pl.BlockSpec((k, tn), lambda i, j: (0, j))],
        out_specs=pl.BlockSpec((tm, tn), lambda i, j: (i, j)),
        out_shape=jax.ShapeDtypeStruct((n, nout), out_dtype),
        scratch_shapes=[pltpu.VMEM((tm, k), BF16)],
        compiler_params=_cparams(("parallel", "arbitrary"), x_bufs * len(parts) * _nbytes((tm, k), F32),
                                 _nbytes((tm, k), BF16), 2 * _nbytes((k, tn), BF16), 3 * _nbytes((tm, tn), F32)),
        name="norm_matmul",
    )(*parts, g.reshape(1, k), w)


def _partner32(x):
    lane = lax.broadcasted_iota(jnp.int32, x.shape, 1)
    return jnp.where((lane & 32) == 0, pltpu.roll(x, LANES - 32, axis=1), pltpu.roll(x, 32, axis=1))


def _mla_up_kernel(cq_ref, ckv_ref, kra_ref, krb_ref, qn_ref, kvn_ref, wq_ref, wkv_ref, cos_ref, sin_ref,
                   q_ref, k_ref, v_ref, cqn_ref, ckvn_ref, *, scale):
    _norm_into(cqn_ref, cq_ref, qn_ref)
    _norm_into(ckvn_ref, ckv_ref, kvn_ref)
    cos, sin = cos_ref[...], sin_ref[...]
    k_rope = (kra_ref[...] * cos + krb_ref[...] * sin).astype(BF16)
    q = jnp.dot(cqn_ref[...], wq_ref[...], preferred_element_type=F32)
    kv = jnp.dot(ckvn_ref[...], wkv_ref[...], preferred_element_type=F32)
    for h in range(MLA_H):
        lo, mid, hi = 2 * h * LANES, (2 * h + 1) * LANES, (2 * h + 2) * LANES
        q_r = q[:, mid:hi]
        q_ref[:, lo:mid] = (q[:, lo:mid] * scale).astype(BF16)
        q_ref[:, mid:hi] = ((q_r * cos + _partner32(q_r) * sin) * scale).astype(BF16)
        k_ref[:, lo:mid] = kv[:, lo:mid].astype(BF16)
        k_ref[:, mid:hi] = k_rope
        v_ref[:, h * LANES:(h + 1) * LANES] = kv[:, mid:hi].astype(BF16)


def _mla_up(proj, qn, kvn, wq, wkv, cos_t, sin_t, seq, tm=512):
    n = proj.shape[0]
    tm = min(tm, seq)
    nt = seq // tm
    lora = MLA_LORA
    width = MLA_H * 2 * LANES
    scale = float((MLA_NOPE + MLA_ROPE) ** -0.5) * LOG2E
    return pl.pallas_call(
        functools.partial(_mla_up_kernel, scale=scale),
        grid=(n // tm,),
        in_specs=[pl.BlockSpec((tm, lora), lambda i: (i, 0)),
                  pl.BlockSpec((tm, lora), lambda i: (i, 1)),
                  pl.BlockSpec((tm, LANES), lambda i: (i, E_KRA)),
                  pl.BlockSpec((tm, LANES), lambda i: (i, E_KRB)),
                  pl.BlockSpec((1, lora), lambda i: (0, 0)),
                  pl.BlockSpec((1, lora), lambda i: (0, 0)),
                  pl.BlockSpec((lora, width), lambda i: (0, 0)),
                  pl.BlockSpec((lora, width), lambda i: (0, 0)),
                  pl.BlockSpec((tm, LANES), lambda i: (i % nt, 0)),
                  pl.BlockSpec((tm, LANES), lambda i: (i % nt, 0))],
        out_specs=[pl.BlockSpec((tm, width), lambda i: (i, 0)),
                   pl.BlockSpec((tm, width), lambda i: (i, 0)),
                   pl.BlockSpec((tm, MLA_H * LANES), lambda i: (i, 0))],
        out_shape=[jax.ShapeDtypeStruct((n, width), BF16),
                   jax.ShapeDtypeStruct((n, width), BF16),
                   jax.ShapeDtypeStruct((n, MLA_H * LANES), BF16)],
        scratch_shapes=[pltpu.VMEM((tm, lora), BF16), pltpu.VMEM((tm, lora), BF16)],
        compiler_params=_cparams(("parallel",), 4 * _nbytes((tm, lora), F32), 12 * _nbytes((tm, LANES), F32),
                                 4 * _nbytes((lora, width), BF16), 2 * _nbytes((tm, 5 * MLA_H * LANES), BF16),
                                 3 * _nbytes((tm, width), F32)),
        name="mla_up",
    )(proj, proj, proj, proj, qn.reshape(1, lora), kvn.reshape(1, lora), wq, wkv, cos_t, sin_t)


ATT_TK = 1024
ATT_SUB = 1024
LOG2E = 1.4426950408889634


def _attention_kernel(q_ref, k_ref, v_ref, o_ref):
    tq = q_ref.shape[1]
    t = k_ref.shape[1]
    for r0 in range(0, tq, ATT_SUB):
        q = q_ref[0, r0:r0 + ATT_SUB, :]
        m = l = acc = None
        for c0 in range(0, t, ATT_TK):
            s = lax.dot_general(q, k_ref[0, c0:c0 + ATT_TK, :], (((1,), (1,)), ((), ())),
                                preferred_element_type=F32)
            m_c = jnp.max(s, axis=-1, keepdims=True)
            m_new = m_c if m is None else jnp.maximum(m, m_c)
            p = jnp.exp2(s - m_new)
            pv = jnp.dot(p.astype(BF16), v_ref[0, c0:c0 + ATT_TK, :], preferred_element_type=F32)
            l_c = jnp.sum(p, axis=-1, keepdims=True)
            if m is None:
                l, acc = l_c, pv
            else:
                alpha = jnp.exp2(m - m_new)
                l, acc = alpha * l + l_c, alpha * acc + pv
            m = m_new
        o_ref[0, r0:r0 + ATT_SUB, :] = (acc / l).astype(o_ref.dtype)


def _attention(q, k, v, heads, kv_heads, tq=1024):
    b, t, _ = q.shape
    dq = q.shape[2] // heads
    dv = v.shape[2] // kv_heads
    g = heads // kv_heads
    tq = min(tq, t)
    return pl.pallas_call(
        _attention_kernel,
        grid=(b, heads, t // tq),
        in_specs=[pl.BlockSpec((1, tq, dq), lambda bi, h, qi: (bi, qi, h)),
                  pl.BlockSpec((1, t, dq), lambda bi, h, qi: (bi, 0, h // g)),
                  pl.BlockSpec((1, t, dv), lambda bi, h, qi: (bi, 0, h // g))],
        out_specs=pl.BlockSpec((1, tq, dv), lambda bi, h, qi: (bi, qi, h)),
        out_shape=jax.ShapeDtypeStruct((b, t, heads * dv), BF16),
        compiler_params=_cparams(("parallel", "parallel", "arbitrary"), 2 * _nbytes((tq, dq), BF16),
                                 2 * _nbytes((t, dq + dv), BF16), 2 * _nbytes((tq, dv), BF16),
                                 6 * _nbytes((ATT_SUB, ATT_TK), F32)),
        name="attention",
    )(q, k, v)


HG_CHUNK = 128
HG_DIRECT_CHUNK = 32
HG_MAX_LOG_RANGE = 80.0


def _split3(x):
    a = x.astype(BF16)
    r = x - a.astype(F32)
    b = r.astype(BF16)
    c = (r - b.astype(F32)).astype(BF16)
    return a, b, c


def _tri_masks(c_sz, reverse):
    row = lax.broadcasted_iota(jnp.int32, (c_sz, c_sz), 0)
    col = lax.broadcasted_iota(jnp.int32, (c_sz, c_sz), 1)
    keep = (row <= col) if reverse else (row >= col)
    return keep, jnp.where(keep, 1.0, 0.0).astype(BF16)


def _hgrn2_kernel(hq_ref, ff_ref, fb_ref, hi_ref, hg_ref, lbf_ref, lbb_ref, on_ref, o_ref,
                  accf_ref, accb_ref, sf_ref, sb_ref, dev_ref):
    t = hq_ref.shape[1]

    def chunk_inputs(f_ref, lb, rows):
        gate = lb + (1.0 - lb) * jax.nn.sigmoid(f_ref[0, rows, :])
        hq = hq_ref[0, rows, :]
        return hq * jax.nn.sigmoid(hq), 1.0 - gate, hi_ref[0, rows, :], jnp.log(gate)

    def log_decay(lf, tri):
        p0, p1, p2 = _split3(lf)
        return (jnp.dot(tri, p0, preferred_element_type=F32) + jnp.dot(tri, p1, preferred_element_type=F32)
                + jnp.dot(tri, p2, preferred_element_type=F32))

    def state_step(s_ref, q, k, v, b, b_last):
        s_t = s_ref[...]
        inter = lax.dot_general((q * jnp.exp(b)).astype(BF16), s_t.astype(BF16), (((1,), (1,)), ((), ())),
                                preferred_element_type=F32)
        kd = (k * jnp.exp(b_last - b)).astype(BF16)
        upd = lax.dot_general(v.astype(BF16), kd, (((0,), (0,)), ((), ())), preferred_element_type=F32)
        s_ref[...] = jnp.exp(b_last) * s_t + upd
        return inter

    def fast_step(f_ref, lb, s_ref, acc_ref, c, masks, reverse):
        c_sz = HG_CHUNK
        keep, tri = masks
        rows = pl.ds(pl.multiple_of(c * c_sz, c_sz), c_sz)
        q, k, v, lf = chunk_inputs(f_ref, lb, rows)
        b = log_decay(lf, tri)
        r = b[c_sz // 2:c_sz // 2 + 1, :]
        b_last = b[0:1, :] if reverse else b[c_sz - 1:c_sz, :]
        b_first = b[c_sz - 1:c_sz, :] if reverse else b[0:1, :]
        dev_ref[...] = jnp.maximum(dev_ref[...], jnp.maximum(jnp.abs(b_first - r), jnp.abs(b_last - r)))
        qe = (q * jnp.exp(b - r)).astype(BF16)
        ke = (k * jnp.exp(r - b)).astype(BF16)
        a = lax.dot_general(qe, ke, (((1,), (1,)), ((), ())), preferred_element_type=F32)
        a = jnp.where(keep, a, 0.0).astype(BF16)
        intra = jnp.dot(a, v.astype(BF16), preferred_element_type=F32)
        acc_ref[rows, :] = intra + state_step(s_ref, q, k, v, b, b_last)

    def direct_step(f_ref, lb, s_ref, acc_ref, c, masks, reverse):
        c_sz = HG_DIRECT_CHUNK
        _, tri = masks
        row1 = lax.broadcasted_iota(jnp.int32, (c_sz, 1), 0)
        rows = pl.ds(pl.multiple_of(c * c_sz, c_sz), c_sz)
        q, k, v, lf = chunk_inputs(f_ref, lb, rows)
        b = log_decay(lf, tri)
        b_last = b[0:1, :] if reverse else b[c_sz - 1:c_sz, :]
        o = state_step(s_ref, q, k, v, b, b_last)
        for s in range(c_sz):
            e = jnp.exp(jnp.minimum(b - b[s:s + 1, :], 0.0))
            a = jnp.sum(q * k[s:s + 1, :] * e, axis=-1, keepdims=True)
            a = jnp.where((row1 <= s) if reverse else (row1 >= s), a, 0.0)
            o = o + a * v[s:s + 1, :]
        acc_ref[rows, :] = o

    def scan_both(step, c_sz):
        nc = t // c_sz
        masks_f = _tri_masks(c_sz, False)
        masks_b = _tri_masks(c_sz, True)
        sf_ref[...] = jnp.zeros_like(sf_ref)
        sb_ref[...] = jnp.zeros_like(sb_ref)

        def body(ci, carry):
            step(ff_ref, lbf_ref[...], sf_ref, accf_ref, ci, masks_f, False)
            step(fb_ref, lbb_ref[...], sb_ref, accb_ref, nc - 1 - ci, masks_b, True)
            return carry

        lax.fori_loop(0, nc, body, 0, unroll=4)

    dev_ref[...] = jnp.zeros_like(dev_ref)
    scan_both(fast_step, HG_CHUNK)

    @pl.when(jnp.logical_not(jnp.max(dev_ref[...]) <= HG_MAX_LOG_RANGE))
    def _():
        scan_both(direct_step, HG_DIRECT_CHUNK)

    def finish(c, carry):
        rows = pl.ds(pl.multiple_of(c * 256, 256), 256)
        hg = hg_ref[0, rows, :]
        y = _rms_rows(accf_ref[rows, :] + accb_ref[rows, :], on_ref[...]) * (hg * jax.nn.sigmoid(hg))
        o_ref[0, rows, :] = y.astype(o_ref.dtype)
        return carry

    lax.fori_loop(0, t // 256, finish, 0)


def _hgrn2(proj, lb_f, lb_b, out_norm):
    b, t, _ = proj.shape

    def col(base):
        return pl.BlockSpec((1, t, LANES), lambda bi, h: (bi, 0, base + h))

    return pl.pallas_call(
        _hgrn2_kernel,
        grid=(b, HG_H),
        in_specs=[col(E_HQ), col(E_FF), col(E_FB), col(E_HI), col(E_HG),
                  pl.BlockSpec((1, LANES), lambda bi, h: (0, h)),
                  pl.BlockSpec((1, LANES), lambda bi, h: (0, h)),
                  pl.BlockSpec((1, LANES), lambda bi, h: (0, 0))],
        out_specs=pl.BlockSpec((1, t, LANES), lambda bi, h: (bi, 0, h)),
        out_shape=jax.ShapeDtypeStruct((b, t, HG_H * HG_DV), BF16),
        scratch_shapes=[pltpu.VMEM((t, HG_DV), F32), pltpu.VMEM((t, HG_DV), F32),
                        pltpu.VMEM((HG_DV, HG_DK), F32), pltpu.VMEM((HG_DV, HG_DK), F32), pltpu.VMEM((1, HG_DK), F32)],
        compiler_params=_cparams(("parallel", "parallel"), 10 * _nbytes((t, LANES), F32), 4 * _nbytes((t, LANES), F32)),
        name="hgrn2",
    )(proj, proj, proj, proj, proj, lb_f, lb_b, out_norm.reshape(1, HG_DV))


def _out_proj_kernel(a_ref, b_ref, wa_ref, wb_ref, *refs, bounds):
    x_refs, o_ref = refs[:-1], refs[-1]
    acc = jnp.dot(a_ref[...], wa_ref[...], preferred_element_type=F32)
    acc = acc + jnp.dot(b_ref[...], wb_ref[...], preferred_element_type=F32)
    for x_ref, bound in zip(x_refs, bounds):
        @pl.when(_in_part(pl.program_id(0), bound))
        def _():
            o_ref[...] = x_ref[...] + acc


def _out_proj(a, b, w, x_parts, tm=1024, tn=512):
    n, ka = a.shape
    d = w.shape[1]
    tm = min([tm] + [p.shape[0] for p in x_parts])
    x_specs, bounds = _row_parts(x_parts, tm, tn, lambda j: j)
    return pl.pallas_call(
        functools.partial(_out_proj_kernel, bounds=bounds),
        grid=(n // tm, d // tn),
        in_specs=[pl.BlockSpec((tm, ka), lambda i, j: (i, 0)),
                  pl.BlockSpec((tm, ka), lambda i, j: (i, 0)),
                  pl.BlockSpec((ka, tn), lambda i, j: (0, j)),
                  pl.BlockSpec((ka, tn), lambda i, j: (1, j))] + x_specs,
        out_specs=pl.BlockSpec((tm, tn), lambda i, j: (i, j)),
        out_shape=jax.ShapeDtypeStruct((n, d), F32),
        compiler_params=_cparams(("parallel", "arbitrary"), 4 * _nbytes((tm, ka), BF16), 4 * _nbytes((ka, tn), BF16),
                                 (3 + 2 * len(x_parts)) * _nbytes((tm, tn), F32)),
        name="out_proj",
    )(a, b, w, w, *x_parts)


def _ffn_kernel(x_ref, g_ref, wg_ref, wu_ref, wd_ref, o_ref, xn_ref):
    f = pl.program_id(1)

    @pl.when(f == 0)
    def _():
        _norm_into(xn_ref, x_ref, g_ref)
        o_ref[...] = x_ref[...]

    xn = xn_ref[...]
    gt = jnp.dot(xn, wg_ref[...], preferred_element_type=F32)
    up = jnp.dot(xn, wu_ref[...], preferred_element_type=F32)
    act = (gt * jax.nn.sigmoid(gt) * up).astype(BF16)
    o_ref[...] += jnp.dot(act, wd_ref[...], preferred_element_type=F32)


def _ffn(x, g, wg, wu, wd, tm=1024, tf=512):
    n, d = x.shape
    ff = wg.shape[1]
    tm = min(tm, n)
    return pl.pallas_call(
        _ffn_kernel,
        grid=(n // tm, ff // tf),
        in_specs=[pl.BlockSpec((tm, d), lambda i, f: (i, 0), pipeline_mode=pl.Buffered(1)),
                  pl.BlockSpec((1, d), lambda i, f: (0, 0)),
                  pl.BlockSpec((d, tf), lambda i, f: (0, f)),
                  pl.BlockSpec((d, tf), lambda i, f: (0, f)),
                  pl.BlockSpec((tf, d), lambda i, f: (f, 0))],
        out_specs=pl.BlockSpec((tm, d), lambda i, f: (i, 0)),
        out_shape=jax.ShapeDtypeStruct((n, d), F32),
        scratch_shapes=[pltpu.VMEM((tm, d), BF16)],
        compiler_params=_cparams(("parallel", "arbitrary"), 3 * _nbytes((tm, d), F32), _nbytes((tm, d), BF16),
                                 6 * _nbytes((d, tf), BF16), 4 * _nbytes((tm, tf), F32)),
        name="ffn",
    )(x, g.reshape(1, d), wg, wu, wd)


def _axial_prep_kernel(qd_ref, kd_ref, vd_ref, cq_ref, sq_ref, ck_ref, sk_ref, q_ref, k_ref, v_ref, *, scale):
    def rope(x_ref, c_ref, s_ref, h, mul):
        x = x_ref[:, h * LANES:(h + 1) * LANES]
        r = lax.rsqrt(jnp.mean(x * x, axis=-1, keepdims=True) + EPS)
        return ((x * c_ref[...] + _partner32(x) * s_ref[...]) * (r * mul)).astype(BF16)

    for h in range(AX_H):
        q_ref[:, h * LANES:(h + 1) * LANES] = rope(qd_ref, cq_ref, sq_ref, h, scale)
    for h in range(AX_KV):
        k_ref[:, h * LANES:(h + 1) * LANES] = rope(kd_ref, ck_ref, sk_ref, h, 1.0)
    v_ref[...] = vd_ref[...].astype(BF16)


def _axial_prep(proj, cq, sq, ck, sk, seq, tm=512):
    n = proj.shape[0]
    tm = min(tm, seq)
    nt = seq // tm
    qw, kw = AX_H * AX_HD, AX_KV * AX_HD
    tab = pl.BlockSpec((tm, LANES), lambda i: (i % nt, 0))
    return pl.pallas_call(
        functools.partial(_axial_prep_kernel, scale=float(AX_HD ** -0.5) * LOG2E),
        grid=(n // tm,),
        in_specs=[pl.BlockSpec((tm, qw), lambda i: (i, O_QD * LANES // qw)),
                  pl.BlockSpec((tm, kw), lambda i: (i, O_KD * LANES // kw)),
                  pl.BlockSpec((tm, kw), lambda i: (i, O_VD * LANES // kw)),
                  tab, tab, tab, tab],
        out_specs=[pl.BlockSpec((tm, qw), lambda i: (i, 0)),
                   pl.BlockSpec((tm, kw), lambda i: (i, 0)),
                   pl.BlockSpec((tm, kw), lambda i: (i, 0))],
        out_shape=[jax.ShapeDtypeStruct((n, qw), BF16), jax.ShapeDtypeStruct((n, kw), BF16),
                   jax.ShapeDtypeStruct((n, kw), BF16)],
        compiler_params=_cparams(("parallel",), 4 * _nbytes((tm, qw), F32), 6 * _nbytes((tm, kw), F32),
                                 8 * _nbytes((tm, LANES), F32), 2 * _nbytes((tm, qw + 2 * kw), BF16)),
        name="axial_prep",
    )(proj, proj, proj, cq, sq, ck, sk)


WIN_BLOCK = 128


def _window_kernel(sink_ref, slope_ref, q_ref, k_ref, v_ref, o_ref, *, scale):
    t = q_ref.shape[1]
    wb = WIN_BLOCK
    span = 3 * wb
    pair = pl.program_id(1)
    low = lax.broadcasted_iota(jnp.int32, (wb, LANES), 1) < WIN_HD
    row = lax.broadcasted_iota(jnp.int32, (2 * wb, span), 0)
    delta = (row & (wb - 1)) - lax.broadcasted_iota(jnp.int32, (2 * wb, span), 1)
    top = lax.broadcasted_iota(jnp.int32, (2 * wb, 1), 0) < wb
    slope = jnp.where(top, slope_ref[2 * pair], slope_ref[2 * pair + 1]) * LOG2E
    sink = jnp.where(top, sink_ref[2 * pair], sink_ref[2 * pair + 1]) * LOG2E

    def body(qb, carry):
        start = pl.multiple_of(jnp.clip((qb - 1) * wb, 0, t - span), wb)
        kwin = k_ref[0, pl.ds(start, span), :].astype(BF16)
        vwin = v_ref[0, pl.ds(start, span), :].astype(BF16)
        qrows = pl.ds(pl.multiple_of(qb * wb, wb), wb)
        q2 = q_ref[0, qrows, :] * (scale * LOG2E)
        qs = jnp.concatenate([jnp.where(low, q2, 0.0), jnp.where(low, 0.0, q2)], axis=0).astype(BF16)
        dist = jnp.abs(delta + (qb * wb - start))
        s = lax.dot_general(qs, kwin, (((1,), (1,)), ((), ())), preferred_element_type=F32)
        s = jnp.where(dist <= WINDOW, s - slope * dist.astype(F32), -jnp.inf)
        m = jnp.maximum(jnp.max(s, axis=-1, keepdims=True), sink)
        e = jnp.exp2(s - m)
        den = jnp.sum(e, axis=-1, keepdims=True) + jnp.exp2(sink - m)
        o = jnp.dot(e.astype(BF16), vwin, preferred_element_type=F32) / den
        o_ref[0, qrows, :] = jnp.where(low, o[:wb], o[wb:]).astype(o_ref.dtype)
        return carry

    lax.fori_loop(0, t // wb, body, 0, unroll=8)


def _window_attention(proj, sink, slopes):
    b, t, _ = proj.shape
    pairs = WIN_H // 2
    per_kv = pairs // WIN_KV
    smem = pl.BlockSpec(memory_space=pltpu.SMEM)
    return pl.pallas_call(
        functools.partial(_window_kernel, scale=float(WIN_HD ** -0.5)),
        grid=(b, pairs),
        in_specs=[smem, smem,
                  pl.BlockSpec((1, t, LANES), lambda bi, p: (bi, 0, O_QC + p)),
                  pl.BlockSpec((1, t, LANES), lambda bi, p: (bi, 0, O_KC + p // per_kv)),
                  pl.BlockSpec((1, t, LANES), lambda bi, p: (bi, 0, O_VC + p // per_kv))],
        out_specs=pl.BlockSpec((1, t, LANES), lambda bi, p: (bi, 0, p)),
        out_shape=jax.ShapeDtypeStruct((b, t, WIN_H * WIN_HD), BF16),
        compiler_params=_cparams(("parallel", "parallel"), 6 * _nbytes((t, LANES), F32), 2 * _nbytes((t, LANES), BF16)),
        name="window_attention",
    )(sink, slopes, proj, proj, proj)


MOE_TM = 512
R_E1, R_E2, R_W1, R_W2, R_RANK1, R_RANK2 = range(6)


def _router_kernel(x_ref, g_ref, r_ref, route_ref, cnt_ref, carry_ref):
    @pl.when(pl.program_id(0) == 0)
    def _():
        carry_ref[...] = jnp.zeros_like(carry_ref)

    tm = x_ref.shape[0]
    xn = _rms_rows(x_ref[...], g_ref[...])
    logits = [jnp.sum(xn * r_ref[e:e + 1, :], axis=-1, keepdims=True) for e in range(N_EXPERTS)]

    def top(ls):
        m = functools.reduce(jnp.maximum, ls)
        idx = jnp.full_like(m, N_EXPERTS).astype(jnp.int32)
        for e in reversed(range(N_EXPERTS)):
            idx = jnp.where(ls[e] == m, e, idx)
        return m, idx

    m1, i1 = top(logits)
    m2, i2 = top([jnp.where(i1 == e, -jnp.inf, logits[e]) for e in range(N_EXPERTS)])
    e2 = jnp.exp(m2 - m1)
    w1 = 1.0 / (1.0 + e2)
    w2 = e2 / (1.0 + e2)
    lane = lax.broadcasted_iota(jnp.int32, route_ref.shape, 1)
    sel1, sel2 = lane == i1, lane == i2
    onehot = jnp.where(jnp.logical_or(sel1, sel2), 1.0, 0.0)
    row = lax.broadcasted_iota(jnp.int32, (tm, tm), 0)
    col = lax.broadcasted_iota(jnp.int32, (tm, tm), 1)
    earlier = jnp.where(row > col, 1.0, 0.0).astype(BF16)
    before = carry_ref[...] + jnp.dot(earlier, onehot.astype(BF16), preferred_element_type=F32)
    rank1 = jnp.sum(jnp.where(sel1, before, 0.0), axis=-1, keepdims=True)
    rank2 = jnp.sum(jnp.where(sel2, before, 0.0), axis=-1, keepdims=True)
    carry_ref[...] = carry_ref[...] + jnp.sum(onehot, axis=0, keepdims=True)
    cnt_ref[...] = carry_ref[...]
    rec = jnp.zeros(route_ref.shape, F32)
    for pos, val in ((R_E1, i1.astype(F32)), (R_E2, i2.astype(F32)), (R_W1, w1), (R_W2, w2),
                     (R_RANK1, rank1), (R_RANK2, rank2)):
        rec = jnp.where(lane == pos, val, rec)
    route_ref[...] = rec


def _router(x, g, router_t, tm=256):
    n, d = x.shape
    tm = min(tm, n)
    return pl.pallas_call(
        _router_kernel,
        grid=(n // tm,),
        in_specs=[pl.BlockSpec((tm, d), lambda i: (i, 0)),
                  pl.BlockSpec((1, d), lambda i: (0, 0)),
                  pl.BlockSpec((N_EXPERTS, d), lambda i: (0, 0))],
        out_specs=[pl.BlockSpec((tm, LANES), lambda i: (i, 0)),
                   pl.BlockSpec((1, LANES), lambda i: (0, 0))],
        out_shape=[jax.ShapeDtypeStruct((n, LANES), F32), jax.ShapeDtypeStruct((1, LANES), F32)],
        scratch_shapes=[pltpu.VMEM((1, LANES), F32)],
        compiler_params=_cparams(("arbitrary",), 6 * _nbytes((tm, d), F32)),
        name="router",
    )(x, g.reshape(1, d), router_t)


def _row_copy(src_hbm, src_row, dst, dst_row, sem):
    return pltpu.make_async_copy(src_hbm.at[pl.ds(src_row, 1)], dst.at[pl.ds(dst_row, 1)], sem)


def _moe_scatter_kernel(fill_ref, p1_ref, p2_ref, x_ref, xs_hbm, zero_ref, sem, zsem, *, tile):
    i = pl.program_id(0)
    rows = p1_ref.shape[-1]

    @pl.when(i == 0)
    def _():
        zero_ref[...] = jnp.zeros_like(zero_ref)
        for k in range(fill_ref.shape[0]):
            @pl.when(fill_ref[k] >= 0)
            def _():
                start = pl.multiple_of(fill_ref[k], tile)
                fill = pltpu.make_async_copy(zero_ref, xs_hbm.at[pl.ds(start, tile)], zsem)
                fill.start()
                fill.wait()

    def issue(r, carry):
        _row_copy(x_ref, r, xs_hbm, p1_ref[0, 0, r], sem).start()
        _row_copy(x_ref, r, xs_hbm, p2_ref[0, 0, r], sem).start()
        return carry

    lax.fori_loop(0, rows, issue, 0, unroll=8)
    for _ in range(2):
        pltpu.make_async_copy(x_ref, xs_hbm.at[pl.ds(0, rows)], sem).wait()


def _moe_scatter(x, pos1, pos2, fill_rows, total_rows, tile, rows=256):
    n, d = x.shape
    rows = min(rows, n)
    smem_blk = pl.BlockSpec((1, 1, rows), lambda i, fill: (i, 0, 0), memory_space=pltpu.SMEM)
    return pl.pallas_call(
        functools.partial(_moe_scatter_kernel, tile=tile),
        grid_spec=pltpu.PrefetchScalarGridSpec(
            num_scalar_prefetch=1,
            grid=(n // rows,),
            in_specs=[smem_blk, smem_blk, pl.BlockSpec((rows, d), lambda i, fill: (i, 0))],
            out_specs=pl.BlockSpec(memory_space=pl.ANY),
            scratch_shapes=[pltpu.VMEM((tile, d), x.dtype), pltpu.SemaphoreType.DMA, pltpu.SemaphoreType.DMA]),
        out_shape=jax.ShapeDtypeStruct((total_rows, d), x.dtype),
        compiler_params=_cparams(("arbitrary",), _nbytes((tile, d), x.dtype), 2 * _nbytes((rows, d), x.dtype)),
        name="moe_scatter",
    )(fill_rows, pos1.reshape(n // rows, 1, rows), pos2.reshape(n // rows, 1, rows), x)


def _moe_experts_kernel(te_ref, nv_ref, xs_ref, g_ref, wg_ref, wu_ref, wd_ref, y_ref, xn_ref):
    valid = pl.program_id(0) < nv_ref[0]

    @pl.when(valid)
    def _():
        _norm_into(xn_ref, xs_ref, g_ref)
        xn = xn_ref[...]
        gt = jnp.dot(xn, wg_ref[0], preferred_element_type=F32)
        up = jnp.dot(xn, wu_ref[0], preferred_element_type=F32)
        act = (gt * jax.nn.sigmoid(gt) * up).astype(BF16)
        y_ref[...] = jnp.dot(act, wd_ref[0], preferred_element_type=F32)

    @pl.when(jnp.logical_not(valid))
    def _():
        y_ref[...] = jnp.zeros_like(y_ref)


def _moe_experts(xs, g, wg, wu, wd, tile_expert, n_valid, tile):
    rows, d = xs.shape
    ff = wg.shape[2]
    once = dict(pipeline_mode=pl.Buffered(1))

    def row_blk(i, te, nv):
        return (jnp.minimum(i, nv[0] - 1), 0)

    def w_blk(i, te, nv):
        return (te[i], 0, 0)

    return pl.pallas_call(
        _moe_experts_kernel,
        grid_spec=pltpu.PrefetchScalarGridSpec(
            num_scalar_prefetch=2,
            grid=(rows // tile,),
            in_specs=[pl.BlockSpec((tile, d), row_blk),
                      pl.BlockSpec((1, d), lambda i, te, nv: (0, 0)),
                      pl.BlockSpec((1, d, ff), w_blk, **once),
                      pl.BlockSpec((1, d, ff), w_blk, **once),
                      pl.BlockSpec((1, ff, d), w_blk, **once)],
            out_specs=pl.BlockSpec((tile, d), lambda i, te, nv: (i, 0)),
            scratch_shapes=[pltpu.VMEM((tile, d), BF16)]),
        out_shape=jax.ShapeDtypeStruct((rows, d), F32),
        compiler_params=_cparams(("arbitrary",), 4 * _nbytes((tile, d), F32), _nbytes((tile, d), BF16),
                                 3 * _nbytes((d, ff), BF16), 4 * _nbytes((tile, ff), F32), _nbytes((tile, d), F32)),
        name="moe_experts",
    )(tile_expert, n_valid, xs, g.reshape(1, d), wg, wu, wd)


def _moe_combine_kernel(p1_ref, p2_ref, x_ref, route_ref, g_ref, y_hbm, o_ref, ya_ref, yb_ref, sem, *, final):
    rows = x_ref.shape[0]

    def issue(r, carry):
        _row_copy(y_hbm, p1_ref[0, 0, r], ya_ref, r, sem).start()
        _row_copy(y_hbm, p2_ref[0, 0, r], yb_ref, r, sem).start()
        return carry

    lax.fori_loop(0, rows, issue, 0, unroll=8)
    for dst in (ya_ref, yb_ref):
        pltpu.make_async_copy(y_hbm.at[pl.ds(0, rows)], dst, sem).wait()
    lane = lax.broadcasted_iota(jnp.int32, route_ref.shape, 1)
    w1 = jnp.sum(jnp.where(lane == R_W1, route_ref[...], 0.0), axis=-1, keepdims=True)
    w2 = jnp.sum(jnp.where(lane == R_W2, route_ref[...], 0.0), axis=-1, keepdims=True)
    out = x_ref[...] + w1 * ya_ref[...] + w2 * yb_ref[...]
    o_ref[...] = _rms_rows(out, g_ref[...]) if final else out


def _moe_combine(x, route, pos1, pos2, y, gain, row0, nrows, final, rows=256):
    n, d = x.shape
    rows = min(rows, nrows)
    off = row0 // rows
    smem_blk = pl.BlockSpec((1, 1, rows), lambda i: (i + off, 0, 0), memory_space=pltpu.SMEM)
    return pl.pallas_call(
        functools.partial(_moe_combine_kernel, final=final),
        grid=(nrows // rows,),
        in_specs=[smem_blk, smem_blk,
                  pl.BlockSpec((rows, d), lambda i: (i + off, 0)),
                  pl.BlockSpec((rows, LANES), lambda i: (i + off, 0)),
                  pl.BlockSpec((1, d), lambda i: (0, 0)),
                  pl.BlockSpec(memory_space=pl.ANY)],
        out_specs=pl.BlockSpec((rows, d), lambda i: (i, 0)),
        out_shape=jax.ShapeDtypeStruct((nrows, d), F32),
        scratch_shapes=[pltpu.VMEM((rows, d), F32), pltpu.VMEM((rows, d), F32), pltpu.SemaphoreType.DMA],
        compiler_params=_cparams(("arbitrary",), 8 * _nbytes((rows, d), F32)),
        name="moe_combine",
    )(pos1.reshape(n // rows, 1, rows), pos2.reshape(n // rows, 1, rows), x, route, gain.reshape(1, d), y)


def _swap_halves(w, width):
    lead = w.shape[:-1]
    return jnp.flip(w.reshape(lead + (-1, 2, width // 2)), axis=-2).reshape(w.shape)


def _pad_cols(w, width):
    return jnp.pad(w, [(0, 0)] * (w.ndim - 1) + [(0, width - w.shape[-1])])


def _even_in_weight(w):
    kr = w[:, 2 * MLA_LORA:2 * MLA_LORA + MLA_ROPE]
    return jnp.concatenate([w[:, :2 * MLA_LORA], _pad_cols(kr, LANES), _pad_cols(_swap_halves(kr, MLA_ROPE), LANES),
                            w[:, 2 * MLA_LORA + MLA_ROPE:]], axis=1).astype(BF16)


def _mla_q_weight(w):
    w = w.reshape(MLA_LORA, MLA_H, MLA_NOPE + MLA_ROPE)
    rope = w[..., MLA_NOPE:]
    w = jnp.concatenate([w[..., :MLA_NOPE], _pad_cols(rope, LANES)], -1)
    return w.reshape(MLA_LORA, MLA_H * 2 * LANES).astype(BF16)


def _odd_in_weight(w):
    d = w.shape[0]
    o = 0
    parts = {}
    for name, width in (("qc", WIN_H * WIN_HD), ("kc", WIN_KV * WIN_HD), ("vc", WIN_KV * WIN_HD),
                        ("qd", AX_H * AX_HD), ("kd", AX_KV * AX_HD), ("vd", AX_KV * AX_HD)):
        parts[name] = w[:, o:o + width]
        o += width

    def dup(x):
        x = x.reshape(d, WIN_KV, WIN_HD)
        return jnp.concatenate([x, x], axis=-1).reshape(d, WIN_KV * LANES)

    out = jnp.concatenate([parts["qd"], parts["qc"], parts["kd"], parts["vd"], dup(parts["kc"]), dup(parts["vc"])],
                          axis=1)
    assert out.shape[1] == O_END * LANES
    return out.astype(BF16)


def _rope_tables(pos, dim):
    inv = ROPE_THETA ** (-jnp.arange(0, dim, 2, dtype=F32) / dim)
    ang = pos[:, None] * inv[None, :]
    cos, sin = jnp.cos(ang), jnp.sin(ang)
    return jnp.concatenate([cos, cos], -1), jnp.concatenate([-sin, sin], -1)


def _alibi_slopes(n):
    return jnp.asarray(2.0 ** (-8.0 * np.arange(1, n + 1) / n), dtype=F32)


def _mixer_even(x_parts, bsz, seq, j, norm_g, w_in, q_norm, w_uq, kv_norm, w_ukv, lb_fwd, lb_bwd, out_norm, w_out):
    n = bsz * seq
    proj = _norm_matmul(x_parts, norm_g, _even_in_weight(w_in), tn=10 * LANES, out_dtype=F32)
    cos, sin = _rope_tables(jnp.arange(seq, dtype=F32), MLA_ROPE)
    q, k, v = _mla_up(proj, q_norm, kv_norm, _mla_q_weight(w_uq), w_ukv.astype(BF16),
                      _pad_cols(cos, LANES), _pad_cols(sin, LANES), seq)
    o_a = _attention(q.reshape(bsz, seq, -1), k.reshape(bsz, seq, -1), v.reshape(bsz, seq, -1), MLA_H, MLA_H)

    def lower_bound(tab):
        return jnp.cumsum(jax.nn.softmax(tab.astype(F32), axis=0), axis=0)[j].reshape(1, HG_H * HG_DK)

    o_b = _hgrn2(proj.reshape(bsz, seq, -1), lower_bound(lb_fwd), lower_bound(lb_bwd), out_norm)
    return _out_proj(o_a.reshape(n, -1), o_b.reshape(n, -1), w_out.astype(BF16), x_parts)


def _mixer_odd(x_parts, bsz, seq, norm_g, w_in, sink, q_norm, k_norm, w_out):
    n = bsz * seq
    proj = _norm_matmul(x_parts, norm_g, _odd_in_weight(w_in), tn=4 * LANES, out_dtype=F32)
    o_c = _window_attention(proj.reshape(bsz, seq, -1), sink.astype(F32), _alibi_slopes(WIN_H))
    pos = jnp.arange(seq)
    half = AX_HD // 2
    c_row, s_row = _rope_tables((pos // GRID_W).astype(F32), half)
    c_col, s_col = _rope_tables((pos % GRID_W).astype(F32), half)
    cos = jnp.concatenate([c_row, c_col], -1)
    sin = jnp.concatenate([s_row, s_col], -1)

    def tables(g):
        g = g.astype(F32)
        return cos * g[None, :], sin * _swap_halves(g, half)[None, :]

    cq, sq = tables(q_norm)
    ck, sk = tables(k_norm)
    q, k, v = _axial_prep(proj, cq, sq, ck, sk, seq)
    o_d = _attention(q.reshape(bsz, seq, -1), k.reshape(bsz, seq, -1), v.reshape(bsz, seq, -1), AX_H, AX_KV)
    return _out_proj(o_c.reshape(n, -1), o_d.reshape(n, -1), w_out.astype(BF16), x_parts)


def _moe(x, norm_g, router, w_gate, w_up, w_down, out_gain, out_splits):
    n = x.shape[0]
    tile = MOE_TM
    route, counts = _router(x, norm_g, router.astype(F32).T)
    counts = counts[0, :N_EXPERTS].astype(jnp.int32)
    padded = (counts + tile - 1) // tile * tile
    ends = jnp.cumsum(padded)
    starts = ends - padded
    e1, e2 = route[:, R_E1].astype(jnp.int32), route[:, R_E2].astype(jnp.int32)
    pos1 = starts[e1] + route[:, R_RANK1].astype(jnp.int32)
    pos2 = starts[e2] + route[:, R_RANK2].astype(jnp.int32)
    n_tiles = 2 * n // tile + N_EXPERTS
    n_valid = ends[-1] // tile
    tile_row = jnp.minimum(jnp.arange(n_tiles), n_valid - 1) * tile
    tile_expert = jnp.sum(tile_row[:, None] >= ends[None, :], axis=1).astype(jnp.int32)
    group_tail = jnp.where(padded > 0, ends - tile, -1)
    spare = (n_valid + jnp.arange(N_EXPERTS)) * tile
    fill_rows = jnp.concatenate([group_tail, jnp.where(spare < n_tiles * tile, spare, -1)]).astype(jnp.int32)
    xs = _moe_scatter(x, pos1, pos2, fill_rows, n_tiles * tile, tile)
    y = _moe_experts(xs, norm_g, w_gate.astype(BF16), w_up.astype(BF16), w_down.astype(BF16),
                     tile_expert, n_valid.reshape(1).astype(jnp.int32), tile)
    final = out_gain is not None
    gain = out_gain if final else jnp.ones((x.shape[1],), F32)
    return [_moe_combine(x, route, pos1, pos2, y, gain, row0, nrows, final) for row0, nrows in out_splits]


def _trunk(x_parts, bsz, seq, norm_mix_e, w_in_e, mla_q_norm, mla_w_uq, mla_kv_norm, mla_w_ukv, hg_lb_fwd, hg_lb_bwd,
           hg_out_norm, w_out_e, norm_ffn_e, ffn_w_gate, ffn_w_up, ffn_w_down, norm_mix_o, w_in_o, win_sink,
           ax_q_norm, ax_k_norm, w_out_o, norm_ffn_o, moe_router, moe_w_gate, moe_w_up, moe_w_down, final_norm,
           out_splits):
    assert DEPTH % 2 == 0
    for l in range(DEPTH):
        j = l // 2
        if l % 2 == 0:
            x = _mixer_even(x_parts, bsz, seq, j, norm_mix_e[j], w_in_e[j], mla_q_norm[j], mla_w_uq[j],
                            mla_kv_norm[j], mla_w_ukv[j], hg_lb_fwd, hg_lb_bwd, hg_out_norm[j], w_out_e[j])
            x = _ffn(x, norm_ffn_e[j], ffn_w_gate[j].astype(BF16), ffn_w_up[j].astype(BF16),
                     ffn_w_down[j].astype(BF16))
        else:
            x = _mixer_odd(x_parts, bsz, seq, norm_mix_o[j], w_in_o[j], win_sink[j], ax_q_norm[j], ax_k_norm[j],
                           w_out_o[j])
            last = l == DEPTH - 1
            outs = _moe(x, norm_ffn_o[j], moe_router[j], moe_w_gate[j], moe_w_up[j], moe_w_down[j],
                        final_norm if last else None, out_splits if last else [(0, x.shape[0])])
            if last:
                return outs
            x = outs[0]
        x_parts = (x,)


def kernel(x_prompt, x_sample, norm_mix_e, w_in_e, mla_q_norm, mla_w_uq, mla_kv_norm, mla_w_ukv, hg_lb_fwd, hg_lb_bwd, hg_out_norm, w_out_e, norm_ffn_e, ffn_w_gate, ffn_w_up, ffn_w_down, norm_mix_o, w_in_o, win_sink, ax_q_norm, ax_k_norm, w_out_o, norm_ffn_o, moe_router, moe_w_gate, moe_w_up, moe_w_down, final_norm):
    bp, seq, d = x_prompt.shape
    bs = x_sample.shape[0]
    assert x_sample.shape[1:] == (seq, d)
    y_prompt, y_sample = _trunk(
        (x_prompt.reshape(bp * seq, d), x_sample.reshape(bs * seq, d)), bp + bs, seq, norm_mix_e, w_in_e, mla_q_norm, mla_w_uq, mla_kv_norm, mla_w_ukv, hg_lb_fwd, hg_lb_bwd,
        hg_out_norm, w_out_e, norm_ffn_e, ffn_w_gate, ffn_w_up, ffn_w_down, norm_mix_o, w_in_o, win_sink, ax_q_norm,
        ax_k_norm, w_out_o, norm_ffn_o, moe_router, moe_w_gate, moe_w_up, moe_w_down, final_norm,
        [(0, bp * seq), (bp * seq, bs * seq)])
    return (y_prompt.reshape(bp, seq, d), y_sample.reshape(bs, seq, d))
```

```python
import functools

import jax
import jax.numpy as jnp
import numpy as np
from jax import lax
from jax.experimental import pallas as pl
from jax.experimental.pallas import tpu as pltpu

D_MODEL = 2048
DEPTH = 2
GRID_W = 64
EPS = 1e-6
ROPE_THETA = 10000.0

MLA_H = 8
MLA_NOPE = 128
MLA_ROPE = 64
MLA_V = 128
MLA_LORA = D_MODEL // 4

HG_H = 8
HG_DK = 128
HG_DV = 128

WIN_H = 16
WIN_KV = 2
WIN_HD = 64
WINDOW = 128

AX_H = 8
AX_KV = 2
AX_HD = 128

FF_DENSE = 5632
N_EXPERTS = 8
FF_EXPERT = 1408

LANES = 128
VMEM_CAP = 60000 * 1024
BF16 = jnp.bfloat16
F32 = jnp.float32

E_CQ, E_CKV, E_KRA, E_KRB, E_HQ, E_FF, E_FB, E_HI, E_HG, E_END = 0, 4, 8, 9, 10, 18, 26, 34, 42, 50
O_QD, O_QC, O_KD, O_VD, O_KC, O_VC, O_END = 0, 8, 16, 18, 20, 22, 24


def _cparams(sem, *block_bytes):
    need = int(sum(block_bytes)) + (6 << 20)
    return pltpu.CompilerParams(dimension_semantics=sem, vmem_limit_bytes=min(max(need, 16 << 20), VMEM_CAP))


def _nbytes(shape, dtype):
    return int(np.prod(shape)) * jnp.dtype(dtype).itemsize


def _rms_rows(x, g):
    return x * lax.rsqrt(jnp.mean(x * x, axis=-1, keepdims=True) + EPS) * g


def _norm_into(dst_ref, x_ref, g_ref, chunk=256):
    rows = x_ref.shape[0]
    chunk = min(chunk, rows)

    def body(c, carry):
        r = pl.ds(pl.multiple_of(c * chunk, chunk), chunk)
        dst_ref[r, :] = _rms_rows(x_ref[r, :].astype(F32), g_ref[...]).astype(dst_ref.dtype)
        return carry

    lax.fori_loop(0, rows // chunk, body, 0)


def _row_parts(parts, tm, width, col):
    specs, bounds, start = [], [], 0
    for p in parts:
        nt = p.shape[0] // tm

        def index(i, j, start=start, nt=nt):
            inside = jnp.logical_and(i >= start, i < start + nt)
            return (jnp.clip(i - start, 0, nt - 1), jnp.where(inside, col(j), 0))

        specs.append(pl.BlockSpec((tm, width), index))
        bounds.append((start, start + nt))
        start += nt
    return specs, bounds


def _in_part(i, bound):
    return jnp.logical_and(i >= bound[0], i < bound[1])


def _norm_matmul_kernel(*refs, bounds):
    x_refs = refs[:len(bounds)]
    g_ref, w_ref, o_ref, xn_ref = refs[len(bounds):]
    for x_ref, bound in zip(x_refs, bounds):
        @pl.when(jnp.logical_and(pl.program_id(1) == 0, _in_part(pl.program_id(0), bound)))
        def _():
            _norm_into(xn_ref, x_ref, g_ref)

    o_ref[...] = jnp.dot(xn_ref[...], w_ref[...], preferred_element_type=F32).astype(o_ref.dtype)


def _norm_matmul(parts, g, w, tn, out_dtype):
    k = parts[0].shape[1]
    n = sum(p.shape[0] for p in parts)
    nout = w.shape[1]

    def resident(tm):
        return (2 * len(parts) * _nbytes((tm, k), F32), _nbytes((tm, k), BF16), 2 * _nbytes((k, tn), BF16),
                3 * _nbytes((tm, tn), F32))

    rows = min(p.shape[0] for p in parts)
    tm = next(t for t in (1024, 512, 256) if t <= rows and sum(resident(t)) + (8 << 20) <= VMEM_CAP)
    x_specs, bounds = _row_parts(parts, tm, k, lambda j: 0)
    return pl.pallas_call(
        functools.partial(_norm_matmul_kernel, bounds=bounds),
        grid=(n // tm, nout // tn),
        in_specs=x_specs + [pl.BlockSpec((1, k), lambda i, j: (0, 0)),
                            pl.BlockSpec((k, tn), lambda i, j: (0, j))],
        out_specs=pl.BlockSpec((tm, tn), lambda i, j: (i, j)),
        out_shape=jax.ShapeDtypeStruct((n, nout), out_dtype),
        scratch_shapes=[pltpu.VMEM((tm, k), BF16)],
        compiler_params=_cparams(("parallel", "arbitrary"), *resident(tm)),
        name="norm_matmul",
    )(*parts, g.reshape(1, k), w)


def _partner32(x):
    lane = lax.broadcasted_iota(jnp.int32, x.shape, 1)
    return jnp.where((lane & 32) == 0, pltpu.roll(x, LANES - 32, axis=1), pltpu.roll(x, 32, axis=1))


def _mla_up_kernel(cq_ref, ckv_ref, kra_ref, krb_ref, qn_ref, kvn_ref, wq_ref, wkv_ref, cos_ref, sin_ref,
                   q_ref, k_ref, v_ref, cqn_ref, ckvn_ref, *, scale):
    _norm_into(cqn_ref, cq_ref, qn_ref)
    _norm_into(ckvn_ref, ckv_ref, kvn_ref)
    cos, sin = cos_ref[...], sin_ref[...]
    k_rope = (kra_ref[...] * cos + krb_ref[...] * sin).astype(BF16)
    q = jnp.dot(cqn_ref[...], wq_ref[...], preferred_element_type=F32)
    kv = jnp.dot(ckvn_ref[...], wkv_ref[...], preferred_element_type=F32)
    for h in range(MLA_H):
        lo, mid, hi = 2 * h * LANES, (2 * h + 1) * LANES, (2 * h + 2) * LANES
        q_r = q[:, mid:hi]
        q_ref[:, lo:mid] = (q[:, lo:mid] * scale).astype(BF16)
        q_ref[:, mid:hi] = ((q_r * cos + _partner32(q_r) * sin) * scale).astype(BF16)
        k_ref[:, lo:mid] = kv[:, lo:mid].astype(BF16)
        k_ref[:, mid:hi] = k_rope
        v_ref[:, h * LANES:(h + 1) * LANES] = kv[:, mid:hi].astype(BF16)


def _mla_up(proj, qn, kvn, wq, wkv, cos_t, sin_t, seq, tm=512):
    n = proj.shape[0]
    tm = min(tm, seq)
    nt = seq // tm
    lora = MLA_LORA
    width = MLA_H * 2 * LANES
    scale = float((MLA_NOPE + MLA_ROPE) ** -0.5) * LOG2E
    return pl.pallas_call(
        functools.partial(_mla_up_kernel, scale=scale),
        grid=(n // tm,),
        in_specs=[pl.BlockSpec((tm, lora), lambda i: (i, 0)),
                  pl.BlockSpec((tm, lora), lambda i: (i, 1)),
                  pl.BlockSpec((tm, LANES), lambda i: (i, E_KRA)),
                  pl.BlockSpec((tm, LANES), lambda i: (i, E_KRB)),
                  pl.BlockSpec((1, lora), lambda i: (0, 0)),
                  pl.BlockSpec((1, lora), lambda i: (0, 0)),
                  pl.BlockSpec((lora, width), lambda i: (0, 0)),
                  pl.BlockSpec((lora, width), lambda i: (0, 0)),
                  pl.BlockSpec((tm, LANES), lambda i: (i % nt, 0)),
                  pl.BlockSpec((tm, LANES), lambda i: (i % nt, 0))],
        out_specs=[pl.BlockSpec((tm, width), lambda i: (i, 0)),
                   pl.BlockSpec((tm, width), lambda i: (i, 0)),
                   pl.BlockSpec((tm, MLA_H * LANES), lambda i: (i, 0))],
        out_shape=[jax.ShapeDtypeStruct((n, width), BF16),
                   jax.ShapeDtypeStruct((n, width), BF16),
                   jax.ShapeDtypeStruct((n, MLA_H * LANES), BF16)],
        scratch_shapes=[pltpu.VMEM((tm, lora), BF16), pltpu.VMEM((tm, lora), BF16)],
        compiler_params=_cparams(("parallel",), 4 * _nbytes((tm, lora), F32), 12 * _nbytes((tm, LANES), F32),
                                 4 * _nbytes((lora, width), BF16), 2 * _nbytes((tm, 5 * MLA_H * LANES), BF16),
                                 3 * _nbytes((tm, width), F32)),
        name="mla_up",
    )(proj, proj, proj, proj, qn.reshape(1, lora), kvn.reshape(1, lora), wq, wkv, cos_t, sin_t)


ATT_TK = 1024
ATT_SUB = 1024
LOG2E = 1.4426950408889634


def _attention_kernel(q_ref, k_ref, v_ref, o_ref):
    tq = q_ref.shape[1]
    t = k_ref.shape[1]
    for r0 in range(0, tq, ATT_SUB):
        q = q_ref[0, r0:r0 + ATT_SUB, :]
        m = l = acc = None
        for c0 in range(0, t, ATT_TK):
            s = lax.dot_general(q, k_ref[0, c0:c0 + ATT_TK, :], (((1,), (1,)), ((), ())),
                                preferred_element_type=F32)
            m_c = jnp.max(s, axis=-1, keepdims=True)
            m_new = m_c if m is None else jnp.maximum(m, m_c)
            p = jnp.exp2(s - m_new)
            pv = jnp.dot(p.astype(BF16), v_ref[0, c0:c0 + ATT_TK, :], preferred_element_type=F32)
            l_c = jnp.sum(p, axis=-1, keepdims=True)
            if m is None:
                l, acc = l_c, pv
            else:
                alpha = jnp.exp2(m - m_new)
                l, acc = alpha * l + l_c, alpha * acc + pv
            m = m_new
        o_ref[0, r0:r0 + ATT_SUB, :] = (acc / l).astype(o_ref.dtype)


def _attention(q, k, v, heads, kv_heads, tq=1024):
    b, t, _ = q.shape
    dq = q.shape[2] // heads
    dv = v.shape[2] // kv_heads
    g = heads // kv_heads
    tq = min(tq, t)
    return pl.pallas_call(
        _attention_kernel,
        grid=(b, heads, t // tq),
        in_specs=[pl.BlockSpec((1, tq, dq), lambda bi, h, qi: (bi, qi, h)),
                  pl.BlockSpec((1, t, dq), lambda bi, h, qi: (bi, 0, h // g)),
                  pl.BlockSpec((1, t, dv), lambda bi, h, qi: (bi, 0, h // g))],
        out_specs=pl.BlockSpec((1, tq, dv), lambda bi, h, qi: (bi, qi, h)),
        out_shape=jax.ShapeDtypeStruct((b, t, heads * dv), BF16),
        compiler_params=_cparams(("parallel", "parallel", "arbitrary"), 2 * _nbytes((tq, dq), BF16),
                                 2 * _nbytes((t, dq + dv), BF16), 2 * _nbytes((tq, dv), BF16),
                                 6 * _nbytes((ATT_SUB, ATT_TK), F32)),
        name="attention",
    )(q, k, v)


HG_CHUNK = 128
HG_DIRECT_CHUNK = 32
HG_MAX_LOG_RANGE = 80.0


def _split3(x):
    a = x.astype(BF16)
    r = x - a.astype(F32)
    b = r.astype(BF16)
    c = (r - b.astype(F32)).astype(BF16)
    return a, b, c


def _tri_masks(c_sz, reverse):
    row = lax.broadcasted_iota(jnp.int32, (c_sz, c_sz), 0)
    col = lax.broadcasted_iota(jnp.int32, (c_sz, c_sz), 1)
    keep = (row <= col) if reverse else (row >= col)
    return keep, jnp.where(keep, 1.0, 0.0).astype(BF16)


def _hgrn2_kernel(hq_ref, ff_ref, fb_ref, hi_ref, hg_ref, lbf_ref, lbb_ref, on_ref, o_ref,
                  accf_ref, accb_ref, sf_ref, sb_ref, dev_ref):
    t = hq_ref.shape[1]

    def chunk_inputs(f_ref, lb, rows):
        gate = lb + (1.0 - lb) * jax.nn.sigmoid(f_ref[0, rows, :])
        hq = hq_ref[0, rows, :]
        return hq * jax.nn.sigmoid(hq), 1.0 - gate, hi_ref[0, rows, :], jnp.log(gate)

    def log_decay(lf, tri):
        p0, p1, p2 = _split3(lf)
        return (jnp.dot(tri, p0, preferred_element_type=F32) + jnp.dot(tri, p1, preferred_element_type=F32)
                + jnp.dot(tri, p2, preferred_element_type=F32))

    def state_step(s_ref, q, k, v, b, b_last):
        s_t = s_ref[...]
        inter = lax.dot_general((q * jnp.exp(b)).astype(BF16), s_t.astype(BF16), (((1,), (1,)), ((), ())),
                                preferred_element_type=F32)
        kd = (k * jnp.exp(b_last - b)).astype(BF16)
        upd = lax.dot_general(v.astype(BF16), kd, (((0,), (0,)), ((), ())), preferred_element_type=F32)
        s_ref[...] = jnp.exp(b_last) * s_t + upd
        return inter

    def fast_step(f_ref, lb, s_ref, acc_ref, c, masks, reverse):
        c_sz = HG_CHUNK
        keep, tri = masks
        rows = pl.ds(pl.multiple_of(c * c_sz, c_sz), c_sz)
        q, k, v, lf = chunk_inputs(f_ref, lb, rows)
        b = log_decay(lf, tri)
        r = b[c_sz // 2:c_sz // 2 + 1, :]
        b_last = b[0:1, :] if reverse else b[c_sz - 1:c_sz, :]
        b_first = b[c_sz - 1:c_sz, :] if reverse else b[0:1, :]
        dev_ref[...] = jnp.maximum(dev_ref[...], jnp.maximum(jnp.abs(b_first - r), jnp.abs(b_last - r)))
        qe = (q * jnp.exp(b - r)).astype(BF16)
        ke = (k * jnp.exp(r - b)).astype(BF16)
        a = lax.dot_general(qe, ke, (((1,), (1,)), ((), ())), preferred_element_type=F32)
        a = jnp.where(keep, a, 0.0).astype(BF16)
        intra = jnp.dot(a, v.astype(BF16), preferred_element_type=F32)
        acc_ref[rows, :] = intra + state_step(s_ref, q, k, v, b, b_last)

    def direct_step(f_ref, lb, s_ref, acc_ref, c, masks, reverse):
        c_sz = HG_DIRECT_CHUNK
        _, tri = masks
        row1 = lax.broadcasted_iota(jnp.int32, (c_sz, 1), 0)
        rows = pl.ds(pl.multiple_of(c * c_sz, c_sz), c_sz)
        q, k, v, lf = chunk_inputs(f_ref, lb, rows)
        b = log_decay(lf, tri)
        b_last = b[0:1, :] if reverse else b[c_sz - 1:c_sz, :]
        o = state_step(s_ref, q, k, v, b, b_last)
        for s in range(c_sz):
            e = jnp.exp(jnp.minimum(b - b[s:s + 1, :], 0.0))
            a = jnp.sum(q * k[s:s + 1, :] * e, axis=-1, keepdims=True)
            a = jnp.where((row1 <= s) if reverse else (row1 >= s), a, 0.0)
            o = o + a * v[s:s + 1, :]
        acc_ref[rows, :] = o

    def scan_both(step, c_sz):
        nc = t // c_sz
        masks_f = _tri_masks(c_sz, False)
        masks_b = _tri_masks(c_sz, True)
        sf_ref[...] = jnp.zeros_like(sf_ref)
        sb_ref[...] = jnp.zeros_like(sb_ref)

        def body(ci, carry):
            step(ff_ref, lbf_ref[...], sf_ref, accf_ref, ci, masks_f, False)
            step(fb_ref, lbb_ref[...], sb_ref, accb_ref, nc - 1 - ci, masks_b, True)
            return carry

        lax.fori_loop(0, nc, body, 0, unroll=4)

    dev_ref[...] = jnp.zeros_like(dev_ref)
    scan_both(fast_step, HG_CHUNK)

    @pl.when(jnp.logical_not(jnp.max(dev_ref[...]) <= HG_MAX_LOG_RANGE))
    def _():
        scan_both(direct_step, HG_DIRECT_CHUNK)

    def finish(c, carry):
        rows = pl.ds(pl.multiple_of(c * 256, 256), 256)
        hg = hg_ref[0, rows, :]
        y = _rms_rows(accf_ref[rows, :] + accb_ref[rows, :], on_ref[...]) * (hg * jax.nn.sigmoid(hg))
        o_ref[0, rows, :] = y.astype(o_ref.dtype)
        return carry

    lax.fori_loop(0, t // 256, finish, 0)


def _hgrn2(proj, lb_f, lb_b, out_norm):
    b, t, _ = proj.shape

    def col(base):
        return pl.BlockSpec((1, t, LANES), lambda bi, h: (bi, 0, base + h))

    return pl.pallas_call(
        _hgrn2_kernel,
        grid=(b, HG_H),
        in_specs=[col(E_HQ), col(E_FF), col(E_FB), col(E_HI), col(E_HG),
                  pl.BlockSpec((1, LANES), lambda bi, h: (0, h)),
                  pl.BlockSpec((1, LANES), lambda bi, h: (0, h)),
                  pl.BlockSpec((1, LANES), lambda bi, h: (0, 0))],
        out_specs=pl.BlockSpec((1, t, LANES), lambda bi, h: (bi, 0, h)),
        out_shape=jax.ShapeDtypeStruct((b, t, HG_H * HG_DV), BF16),
        scratch_shapes=[pltpu.VMEM((t, HG_DV), F32), pltpu.VMEM((t, HG_DV), F32),
                        pltpu.VMEM((HG_DV, HG_DK), F32), pltpu.VMEM((HG_DV, HG_DK), F32), pltpu.VMEM((1, HG_DK), F32)],
        compiler_params=_cparams(("parallel", "parallel"), 10 * _nbytes((t, LANES), F32), 4 * _nbytes((t, LANES), F32)),
        name="hgrn2",
    )(proj, proj, proj, proj, proj, lb_f, lb_b, out_norm.reshape(1, HG_DV))


def _out_proj_kernel(a_ref, b_ref, wa_ref, wb_ref, *refs, bounds):
    x_refs, o_ref = refs[:-1], refs[-1]
    acc = jnp.dot(a_ref[...], wa_ref[...], preferred_element_type=F32)
    acc = acc + jnp.dot(b_ref[...], wb_ref[...], preferred_element_type=F32)
    for x_ref, bound in zip(x_refs, bounds):
        @pl.when(_in_part(pl.program_id(0), bound))
        def _():
            o_ref[...] = x_ref[...] + acc


def _out_proj(a, b, w, x_parts, tm=1024, tn=512):
    n, ka = a.shape
    d = w.shape[1]
    tm = min([tm] + [p.shape[0] for p in x_parts])
    x_specs, bounds = _row_parts(x_parts, tm, tn, lambda j: j)
    return pl.pallas_call(
        functools.partial(_out_proj_kernel, bounds=bounds),
        grid=(n // tm, d // tn),
        in_specs=[pl.BlockSpec((tm, ka), lambda i, j: (i, 0)),
                  pl.BlockSpec((tm, ka), lambda i, j: (i, 0)),
                  pl.BlockSpec((ka, tn), lambda i, j: (0, j)),
                  pl.BlockSpec((ka, tn), lambda i, j: (1, j))] + x_specs,
        out_specs=pl.BlockSpec((tm, tn), lambda i, j: (i, j)),
        out_shape=jax.ShapeDtypeStruct((n, d), F32),
        compiler_params=_cparams(("parallel", "arbitrary"), 4 * _nbytes((tm, ka), BF16), 4 * _nbytes((ka, tn), BF16),
                                 (3 + 2 * len(x_parts)) * _nbytes((tm, tn), F32)),
        name="out_proj",
    )(a, b, w, w, *x_parts)


def _ffn_kernel(x_ref, g_ref, wg_ref, wu_ref, wd_ref, o_ref, xn_ref):
    f = pl.program_id(1)

    @pl.when(f == 0)
    def _():
        _norm_into(xn_ref, x_ref, g_ref)
        o_ref[...] = x_ref[...]

    xn = xn_ref[...]
    gt = jnp.dot(xn, wg_ref[...], preferred_element_type=F32)
    up = jnp.dot(xn, wu_ref[...], preferred_element_type=F32)
    act = (gt * jax.nn.sigmoid(gt) * up).astype(BF16)
    o_ref[...] += jnp.dot(act, wd_ref[...], preferred_element_type=F32)


def _ffn(x, g, wg, wu, wd, tm=768, tf=512):
    n, d = x.shape
    ff = wg.shape[1]
    tm = min(tm, n)
    return pl.pallas_call(
        _ffn_kernel,
        grid=(n // tm, ff // tf),
        in_specs=[pl.BlockSpec((tm, d), lambda i, f: (i, 0)),
                  pl.BlockSpec((1, d), lambda i, f: (0, 0)),
                  pl.BlockSpec((d, tf), lambda i, f: (0, f)),
                  pl.BlockSpec((d, tf), lambda i, f: (0, f)),
                  pl.BlockSpec((tf, d), lambda i, f: (f, 0))],
        out_specs=pl.BlockSpec((tm, d), lambda i, f: (i, 0)),
        out_shape=jax.ShapeDtypeStruct((n, d), F32),
        scratch_shapes=[pltpu.VMEM((tm, d), BF16)],
        compiler_params=_cparams(("parallel", "arbitrary"), 4 * _nbytes((tm, d), F32), _nbytes((tm, d), BF16),
                                 6 * _nbytes((d, tf), BF16), 4 * _nbytes((tm, tf), F32)),
        name="ffn",
    )(x, g.reshape(1, d), wg, wu, wd)


def _axial_prep_kernel(qd_ref, kd_ref, vd_ref, cq_ref, sq_ref, ck_ref, sk_ref, q_ref, k_ref, v_ref, *, scale):
    def rope(x_ref, c_ref, s_ref, h, mul):
        x = x_ref[:, h * LANES:(h + 1) * LANES]
        r = lax.rsqrt(jnp.mean(x * x, axis=-1, keepdims=True) + EPS)
        return ((x * c_ref[...] + _partner32(x) * s_ref[...]) * (r * mul)).astype(BF16)

    for h in range(AX_H):
        q_ref[:, h * LANES:(h + 1) * LANES] = rope(qd_ref, cq_ref, sq_ref, h, scale)
    for h in range(AX_KV):
        k_ref[:, h * LANES:(h + 1) * LANES] = rope(kd_ref, ck_ref, sk_ref, h, 1.0)
    v_ref[...] = vd_ref[...].astype(BF16)


def _axial_prep(proj, cq, sq, ck, sk, seq, tm=512):
    n = proj.shape[0]
    tm = min(tm, seq)
    nt = seq // tm
    qw, kw = AX_H * AX_HD, AX_KV * AX_HD
    tab = pl.BlockSpec((tm, LANES), lambda i: (i % nt, 0))
    return pl.pallas_call(
        functools.partial(_axial_prep_kernel, scale=float(AX_HD ** -0.5) * LOG2E),
        grid=(n // tm,),
        in_specs=[pl.BlockSpec((tm, qw), lambda i: (i, O_QD * LANES // qw)),
                  pl.BlockSpec((tm, kw), lambda i: (i, O_KD * LANES // kw)),
                  pl.BlockSpec((tm, kw), lambda i: (i, O_VD * LANES // kw)),
                  tab, tab, tab, tab],
        out_specs=[pl.BlockSpec((tm, qw), lambda i: (i, 0)),
                   pl.BlockSpec((tm, kw), lambda i: (i, 0)),
                   pl.BlockSpec((tm, kw), lambda i: (i, 0))],
        out_shape=[jax.ShapeDtypeStruct((n, qw), BF16), jax.ShapeDtypeStruct((n, kw), BF16),
                   jax.ShapeDtypeStruct((n, kw), BF16)],
        compiler_params=_cparams(("parallel",), 4 * _nbytes((tm, qw), F32), 6 * _nbytes((tm, kw), F32),
                                 8 * _nbytes((tm, LANES), F32), 2 * _nbytes((tm, qw + 2 * kw), BF16)),
        name="axial_prep",
    )(proj, proj, proj, cq, sq, ck, sk)


WIN_BLOCK = 128


def _window_kernel(sink_ref, slope_ref, q_ref, k_ref, v_ref, o_ref, *, scale):
    t = q_ref.shape[1]
    wb = WIN_BLOCK
    span = 3 * wb
    pair = pl.program_id(1)
    low = lax.broadcasted_iota(jnp.int32, (wb, LANES), 1) < WIN_HD
    row = lax.broadcasted_iota(jnp.int32, (2 * wb, span), 0)
    delta = (row & (wb - 1)) - lax.broadcasted_iota(jnp.int32, (2 * wb, span), 1)
    top = lax.broadcasted_iota(jnp.int32, (2 * wb, 1), 0) < wb
    slope = jnp.where(top, slope_ref[2 * pair], slope_ref[2 * pair + 1]) * LOG2E
    sink = jnp.where(top, sink_ref[2 * pair], sink_ref[2 * pair + 1]) * LOG2E

    def body(qb, carry):
        start = pl.multiple_of(jnp.clip((qb - 1) * wb, 0, t - span), wb)
        kwin = k_ref[0, pl.ds(start, span), :].astype(BF16)
        vwin = v_ref[0, pl.ds(start, span), :].astype(BF16)
        qrows = pl.ds(pl.multiple_of(qb * wb, wb), wb)
        q2 = q_ref[0, qrows, :] * (scale * LOG2E)
        qs = jnp.concatenate([jnp.where(low, q2, 0.0), jnp.where(low, 0.0, q2)], axis=0).astype(BF16)
        dist = jnp.abs(delta + (qb * wb - start))
        s = lax.dot_general(qs, kwin, (((1,), (1,)), ((), ())), preferred_element_type=F32)
        s = jnp.where(dist <= WINDOW, s - slope * dist.astype(F32), -jnp.inf)
        m = jnp.maximum(jnp.max(s, axis=-1, keepdims=True), sink)
        e = jnp.exp2(s - m)
        den = jnp.sum(e, axis=-1, keepdims=True) + jnp.exp2(sink - m)
        o = jnp.dot(e.astype(BF16), vwin, preferred_element_type=F32) / den
        o_ref[0, qrows, :] = jnp.where(low, o[:wb], o[wb:]).astype(o_ref.dtype)
        return carry

    lax.fori_loop(0, t // wb, body, 0, unroll=8)


def _window_attention(proj, sink, slopes):
    b, t, _ = proj.shape
    pairs = WIN_H // 2
    per_kv = pairs // WIN_KV
    smem = pl.BlockSpec(memory_space=pltpu.SMEM)
    return pl.pallas_call(
        functools.partial(_window_kernel, scale=float(WIN_HD ** -0.5)),
        grid=(b, pairs),
        in_specs=[smem, smem,
                  pl.BlockSpec((1, t, LANES), lambda bi, p: (bi, 0, O_QC + p)),
                  pl.BlockSpec((1, t, LANES), lambda bi, p: (bi, 0, O_KC + p // per_kv)),
                  pl.BlockSpec((1, t, LANES), lambda bi, p: (bi, 0, O_VC + p // per_kv))],
        out_specs=pl.BlockSpec((1, t, LANES), lambda bi, p: (bi, 0, p)),
        out_shape=jax.ShapeDtypeStruct((b, t, WIN_H * WIN_HD), BF16),
        compiler_params=_cparams(("parallel", "parallel"), 6 * _nbytes((t, LANES), F32), 2 * _nbytes((t, LANES), BF16)),
        name="window_attention",
    )(sink, slopes, proj, proj, proj)


MOE_TM = 512
R_E1, R_E2, R_W1, R_W2, R_RANK1, R_RANK2 = range(6)


def _router_kernel(x_ref, g_ref, r_ref, route_ref, cnt_ref, carry_ref):
    @pl.when(pl.program_id(0) == 0)
    def _():
        carry_ref[...] = jnp.zeros_like(carry_ref)

    tm = x_ref.shape[0]
    xn = _rms_rows(x_ref[...], g_ref[...])
    logits = [jnp.sum(xn * r_ref[e:e + 1, :], axis=-1, keepdims=True) for e in range(N_EXPERTS)]

    def top(ls):
        m = functools.reduce(jnp.maximum, ls)
        idx = jnp.full_like(m, N_EXPERTS).astype(jnp.int32)
        for e in reversed(range(N_EXPERTS)):
            idx = jnp.where(ls[e] == m, e, idx)
        return m, idx

    m1, i1 = top(logits)
    m2, i2 = top([jnp.where(i1 == e, -jnp.inf, logits[e]) for e in range(N_EXPERTS)])
    e2 = jnp.exp(m2 - m1)
    w1 = 1.0 / (1.0 + e2)
    w2 = e2 / (1.0 + e2)
    lane = lax.broadcasted_iota(jnp.int32, route_ref.shape, 1)
    sel1, sel2 = lane == i1, lane == i2
    onehot = jnp.where(jnp.logical_or(sel1, sel2), 1.0, 0.0)
    row = lax.broadcasted_iota(jnp.int32, (tm, tm), 0)
    col = lax.broadcasted_iota(jnp.int32, (tm, tm), 1)
    earlier = jnp.where(row > col, 1.0, 0.0).astype(BF16)
    before = carry_ref[...] + jnp.dot(earlier, onehot.astype(BF16), preferred_element_type=F32)
    rank1 = jnp.sum(jnp.where(sel1, before, 0.0), axis=-1, keepdims=True)
    rank2 = jnp.sum(jnp.where(sel2, before, 0.0), axis=-1, keepdims=True)
    carry_ref[...] = carry_ref[...] + jnp.sum(onehot, axis=0, keepdims=True)
    cnt_ref[...] = carry_ref[...]
    rec = jnp.zeros(route_ref.shape, F32)
    for pos, val in ((R_E1, i1.astype(F32)), (R_E2, i2.astype(F32)), (R_W1, w1), (R_W2, w2),
                     (R_RANK1, rank1), (R_RANK2, rank2)):
        rec = jnp.where(lane == pos, val, rec)
    route_ref[...] = rec


def _router(x, g, router_t, tm=256):
    n, d = x.shape
    tm = min(tm, n)
    return pl.pallas_call(
        _router_kernel,
        grid=(n // tm,),
        in_specs=[pl.BlockSpec((tm, d), lambda i: (i, 0)),
                  pl.BlockSpec((1, d), lambda i: (0, 0)),
                  pl.BlockSpec((N_EXPERTS, d), lambda i: (0, 0))],
        out_specs=[pl.BlockSpec((tm, LANES), lambda i: (i, 0)),
                   pl.BlockSpec((1, LANES), lambda i: (0, 0))],
        out_shape=[jax.ShapeDtypeStruct((n, LANES), F32), jax.ShapeDtypeStruct((1, LANES), F32)],
        scratch_shapes=[pltpu.VMEM((1, LANES), F32)],
        compiler_params=_cparams(("arbitrary",), 6 * _nbytes((tm, d), F32)),
        name="router",
    )(x, g.reshape(1, d), router_t)


def _row_copy(src_hbm, src_row, dst, dst_row, sem):
    return pltpu.make_async_copy(src_hbm.at[pl.ds(src_row, 1)], dst.at[pl.ds(dst_row, 1)], sem)


def _moe_scatter_kernel(fill_ref, p1_ref, p2_ref, x_ref, xs_hbm, zero_ref, sem, zsem, *, tile):
    i = pl.program_id(0)
    rows = p1_ref.shape[-1]

    @pl.when(i == 0)
    def _():
        zero_ref[...] = jnp.zeros_like(zero_ref)
        for k in range(fill_ref.shape[0]):
            @pl.when(fill_ref[k] >= 0)
            def _():
                start = pl.multiple_of(fill_ref[k], tile)
                fill = pltpu.make_async_copy(zero_ref, xs_hbm.at[pl.ds(start, tile)], zsem)
                fill.start()
                fill.wait()

    def issue(r, carry):
        _row_copy(x_ref, r, xs_hbm, p1_ref[0, 0, r], sem).start()
        _row_copy(x_ref, r, xs_hbm, p2_ref[0, 0, r], sem).start()
        return carry

    lax.fori_loop(0, rows, issue, 0, unroll=8)
    for _ in range(2):
        pltpu.make_async_copy(x_ref, xs_hbm.at[pl.ds(0, rows)], sem).wait()


def _moe_scatter(x, pos1, pos2, fill_rows, total_rows, tile, rows=256):
    n, d = x.shape
    rows = min(rows, n)
    smem_blk = pl.BlockSpec((1, 1, rows), lambda i, fill: (i, 0, 0), memory_space=pltpu.SMEM)
    return pl.pallas_call(
        functools.partial(_moe_scatter_kernel, tile=tile),
        grid_spec=pltpu.PrefetchScalarGridSpec(
            num_scalar_prefetch=1,
            grid=(n // rows,),
            in_specs=[smem_blk, smem_blk, pl.BlockSpec((rows, d), lambda i, fill: (i, 0))],
            out_specs=pl.BlockSpec(memory_space=pl.ANY),
            scratch_shapes=[pltpu.VMEM((tile, d), x.dtype), pltpu.SemaphoreType.DMA, pltpu.SemaphoreType.DMA]),
        out_shape=jax.ShapeDtypeStruct((total_rows, d), x.dtype),
        compiler_params=_cparams(("arbitrary",), _nbytes((tile, d), x.dtype), 2 * _nbytes((rows, d), x.dtype)),
        name="moe_scatter",
    )(fill_rows, pos1.reshape(n // rows, 1, rows), pos2.reshape(n // rows, 1, rows), x)


def _moe_experts_kernel(te_ref, nv_ref, xs_ref, g_ref, wg_ref, wu_ref, wd_ref, y_ref, xn_ref):
    valid = pl.program_id(0) < nv_ref[0]

    @pl.when(valid)
    def _():
        _norm_into(xn_ref, xs_ref, g_ref)
        xn = xn_ref[...]
        gt = jnp.dot(xn, wg_ref[0], preferred_element_type=F32)
        up = jnp.dot(xn, wu_ref[0], preferred_element_type=F32)
        act = (gt * jax.nn.sigmoid(gt) * up).astype(BF16)
        y_ref[...] = jnp.dot(act, wd_ref[0], preferred_element_type=F32)

    @pl.when(jnp.logical_not(valid))
    def _():
        y_ref[...] = jnp.zeros_like(y_ref)


def _moe_experts(xs, g, wg, wu, wd, tile_expert, n_valid, tile):
    rows, d = xs.shape
    ff = wg.shape[2]
    once = dict(pipeline_mode=pl.Buffered(1))

    def row_blk(i, te, nv):
        return (jnp.minimum(i, nv[0] - 1), 0)

    def w_blk(i, te, nv):
        return (te[i], 0, 0)

    return pl.pallas_call(
        _moe_experts_kernel,
        grid_spec=pltpu.PrefetchScalarGridSpec(
            num_scalar_prefetch=2,
            grid=(rows // tile,),
            in_specs=[pl.BlockSpec((tile, d), row_blk),
                      pl.BlockSpec((1, d), lambda i, te, nv: (0, 0)),
                      pl.BlockSpec((1, d, ff), w_blk, **once),
                      pl.BlockSpec((1, d, ff), w_blk, **once),
                      pl.BlockSpec((1, ff, d), w_blk, **once)],
            out_specs=pl.BlockSpec((tile, d), lambda i, te, nv: (i, 0)),
            scratch_shapes=[pltpu.VMEM((tile, d), BF16)]),
        out_shape=jax.ShapeDtypeStruct((rows, d), F32),
        compiler_params=_cparams(("arbitrary",), 4 * _nbytes((tile, d), F32), _nbytes((tile, d), BF16),
                                 3 * _nbytes((d, ff), BF16), 4 * _nbytes((tile, ff), F32), _nbytes((tile, d), F32)),
        name="moe_experts",
    )(tile_expert, n_valid, xs, g.reshape(1, d), wg, wu, wd)


def _moe_combine_kernel(p1_ref, p2_ref, x_ref, route_ref, g_ref, y_hbm, o_ref, ya_ref, yb_ref, sem, *, final):
    rows = x_ref.shape[0]

    def issue(r, carry):
        _row_copy(y_hbm, p1_ref[0, 0, r], ya_ref, r, sem).start()
        _row_copy(y_hbm, p2_ref[0, 0, r], yb_ref, r, sem).start()
        return carry

    lax.fori_loop(0, rows, issue, 0, unroll=8)
    for dst in (ya_ref, yb_ref):
        pltpu.make_async_copy(y_hbm.at[pl.ds(0, rows)], dst, sem).wait()
    lane = lax.broadcasted_iota(jnp.int32, route_ref.shape, 1)
    w1 = jnp.sum(jnp.where(lane == R_W1, route_ref[...], 0.0), axis=-1, keepdims=True)
    w2 = jnp.sum(jnp.where(lane == R_W2, route_ref[...], 0.0), axis=-1, keepdims=True)
    out = x_ref[...] + w1 * ya_ref[...] + w2 * yb_ref[...]
    o_ref[...] = _rms_rows(out, g_ref[...]) if final else out


def _moe_combine(x, route, pos1, pos2, y, gain, row0, nrows, final, rows=256):
    n, d = x.shape
    rows = min(rows, nrows)
    off = row0 // rows
    smem_blk = pl.BlockSpec((1, 1, rows), lambda i: (i + off, 0, 0), memory_space=pltpu.SMEM)
    return pl.pallas_call(
        functools.partial(_moe_combine_kernel, final=final),
        grid=(nrows // rows,),
        in_specs=[smem_blk, smem_blk,
                  pl.BlockSpec((rows, d), lambda i: (i + off, 0)),
                  pl.BlockSpec((rows, LANES), lambda i: (i + off, 0)),
                  pl.BlockSpec((1, d), lambda i: (0, 0)),
                  pl.BlockSpec(memory_space=pl.ANY)],
        out_specs=pl.BlockSpec((rows, d), lambda i: (i, 0)),
        out_shape=jax.ShapeDtypeStruct((nrows, d), F32),
        scratch_shapes=[pltpu.VMEM((rows, d), F32), pltpu.VMEM((rows, d), F32), pltpu.SemaphoreType.DMA],
        compiler_params=_cparams(("arbitrary",), 8 * _nbytes((rows, d), F32)),
        name="moe_combine",
    )(pos1.reshape(n // rows, 1, rows), pos2.reshape(n // rows, 1, rows), x, route, gain.reshape(1, d), y)


def _swap_halves(w, width):
    lead = w.shape[:-1]
    return jnp.flip(w.reshape(lead + (-1, 2, width // 2)), axis=-2).reshape(w.shape)


def _pad_cols(w, width):
    return jnp.pad(w, [(0, 0)] * (w.ndim - 1) + [(0, width - w.shape[-1])])


def _even_in_weight(w):
    kr = w[:, 2 * MLA_LORA:2 * MLA_LORA + MLA_ROPE]
    return jnp.concatenate([w[:, :2 * MLA_LORA], _pad_cols(kr, LANES), _pad_cols(_swap_halves(kr, MLA_ROPE), LANES),
                            w[:, 2 * MLA_LORA + MLA_ROPE:]], axis=1).astype(BF16)


def _mla_q_weight(w):
    w = w.reshape(MLA_LORA, MLA_H, MLA_NOPE + MLA_ROPE)
    rope = w[..., MLA_NOPE:]
    w = jnp.concatenate([w[..., :MLA_NOPE], _pad_cols(rope, LANES)], -1)
    return w.reshape(MLA_LORA, MLA_H * 2 * LANES).astype(BF16)


def _odd_in_weight(w):
    d = w.shape[0]
    o = 0
    parts = {}
    for name, width in (("qc", WIN_H * WIN_HD), ("kc", WIN_KV * WIN_HD), ("vc", WIN_KV * WIN_HD),
                        ("qd", AX_H * AX_HD), ("kd", AX_KV * AX_HD), ("vd", AX_KV * AX_HD)):
        parts[name] = w[:, o:o + width]
        o += width

    def dup(x):
        x = x.reshape(d, WIN_KV, WIN_HD)
        return jnp.concatenate([x, x], axis=-1).reshape(d, WIN_KV * LANES)

    out = jnp.concatenate([parts["qd"], parts["qc"], parts["kd"], parts["vd"], dup(parts["kc"]), dup(parts["vc"])],
                          axis=1)
    assert out.shape[1] == O_END * LANES
    return out.astype(BF16)


def _rope_tables(pos, dim):
    inv = ROPE_THETA ** (-jnp.arange(0, dim, 2, dtype=F32) / dim)
    ang = pos[:, None] * inv[None, :]
    cos, sin = jnp.cos(ang), jnp.sin(ang)
    return jnp.concatenate([cos, cos], -1), jnp.concatenate([-sin, sin], -1)


def _alibi_slopes(n):
    return jnp.asarray(2.0 ** (-8.0 * np.arange(1, n + 1) / n), dtype=F32)


def _mixer_even(x_parts, bsz, seq, j, norm_g, w_in, q_norm, w_uq, kv_norm, w_ukv, lb_fwd, lb_bwd, out_norm, w_out):
    n = bsz * seq
    proj = _norm_matmul(x_parts, norm_g, _even_in_weight(w_in), tn=10 * LANES, out_dtype=F32)
    cos, sin = _rope_tables(jnp.arange(seq, dtype=F32), MLA_ROPE)
    q, k, v = _mla_up(proj, q_norm, kv_norm, _mla_q_weight(w_uq), w_ukv.astype(BF16),
                      _pad_cols(cos, LANES), _pad_cols(sin, LANES), seq)
    o_a = _attention(q.reshape(bsz, seq, -1), k.reshape(bsz, seq, -1), v.reshape(bsz, seq, -1), MLA_H, MLA_H)

    def lower_bound(tab):
        return jnp.cumsum(jax.nn.softmax(tab.astype(F32), axis=0), axis=0)[j].reshape(1, HG_H * HG_DK)

    o_b = _hgrn2(proj.reshape(bsz, seq, -1), lower_bound(lb_fwd), lower_bound(lb_bwd), out_norm)
    return _out_proj(o_a.reshape(n, -1), o_b.reshape(n, -1), w_out.astype(BF16), x_parts)


def _mixer_odd(x_parts, bsz, seq, norm_g, w_in, sink, q_norm, k_norm, w_out):
    n = bsz * seq
    proj = _norm_matmul(x_parts, norm_g, _odd_in_weight(w_in), tn=8 * LANES, out_dtype=F32)
    o_c = _window_attention(proj.reshape(bsz, seq, -1), sink.astype(F32), _alibi_slopes(WIN_H))
    pos = jnp.arange(seq)
    half = AX_HD // 2
    c_row, s_row = _rope_tables((pos // GRID_W).astype(F32), half)
    c_col, s_col = _rope_tables((pos % GRID_W).astype(F32), half)
    cos = jnp.concatenate([c_row, c_col], -1)
    sin = jnp.concatenate([s_row, s_col], -1)

    def tables(g):
        g = g.astype(F32)
        return cos * g[None, :], sin * _swap_halves(g, half)[None, :]

    cq, sq = tables(q_norm)
    ck, sk = tables(k_norm)
    q, k, v = _axial_prep(proj, cq, sq, ck, sk, seq)
    o_d = _attention(q.reshape(bsz, seq, -1), k.reshape(bsz, seq, -1), v.reshape(bsz, seq, -1), AX_H, AX_KV)
    return _out_proj(o_c.reshape(n, -1), o_d.reshape(n, -1), w_out.astype(BF16), x_parts)


def _moe(x, norm_g, router, w_gate, w_up, w_down, out_gain, out_splits):
    n = x.shape[0]
    tile = MOE_TM
    route, counts = _router(x, norm_g, router.astype(F32).T)
    counts = counts[0, :N_EXPERTS].astype(jnp.int32)
    padded = (counts + tile - 1) // tile * tile
    ends = jnp.cumsum(padded)
    starts = ends - padded
    e1, e2 = route[:, R_E1].astype(jnp.int32), route[:, R_E2].astype(jnp.int32)
    pos1 = starts[e1] + route[:, R_RANK1].astype(jnp.int32)
    pos2 = starts[e2] + route[:, R_RANK2].astype(jnp.int32)
    n_tiles = 2 * n // tile + N_EXPERTS
    n_valid = ends[-1] // tile
    tile_row = jnp.minimum(jnp.arange(n_tiles), n_valid - 1) * tile
    tile_expert = jnp.sum(tile_row[:, None] >= ends[None, :], axis=1).astype(jnp.int32)
    group_tail = jnp.where(padded > 0, ends - tile, -1)
    spare = (n_valid + jnp.arange(N_EXPERTS)) * tile
    fill_rows = jnp.concatenate([group_tail, jnp.where(spare < n_tiles * tile, spare, -1)]).astype(jnp.int32)
    xs = _moe_scatter(x, pos1, pos2, fill_rows, n_tiles * tile, tile)
    y = _moe_experts(xs, norm_g, w_gate.astype(BF16), w_up.astype(BF16), w_down.astype(BF16),
                     tile_expert, n_valid.reshape(1).astype(jnp.int32), tile)
    final = out_gain is not None
    gain = out_gain if final else jnp.ones((x.shape[1],), F32)
    return [_moe_combine(x, route, pos1, pos2, y, gain, row0, nrows, final) for row0, nrows in out_splits]


def _trunk(x_parts, bsz, seq, norm_mix_e, w_in_e, mla_q_norm, mla_w_uq, mla_kv_norm, mla_w_ukv, hg_lb_fwd, hg_lb_bwd,
           hg_out_norm, w_out_e, norm_ffn_e, ffn_w_gate, ffn_w_up, ffn_w_down, norm_mix_o, w_in_o, win_sink,
           ax_q_norm, ax_k_norm, w_out_o, norm_ffn_o, moe_router, moe_w_gate, moe_w_up, moe_w_down, final_norm,
           out_splits):
    assert DEPTH % 2 == 0
    for l in range(DEPTH):
        j = l // 2
        if l % 2 == 0:
            x = _mixer_even(x_parts, bsz, seq, j, norm_mix_e[j], w_in_e[j], mla_q_norm[j], mla_w_uq[j],
                            mla_kv_norm[j], mla_w_ukv[j], hg_lb_fwd, hg_lb_bwd, hg_out_norm[j], w_out_e[j])
            x = _ffn(x, norm_ffn_e[j], ffn_w_gate[j].astype(BF16), ffn_w_up[j].astype(BF16),
                     ffn_w_down[j].astype(BF16))
        else:
            x = _mixer_odd(x_parts, bsz, seq, norm_mix_o[j], w_in_o[j], win_sink[j], ax_q_norm[j], ax_k_norm[j],
                           w_out_o[j])
            last = l == DEPTH - 1
            outs = _moe(x, norm_ffn_o[j], moe_router[j], moe_w_gate[j], moe_w_up[j], moe_w_down[j],
                        final_norm if last else None, out_splits if last else [(0, x.shape[0])])
            if last:
                return outs
            x = outs[0]
        x_parts = (x,)


def kernel(x_prompt, x_sample, norm_mix_e, w_in_e, mla_q_norm, mla_w_uq, mla_kv_norm, mla_w_ukv, hg_lb_fwd, hg_lb_bwd, hg_out_norm, w_out_e, norm_ffn_e, ffn_w_gate, ffn_w_up, ffn_w_down, norm_mix_o, w_in_o, win_sink, ax_q_norm, ax_k_norm, w_out_o, norm_ffn_o, moe_router, moe_w_gate, moe_w_up, moe_w_down, final_norm):
    bp, seq, d = x_prompt.shape
    bs = x_sample.shape[0]
    assert x_sample.shape[1:] == (seq, d)
    y_prompt, y_sample = _trunk(
        (x_prompt.reshape(bp * seq, d), x_sample.reshape(bs * seq, d)), bp + bs, seq, norm_mix_e, w_in_e, mla_q_norm, mla_w_uq, mla_kv_norm, mla_w_ukv, hg_lb_fwd, hg_lb_bwd,
        hg_out_norm, w_out_e, norm_ffn_e, ffn_w_gate, ffn_w_up, ffn_w_down, norm_mix_o, w_in_o, win_sink, ax_q_norm,
        ax_k_norm, w_out_o, norm_ffn_o, moe_router, moe_w_gate, moe_w_up, moe_w_down, final_norm,
        [(0, bp * seq), (bp * seq, bs * seq)])
    return (y_prompt.reshape(bp, seq, d), y_sample.reshape(bs, seq, d))
```

```python
import functools

import jax
import jax.numpy as jnp
import numpy as np
from jax import lax
from jax.experimental import pallas as pl
from jax.experimental.pallas import tpu as pltpu

D_MODEL = 2048
DEPTH = 2
GRID_W = 64
EPS = 1e-6
ROPE_THETA = 10000.0

MLA_H = 8
MLA_NOPE = 128
MLA_ROPE = 64
MLA_V = 128
MLA_LORA = D_MODEL // 4

HG_H = 8
HG_DK = 128
HG_DV = 128

WIN_H = 16
WIN_KV = 2
WIN_HD = 64
WINDOW = 128

AX_H = 8
AX_KV = 2
AX_HD = 128

FF_DENSE = 5632
N_EXPERTS = 8
FF_EXPERT = 1408

LANES = 128
VMEM_CAP = 60000 * 1024
BF16 = jnp.bfloat16
F32 = jnp.float32

E_CQ, E_CKV, E_KRA, E_KRB, E_HQ, E_FF, E_FB, E_HI, E_HG, E_END = 0, 4, 8, 9, 10, 18, 26, 34, 42, 50
O_QD, O_QC, O_KD, O_VD, O_KC, O_VC, O_END = 0, 8, 16, 18, 20, 22, 24


def _cparams(sem, *block_bytes):
    need = int(sum(block_bytes)) + (6 << 20)
    return pltpu.CompilerParams(dimension_semantics=sem, vmem_limit_bytes=min(max(need, 16 << 20), VMEM_CAP))


def _nbytes(shape, dtype):
    return int(np.prod(shape)) * jnp.dtype(dtype).itemsize


def _rms_rows(x, g):
    return x * lax.rsqrt(jnp.mean(x * x, axis=-1, keepdims=True) + EPS) * g


def _norm_into(dst_ref, x_ref, g_ref, chunk=256):
    rows = x_ref.shape[0]
    chunk = min(chunk, rows)

    def body(c, carry):
        r = pl.ds(pl.multiple_of(c * chunk, chunk), chunk)
        dst_ref[r, :] = _rms_rows(x_ref[r, :].astype(F32), g_ref[...]).astype(dst_ref.dtype)
        return carry

    lax.fori_loop(0, rows // chunk, body, 0)


def _row_parts(parts, tm, width, col):
    specs, bounds, start = [], [], 0
    for p in parts:
        nt = p.shape[0] // tm

        def index(i, j, start=start, nt=nt):
            inside = jnp.logical_and(i >= start, i < start + nt)
            return (jnp.clip(i - start, 0, nt - 1), jnp.where(inside, col(j), 0))

        specs.append(pl.BlockSpec((tm, width), index))
        bounds.append((start, start + nt))
        start += nt
    return specs, bounds


def _in_part(i, bound):
    return jnp.logical_and(i >= bound[0], i < bound[1])


def _norm_matmul_kernel(*refs, bounds):
    x_refs = refs[:len(bounds)]
    g_ref, w_ref, o_ref, xn_ref = refs[len(bounds):]
    for x_ref, bound in zip(x_refs, bounds):
        @pl.when(jnp.logical_and(pl.program_id(1) == 0, _in_part(pl.program_id(0), bound)))
        def _():
            _norm_into(xn_ref, x_ref, g_ref)

    o_ref[...] = jnp.dot(xn_ref[...], w_ref[...], preferred_element_type=F32).astype(o_ref.dtype)


def _norm_matmul(parts, g, w, tn, out_dtype):
    k = parts[0].shape[1]
    n = sum(p.shape[0] for p in parts)
    nout = w.shape[1]

    def resident(tm):
        return (2 * len(parts) * _nbytes((tm, k), F32), _nbytes((tm, k), BF16), 2 * _nbytes((k, tn), BF16),
                3 * _nbytes((tm, tn), F32))

    rows = min(p.shape[0] for p in parts)
    tm = next(t for t in (1024, 512, 256) if t <= rows and sum(resident(t)) + (8 << 20) <= VMEM_CAP)
    x_specs, bounds = _row_parts(parts, tm, k, lambda j: 0)
    return pl.pallas_call(
        functools.partial(_norm_matmul_kernel, bounds=bounds),
        grid=(n // tm, nout // tn),
        in_specs=x_specs + [pl.BlockSpec((1, k), lambda i, j: (0, 0)),
                            pl.BlockSpec((k, tn), lambda i, j: (0, j))],
        out_specs=pl.BlockSpec((tm, tn), lambda i, j: (i, j)),
        out_shape=jax.ShapeDtypeStruct((n, nout), out_dtype),
        scratch_shapes=[pltpu.VMEM((tm, k), BF16)],
        compiler_params=_cparams(("parallel", "arbitrary"), *resident(tm)),
        name="norm_matmul",
    )(*parts, g.reshape(1, k), w)


def _partner32(x):
    lane = lax.broadcasted_iota(jnp.int32, x.shape, 1)
    return jnp.where((lane & 32) == 0, pltpu.roll(x, LANES - 32, axis=1), pltpu.roll(x, 32, axis=1))


def _mla_up_kernel(cq_ref, ckv_ref, kra_ref, krb_ref, qn_ref, kvn_ref, wq_ref, wkv_ref, cos_ref, sin_ref,
                   q_ref, k_ref, v_ref, cqn_ref, ckvn_ref, *, scale):
    _norm_into(cqn_ref, cq_ref, qn_ref)
    _norm_into(ckvn_ref, ckv_ref, kvn_ref)
    cos, sin = cos_ref[...], sin_ref[...]
    k_rope = (kra_ref[...] * cos + krb_ref[...] * sin).astype(BF16)
    q = jnp.dot(cqn_ref[...], wq_ref[...], preferred_element_type=F32)
    kv = jnp.dot(ckvn_ref[...], wkv_ref[...], preferred_element_type=F32)
    for h in range(MLA_H):
        lo, mid, hi = 2 * h * LANES, (2 * h + 1) * LANES, (2 * h + 2) * LANES
        q_r = q[:, mid:hi]
        q_ref[:, lo:mid] = (q[:, lo:mid] * scale).astype(BF16)
        q_ref[:, mid:hi] = ((q_r * cos + _partner32(q_r) * sin) * scale).astype(BF16)
        k_ref[:, lo:mid] = kv[:, lo:mid].astype(BF16)
        k_ref[:, mid:hi] = k_rope
        v_ref[:, h * LANES:(h + 1) * LANES] = kv[:, mid:hi].astype(BF16)


def _mla_up(proj, qn, kvn, wq, wkv, cos_t, sin_t, seq, tm=512):
    n = proj.shape[0]
    tm = min(tm, seq)
    nt = seq // tm
    lora = MLA_LORA
    width = MLA_H * 2 * LANES
    scale = float((MLA_NOPE + MLA_ROPE) ** -0.5) * LOG2E
    return pl.pallas_call(
        functools.partial(_mla_up_kernel, scale=scale),
        grid=(n // tm,),
        in_specs=[pl.BlockSpec((tm, lora), lambda i: (i, 0)),
                  pl.BlockSpec((tm, lora), lambda i: (i, 1)),
                  pl.BlockSpec((tm, LANES), lambda i: (i, E_KRA)),
                  pl.BlockSpec((tm, LANES), lambda i: (i, E_KRB)),
                  pl.BlockSpec((1, lora), lambda i: (0, 0)),
                  pl.BlockSpec((1, lora), lambda i: (0, 0)),
                  pl.BlockSpec((lora, width), lambda i: (0, 0)),
                  pl.BlockSpec((lora, width), lambda i: (0, 0)),
                  pl.BlockSpec((tm, LANES), lambda i: (i % nt, 0)),
                  pl.BlockSpec((tm, LANES), lambda i: (i % nt, 0))],
        out_specs=[pl.BlockSpec((tm, width), lambda i: (i, 0)),
                   pl.BlockSpec((tm, width), lambda i: (i, 0)),
                   pl.BlockSpec((tm, MLA_H * LANES), lambda i: (i, 0))],
        out_shape=[jax.ShapeDtypeStruct((n, width), BF16),
                   jax.ShapeDtypeStruct((n, width), BF16),
                   jax.ShapeDtypeStruct((n, MLA_H * LANES), BF16)],
        scratch_shapes=[pltpu.VMEM((tm, lora), BF16), pltpu.VMEM((tm, lora), BF16)],
        compiler_params=_cparams(("parallel",), 4 * _nbytes((tm, lora), F32), 12 * _nbytes((tm, LANES), F32),
                                 4 * _nbytes((lora, width), BF16), 2 * _nbytes((tm, 5 * MLA_H * LANES), BF16),
                                 3 * _nbytes((tm, width), F32)),
        name="mla_up",
    )(proj, proj, proj, proj, qn.reshape(1, lora), kvn.reshape(1, lora), wq, wkv, cos_t, sin_t)


ATT_TK = 1024
ATT_SUB = 1024
LOG2E = 1.4426950408889634


def _attention_kernel(q_ref, k_ref, v_ref, o_ref):
    tq = q_ref.shape[1]
    t = k_ref.shape[1]
    for r0 in range(0, tq, ATT_SUB):
        q = q_ref[0, r0:r0 + ATT_SUB, :]
        m = l = acc = None
        for c0 in range(0, t, ATT_TK):
            s = lax.dot_general(q, k_ref[0, c0:c0 + ATT_TK, :], (((1,), (1,)), ((), ())),
                                preferred_element_type=F32)
            m_c = jnp.max(s, axis=-1, keepdims=True)
            m_new = m_c if m is None else jnp.maximum(m, m_c)
            p = jnp.exp2(s - m_new)
            pv = jnp.dot(p.astype(BF16), v_ref[0, c0:c0 + ATT_TK, :], preferred_element_type=F32)
            l_c = jnp.sum(p, axis=-1, keepdims=True)
            if m is None:
                l, acc = l_c, pv
            else:
                alpha = jnp.exp2(m - m_new)
                l, acc = alpha * l + l_c, alpha * acc + pv
            m = m_new
        o_ref[0, r0:r0 + ATT_SUB, :] = (acc / l).astype(o_ref.dtype)


def _attention(q, k, v, heads, kv_heads, tq=2048):
    b, t, _ = q.shape
    dq = q.shape[2] // heads
    dv = v.shape[2] // kv_heads
    g = heads // kv_heads
    tq = min(tq, t)
    return pl.pallas_call(
        _attention_kernel,
        grid=(b, heads, t // tq),
        in_specs=[pl.BlockSpec((1, tq, dq), lambda bi, h, qi: (bi, qi, h)),
                  pl.BlockSpec((1, t, dq), lambda bi, h, qi: (bi, 0, h // g)),
                  pl.BlockSpec((1, t, dv), lambda bi, h, qi: (bi, 0, h // g))],
        out_specs=pl.BlockSpec((1, tq, dv), lambda bi, h, qi: (bi, qi, h)),
        out_shape=jax.ShapeDtypeStruct((b, t, heads * dv), BF16),
        compiler_params=_cparams(("parallel", "parallel", "arbitrary"), 2 * _nbytes((tq, dq), BF16),
                                 2 * _nbytes((t, dq + dv), BF16), 2 * _nbytes((tq, dv), BF16),
                                 6 * _nbytes((ATT_SUB, ATT_TK), F32)),
        name="attention",
    )(q, k, v)


HG_CHUNK = 128
HG_DIRECT_CHUNK = 32
HG_MAX_LOG_RANGE = 80.0


def _split3(x):
    a = x.astype(BF16)
    r = x - a.astype(F32)
    b = r.astype(BF16)
    c = (r - b.astype(F32)).astype(BF16)
    return a, b, c


def _tri_masks(c_sz, reverse):
    row = lax.broadcasted_iota(jnp.int32, (c_sz, c_sz), 0)
    col = lax.broadcasted_iota(jnp.int32, (c_sz, c_sz), 1)
    keep = (row <= col) if reverse else (row >= col)
    return keep, jnp.where(keep, 1.0, 0.0).astype(BF16)


def _hgrn2_kernel(hq_ref, ff_ref, fb_ref, hi_ref, hg_ref, lbf_ref, lbb_ref, on_ref, o_ref,
                  accf_ref, accb_ref, sf_ref, sb_ref, dev_ref):
    t = hq_ref.shape[1]

    def chunk_inputs(f_ref, lb, rows):
        gate = lb + (1.0 - lb) * jax.nn.sigmoid(f_ref[0, rows, :])
        hq = hq_ref[0, rows, :]
        return hq * jax.nn.sigmoid(hq), 1.0 - gate, hi_ref[0, rows, :], jnp.log(gate)

    def log_decay(lf, tri):
        p0, p1, p2 = _split3(lf)
        return (jnp.dot(tri, p0, preferred_element_type=F32) + jnp.dot(tri, p1, preferred_element_type=F32)
                + jnp.dot(tri, p2, preferred_element_type=F32))

    def state_step(s_ref, q, k, v, b, b_last):
        s_t = s_ref[...]
        inter = lax.dot_general((q * jnp.exp(b)).astype(BF16), s_t.astype(BF16), (((1,), (1,)), ((), ())),
                                preferred_element_type=F32)
        kd = (k * jnp.exp(b_last - b)).astype(BF16)
        upd = lax.dot_general(v.astype(BF16), kd, (((0,), (0,)), ((), ())), preferred_element_type=F32)
        s_ref[...] = jnp.exp(b_last) * s_t + upd
        return inter

    def fast_step(f_ref, lb, s_ref, acc_ref, c, masks, reverse):
        c_sz = HG_CHUNK
        keep, tri = masks
        rows = pl.ds(pl.multiple_of(c * c_sz, c_sz), c_sz)
        q, k, v, lf = chunk_inputs(f_ref, lb, rows)
        b = log_decay(lf, tri)
        r = b[c_sz // 2:c_sz // 2 + 1, :]
        b_last = b[0:1, :] if reverse else b[c_sz - 1:c_sz, :]
        b_first = b[c_sz - 1:c_sz, :] if reverse else b[0:1, :]
        dev_ref[...] = jnp.maximum(dev_ref[...], jnp.maximum(jnp.abs(b_first - r), jnp.abs(b_last - r)))
        qe = (q * jnp.exp(b - r)).astype(BF16)
        ke = (k * jnp.exp(r - b)).astype(BF16)
        a = lax.dot_general(qe, ke, (((1,), (1,)), ((), ())), preferred_element_type=F32)
        a = jnp.where(keep, a, 0.0).astype(BF16)
        intra = jnp.dot(a, v.astype(BF16), preferred_element_type=F32)
        acc_ref[rows, :] = intra + state_step(s_ref, q, k, v, b, b_last)

    def direct_step(f_ref, lb, s_ref, acc_ref, c, masks, reverse):
        c_sz = HG_DIRECT_CHUNK
        _, tri = masks
        row1 = lax.broadcasted_iota(jnp.int32, (c_sz, 1), 0)
        rows = pl.ds(pl.multiple_of(c * c_sz, c_sz), c_sz)
        q, k, v, lf = chunk_inputs(f_ref, lb, rows)
        b = log_decay(lf, tri)
        b_last = b[0:1, :] if reverse else b[c_sz - 1:c_sz, :]
        o = state_step(s_ref, q, k, v, b, b_last)
        for s in range(c_sz):
            e = jnp.exp(jnp.minimum(b - b[s:s + 1, :], 0.0))
            a = jnp.sum(q * k[s:s + 1, :] * e, axis=-1, keepdims=True)
            a = jnp.where((row1 <= s) if reverse else (row1 >= s), a, 0.0)
            o = o + a * v[s:s + 1, :]
        acc_ref[rows, :] = o

    def scan_both(step, c_sz):
        nc = t // c_sz
        masks_f = _tri_masks(c_sz, False)
        masks_b = _tri_masks(c_sz, True)
        sf_ref[...] = jnp.zeros_like(sf_ref)
        sb_ref[...] = jnp.zeros_like(sb_ref)

        def body(ci, carry):
            step(ff_ref, lbf_ref[...], sf_ref, accf_ref, ci, masks_f, False)
            step(fb_ref, lbb_ref[...], sb_ref, accb_ref, nc - 1 - ci, masks_b, True)
            return carry

        lax.fori_loop(0, nc, body, 0, unroll=4)

    dev_ref[...] = jnp.zeros_like(dev_ref)
    scan_both(fast_step, HG_CHUNK)

    @pl.when(jnp.logical_not(jnp.max(dev_ref[...]) <= HG_MAX_LOG_RANGE))
    def _():
        scan_both(direct_step, HG_DIRECT_CHUNK)

    def finish(c, carry):
        rows = pl.ds(pl.multiple_of(c * 256, 256), 256)
        hg = hg_ref[0, rows, :]
        y = _rms_rows(accf_ref[rows, :] + accb_ref[rows, :], on_ref[...]) * (hg * jax.nn.sigmoid(hg))
        o_ref[0, rows, :] = y.astype(o_ref.dtype)
        return carry

    lax.fori_loop(0, t // 256, finish, 0)


def _hgrn2(proj, lb_f, lb_b, out_norm):
    b, t, _ = proj.shape

    def col(base):
        return pl.BlockSpec((1, t, LANES), lambda bi, h: (bi, 0, base + h))

    return pl.pallas_call(
        _hgrn2_kernel,
        grid=(b, HG_H),
        in_specs=[col(E_HQ), col(E_FF), col(E_FB), col(E_HI), col(E_HG),
                  pl.BlockSpec((1, LANES), lambda bi, h: (0, h)),
                  pl.BlockSpec((1, LANES), lambda bi, h: (0, h)),
                  pl.BlockSpec((1, LANES), lambda bi, h: (0, 0))],
        out_specs=pl.BlockSpec((1, t, LANES), lambda bi, h: (bi, 0, h)),
        out_shape=jax.ShapeDtypeStruct((b, t, HG_H * HG_DV), BF16),
        scratch_shapes=[pltpu.VMEM((t, HG_DV), F32), pltpu.VMEM((t, HG_DV), F32),
                        pltpu.VMEM((HG_DV, HG_DK), F32), pltpu.VMEM((HG_DV, HG_DK), F32), pltpu.VMEM((1, HG_DK), F32)],
        compiler_params=_cparams(("parallel", "parallel"), 10 * _nbytes((t, LANES), F32), 4 * _nbytes((t, LANES), F32)),
        name="hgrn2",
    )(proj, proj, proj, proj, proj, lb_f, lb_b, out_norm.reshape(1, HG_DV))


def _out_proj_kernel(a_ref, b_ref, wa_ref, wb_ref, *refs, bounds):
    x_refs, o_ref = refs[:-1], refs[-1]
    acc = jnp.dot(a_ref[...], wa_ref[...], preferred_element_type=F32)
    acc = acc + jnp.dot(b_ref[...], wb_ref[...], preferred_element_type=F32)
    for x_ref, bound in zip(x_refs, bounds):
        @pl.when(_in_part(pl.program_id(0), bound))
        def _():
            o_ref[...] = x_ref[...] + acc


def _out_proj(a, b, w, x_parts, tm=1024, tn=512):
    n, ka = a.shape
    d = w.shape[1]
    tm = min([tm] + [p.shape[0] for p in x_parts])
    x_specs, bounds = _row_parts(x_parts, tm, tn, lambda j: j)
    return pl.pallas_call(
        functools.partial(_out_proj_kernel, bounds=bounds),
        grid=(n // tm, d // tn),
        in_specs=[pl.BlockSpec((tm, ka), lambda i, j: (i, 0)),
                  pl.BlockSpec((tm, ka), lambda i, j: (i, 0)),
                  pl.BlockSpec((ka, tn), lambda i, j: (0, j)),
                  pl.BlockSpec((ka, tn), lambda i, j: (1, j))] + x_specs,
        out_specs=pl.BlockSpec((tm, tn), lambda i, j: (i, j)),
        out_shape=jax.ShapeDtypeStruct((n, d), F32),
        compiler_params=_cparams(("parallel", "arbitrary"), 4 * _nbytes((tm, ka), BF16), 4 * _nbytes((ka, tn), BF16),
                                 (3 + 2 * len(x_parts)) * _nbytes((tm, tn), F32)),
        name="out_proj",
    )(a, b, w, w, *x_parts)


def _ffn_kernel(x_ref, g_ref, wg_ref, wu_ref, wd_ref, o_ref, xn_ref):
    f = pl.program_id(1)

    @pl.when(f == 0)
    def _():
        _norm_into(xn_ref, x_ref, g_ref)
        o_ref[...] = x_ref[...]

    xn = xn_ref[...]
    gt = jnp.dot(xn, wg_ref[...], preferred_element_type=F32)
    up = jnp.dot(xn, wu_ref[...], preferred_element_type=F32)
    act = (gt * jax.nn.sigmoid(gt) * up).astype(BF16)
    o_ref[...] += jnp.dot(act, wd_ref[...], preferred_element_type=F32)


def _ffn(x, g, wg, wu, wd, tm=768, tf=512):
    n, d = x.shape
    ff = wg.shape[1]
    tm = min(tm, n)
    return pl.pallas_call(
        _ffn_kernel,
        grid=(n // tm, ff // tf),
        in_specs=[pl.BlockSpec((tm, d), lambda i, f: (i, 0)),
                  pl.BlockSpec((1, d), lambda i, f: (0, 0)),
                  pl.BlockSpec((d, tf), lambda i, f: (0, f)),
                  pl.BlockSpec((d, tf), lambda i, f: (0, f)),
                  pl.BlockSpec((tf, d), lambda i, f: (f, 0))],
        out_specs=pl.BlockSpec((tm, d), lambda i, f: (i, 0)),
        out_shape=jax.ShapeDtypeStruct((n, d), F32),
        scratch_shapes=[pltpu.VMEM((tm, d), BF16)],
        compiler_params=_cparams(("parallel", "arbitrary"), 4 * _nbytes((tm, d), F32), _nbytes((tm, d), BF16),
                                 6 * _nbytes((d, tf), BF16), 4 * _nbytes((tm, tf), F32)),
        name="ffn",
    )(x, g.reshape(1, d), wg, wu, wd)


def _axial_prep_kernel(qd_ref, kd_ref, vd_ref, cq_ref, sq_ref, ck_ref, sk_ref, q_ref, k_ref, v_ref, *, scale):
    def rope(x_ref, c_ref, s_ref, h, mul):
        x = x_ref[:, h * LANES:(h + 1) * LANES]
        r = lax.rsqrt(jnp.mean(x * x, axis=-1, keepdims=True) + EPS)
        return ((x * c_ref[...] + pltpu.roll(x, LANES // 2, axis=1) * s_ref[...]) * (r * mul)).astype(BF16)

    for h in range(AX_H):
        q_ref[:, h * LANES:(h + 1) * LANES] = rope(qd_ref, cq_ref, sq_ref, h, scale)
    for h in range(AX_KV):
        k_ref[:, h * LANES:(h + 1) * LANES] = rope(kd_ref, ck_ref, sk_ref, h, 1.0)
    v_ref[...] = vd_ref[...].astype(BF16)


def _axial_prep(proj, cq, sq, ck, sk, seq, tm=512):
    n = proj.shape[0]
    tm = min(tm, seq)
    nt = seq // tm
    qw, kw = AX_H * AX_HD, AX_KV * AX_HD
    tab = pl.BlockSpec((tm, LANES), lambda i: (i % nt, 0))
    return pl.pallas_call(
        functools.partial(_axial_prep_kernel, scale=float(AX_HD ** -0.5) * LOG2E),
        grid=(n // tm,),
        in_specs=[pl.BlockSpec((tm, qw), lambda i: (i, O_QD * LANES // qw)),
                  pl.BlockSpec((tm, kw), lambda i: (i, O_KD * LANES // kw)),
                  pl.BlockSpec((tm, kw), lambda i: (i, O_VD * LANES // kw)),
                  tab, tab, tab, tab],
        out_specs=[pl.BlockSpec((tm, qw), lambda i: (i, 0)),
                   pl.BlockSpec((tm, kw), lambda i: (i, 0)),
                   pl.BlockSpec((tm, kw), lambda i: (i, 0))],
        out_shape=[jax.ShapeDtypeStruct((n, qw), BF16), jax.ShapeDtypeStruct((n, kw), BF16),
                   jax.ShapeDtypeStruct((n, kw), BF16)],
        compiler_params=_cparams(("parallel",), 4 * _nbytes((tm, qw), F32), 6 * _nbytes((tm, kw), F32),
                                 8 * _nbytes((tm, LANES), F32), 2 * _nbytes((tm, qw + 2 * kw), BF16)),
        name="axial_prep",
    )(proj, proj, proj, cq, sq, ck, sk)


WIN_BLOCK = 128


def _window_kernel(sink_ref, slope_ref, q_ref, k_ref, v_ref, o_ref, *, scale):
    t = q_ref.shape[1]
    wb = WIN_BLOCK
    span = 3 * wb
    pair = pl.program_id(1)
    low = lax.broadcasted_iota(jnp.int32, (wb, LANES), 1) < WIN_HD
    row = lax.broadcasted_iota(jnp.int32, (2 * wb, span), 0)
    delta = (row & (wb - 1)) - lax.broadcasted_iota(jnp.int32, (2 * wb, span), 1)
    top = lax.broadcasted_iota(jnp.int32, (2 * wb, 1), 0) < wb
    slope = jnp.where(top, slope_ref[2 * pair], slope_ref[2 * pair + 1]) * LOG2E
    sink = jnp.where(top, sink_ref[2 * pair], sink_ref[2 * pair + 1]) * LOG2E

    def body(qb, carry):
        start = pl.multiple_of(jnp.clip((qb - 1) * wb, 0, t - span), wb)
        kwin = k_ref[0, pl.ds(start, span), :].astype(BF16)
        vwin = v_ref[0, pl.ds(start, span), :].astype(BF16)
        qrows = pl.ds(pl.multiple_of(qb * wb, wb), wb)
        q2 = q_ref[0, qrows, :] * (scale * LOG2E)
        qs = jnp.concatenate([jnp.where(low, q2, 0.0), jnp.where(low, 0.0, q2)], axis=0).astype(BF16)
        dist = jnp.abs(delta + (qb * wb - start))
        s = lax.dot_general(qs, kwin, (((1,), (1,)), ((), ())), preferred_element_type=F32)
        s = jnp.where(dist <= WINDOW, s - slope * dist.astype(F32), -jnp.inf)
        m = jnp.maximum(jnp.max(s, axis=-1, keepdims=True), sink)
        e = jnp.exp2(s - m)
        den = jnp.sum(e, axis=-1, keepdims=True) + jnp.exp2(sink - m)
        o = jnp.dot(e.astype(BF16), vwin, preferred_element_type=F32) / den
        o_ref[0, qrows, :] = jnp.where(low, o[:wb], o[wb:]).astype(o_ref.dtype)
        return carry

    lax.fori_loop(0, t // wb, body, 0, unroll=8)


def _window_attention(proj, sink, slopes):
    b, t, _ = proj.shape
    pairs = WIN_H // 2
    per_kv = pairs // WIN_KV
    smem = pl.BlockSpec(memory_space=pltpu.SMEM)
    return pl.pallas_call(
        functools.partial(_window_kernel, scale=float(WIN_HD ** -0.5)),
        grid=(b, pairs),
        in_specs=[smem, smem,
                  pl.BlockSpec((1, t, LANES), lambda bi, p: (bi, 0, O_QC + p)),
                  pl.BlockSpec((1, t, LANES), lambda bi, p: (bi, 0, O_KC + p // per_kv)),
                  pl.BlockSpec((1, t, LANES), lambda bi, p: (bi, 0, O_VC + p // per_kv))],
        out_specs=pl.BlockSpec((1, t, LANES), lambda bi, p: (bi, 0, p)),
        out_shape=jax.ShapeDtypeStruct((b, t, WIN_H * WIN_HD), BF16),
        compiler_params=_cparams(("parallel", "parallel"), 6 * _nbytes((t, LANES), F32), 2 * _nbytes((t, LANES), BF16)),
        name="window_attention",
    )(sink, slopes, proj, proj, proj)


MOE_TM = 512
R_E1, R_E2, R_W1, R_W2, R_RANK1, R_RANK2 = range(6)


def _router_kernel(x_ref, g_ref, r_ref, route_ref, cnt_ref, carry_ref):
    @pl.when(pl.program_id(0) == 0)
    def _():
        carry_ref[...] = jnp.zeros_like(carry_ref)

    tm = x_ref.shape[0]
    xn = _rms_rows(x_ref[...], g_ref[...])
    logits = [jnp.sum(xn * r_ref[e:e + 1, :], axis=-1, keepdims=True) for e in range(N_EXPERTS)]

    def top(ls):
        m = functools.reduce(jnp.maximum, ls)
        idx = jnp.full_like(m, N_EXPERTS).astype(jnp.int32)
        for e in reversed(range(N_EXPERTS)):
            idx = jnp.where(ls[e] == m, e, idx)
        return m, idx

    m1, i1 = top(logits)
    m2, i2 = top([jnp.where(i1 == e, -jnp.inf, logits[e]) for e in range(N_EXPERTS)])
    e2 = jnp.exp(m2 - m1)
    w1 = 1.0 / (1.0 + e2)
    w2 = e2 / (1.0 + e2)
    lane = lax.broadcasted_iota(jnp.int32, route_ref.shape, 1)
    sel1, sel2 = lane == i1, lane == i2
    onehot = jnp.where(jnp.logical_or(sel1, sel2), 1.0, 0.0)
    row = lax.broadcasted_iota(jnp.int32, (tm, tm), 0)
    col = lax.broadcasted_iota(jnp.int32, (tm, tm), 1)
    earlier = jnp.where(row > col, 1.0, 0.0).astype(BF16)
    before = carry_ref[...] + jnp.dot(earlier, onehot.astype(BF16), preferred_element_type=F32)
    rank1 = jnp.sum(jnp.where(sel1, before, 0.0), axis=-1, keepdims=True)
    rank2 = jnp.sum(jnp.where(sel2, before, 0.0), axis=-1, keepdims=True)
    carry_ref[...] = carry_ref[...] + jnp.sum(onehot, axis=0, keepdims=True)
    cnt_ref[...] = carry_ref[...]
    rec = jnp.zeros(route_ref.shape, F32)
    for pos, val in ((R_E1, i1.astype(F32)), (R_E2, i2.astype(F32)), (R_W1, w1), (R_W2, w2),
                     (R_RANK1, rank1), (R_RANK2, rank2)):
        rec = jnp.where(lane == pos, val, rec)
    route_ref[...] = rec


def _router(x, g, router_t, tm=256):
    n, d = x.shape
    tm = min(tm, n)
    return pl.pallas_call(
        _router_kernel,
        grid=(n // tm,),
        in_specs=[pl.BlockSpec((tm, d), lambda i: (i, 0)),
                  pl.BlockSpec((1, d), lambda i: (0, 0)),
                  pl.BlockSpec((N_EXPERTS, d), lambda i: (0, 0))],
        out_specs=[pl.BlockSpec((tm, LANES), lambda i: (i, 0)),
                   pl.BlockSpec((1, LANES), lambda i: (0, 0))],
        out_shape=[jax.ShapeDtypeStruct((n, LANES), F32), jax.ShapeDtypeStruct((1, LANES), F32)],
        scratch_shapes=[pltpu.VMEM((1, LANES), F32)],
        compiler_params=_cparams(("arbitrary",), 6 * _nbytes((tm, d), F32)),
        name="router",
    )(x, g.reshape(1, d), router_t)


def _row_copy(src_hbm, src_row, dst, dst_row, sem):
    return pltpu.make_async_copy(src_hbm.at[pl.ds(src_row, 1)], dst.at[pl.ds(dst_row, 1)], sem)


def _moe_scatter_kernel(fill_ref, p1_ref, p2_ref, x_ref, xs_hbm, zero_ref, sem, zsem, *, tile):
    i = pl.program_id(0)
    rows = p1_ref.shape[-1]

    @pl.when(i == 0)
    def _():
        zero_ref[...] = jnp.zeros_like(zero_ref)
        for k in range(fill_ref.shape[0]):
            @pl.when(fill_ref[k] >= 0)
            def _():
                start = pl.multiple_of(fill_ref[k], tile)
                fill = pltpu.make_async_copy(zero_ref, xs_hbm.at[pl.ds(start, tile)], zsem)
                fill.start()
                fill.wait()

    def issue(r, carry):
        _row_copy(x_ref, r, xs_hbm, p1_ref[0, 0, r], sem).start()
        _row_copy(x_ref, r, xs_hbm, p2_ref[0, 0, r], sem).start()
        return carry

    lax.fori_loop(0, rows, issue, 0, unroll=8)
    for _ in range(2):
        pltpu.make_async_copy(x_ref, xs_hbm.at[pl.ds(0, rows)], sem).wait()


def _moe_scatter(x, pos1, pos2, fill_rows, total_rows, tile, rows=256):
    n, d = x.shape
    rows = min(rows, n)
    smem_blk = pl.BlockSpec((1, 1, rows), lambda i, fill: (i, 0, 0), memory_space=pltpu.SMEM)
    return pl.pallas_call(
        functools.partial(_moe_scatter_kernel, tile=tile),
        grid_spec=pltpu.PrefetchScalarGridSpec(
            num_scalar_prefetch=1,
            grid=(n // rows,),
            in_specs=[smem_blk, smem_blk, pl.BlockSpec((rows, d), lambda i, fill: (i, 0))],
            out_specs=pl.BlockSpec(memory_space=pl.ANY),
            scratch_shapes=[pltpu.VMEM((tile, d), x.dtype), pltpu.SemaphoreType.DMA, pltpu.SemaphoreType.DMA]),
        out_shape=jax.ShapeDtypeStruct((total_rows, d), x.dtype),
        compiler_params=_cparams(("arbitrary",), _nbytes((tile, d), x.dtype), 2 * _nbytes((rows, d), x.dtype)),
        name="moe_scatter",
    )(fill_rows, pos1.reshape(n // rows, 1, rows), pos2.reshape(n // rows, 1, rows), x)


def _moe_experts_kernel(te_ref, nv_ref, xs_ref, g_ref, wg_ref, wu_ref, wd_ref, y_ref, xn_ref):
    valid = pl.program_id(0) < nv_ref[0]

    @pl.when(valid)
    def _():
        _norm_into(xn_ref, xs_ref, g_ref)
        xn = xn_ref[...]
        gt = jnp.dot(xn, wg_ref[0], preferred_element_type=F32)
        up = jnp.dot(xn, wu_ref[0], preferred_element_type=F32)
        act = (gt * jax.nn.sigmoid(gt) * up).astype(BF16)
        y_ref[...] = jnp.dot(act, wd_ref[0], preferred_element_type=F32)

    @pl.when(jnp.logical_not(valid))
    def _():
        y_ref[...] = jnp.zeros_like(y_ref)


def _moe_experts(xs, g, wg, wu, wd, tile_expert, n_valid, tile):
    rows, d = xs.shape
    ff = wg.shape[2]
    once = dict(pipeline_mode=pl.Buffered(1))

    def row_blk(i, te, nv):
        return (jnp.minimum(i, nv[0] - 1), 0)

    def w_blk(i, te, nv):
        return (te[i], 0, 0)

    return pl.pallas_call(
        _moe_experts_kernel,
        grid_spec=pltpu.PrefetchScalarGridSpec(
            num_scalar_prefetch=2,
            grid=(rows // tile,),
            in_specs=[pl.BlockSpec((tile, d), row_blk),
                      pl.BlockSpec((1, d), lambda i, te, nv: (0, 0)),
                      pl.BlockSpec((1, d, ff), w_blk, **once),
                      pl.BlockSpec((1, d, ff), w_blk, **once),
                      pl.BlockSpec((1, ff, d), w_blk, **once)],
            out_specs=pl.BlockSpec((tile, d), lambda i, te, nv: (i, 0)),
            scratch_shapes=[pltpu.VMEM((tile, d), BF16)]),
        out_shape=jax.ShapeDtypeStruct((rows, d), F32),
        compiler_params=_cparams(("arbitrary",), 4 * _nbytes((tile, d), F32), _nbytes((tile, d), BF16),
                                 3 * _nbytes((d, ff), BF16), 4 * _nbytes((tile, ff), F32), _nbytes((tile, d), F32)),
        name="moe_experts",
    )(tile_expert, n_valid, xs, g.reshape(1, d), wg, wu, wd)


def _moe_combine_kernel(p1_ref, p2_ref, x_ref, route_ref, g_ref, y_hbm, o_ref, ya_ref, yb_ref, sem, *, final):
    rows = x_ref.shape[0]

    def issue(r, carry):
        _row_copy(y_hbm, p1_ref[0, 0, r], ya_ref, r, sem).start()
        _row_copy(y_hbm, p2_ref[0, 0, r], yb_ref, r, sem).start()
        return carry

    lax.fori_loop(0, rows, issue, 0, unroll=8)
    for dst in (ya_ref, yb_ref):
        pltpu.make_async_copy(y_hbm.at[pl.ds(0, rows)], dst, sem).wait()
    lane = lax.broadcasted_iota(jnp.int32, route_ref.shape, 1)
    w1 = jnp.sum(jnp.where(lane == R_W1, route_ref[...], 0.0), axis=-1, keepdims=True)
    w2 = jnp.sum(jnp.where(lane == R_W2, route_ref[...], 0.0), axis=-1, keepdims=True)
    out = x_ref[...] + w1 * ya_ref[...] + w2 * yb_ref[...]
    o_ref[...] = _rms_rows(out, g_ref[...]) if final else out


def _moe_combine(x, route, pos1, pos2, y, gain, row0, nrows, final, rows=256):
    n, d = x.shape
    rows = min(rows, nrows)
    off = row0 // rows
    smem_blk = pl.BlockSpec((1, 1, rows), lambda i: (i + off, 0, 0), memory_space=pltpu.SMEM)
    return pl.pallas_call(
        functools.partial(_moe_combine_kernel, final=final),
        grid=(nrows // rows,),
        in_specs=[smem_blk, smem_blk,
                  pl.BlockSpec((rows, d), lambda i: (i + off, 0)),
                  pl.BlockSpec((rows, LANES), lambda i: (i + off, 0)),
                  pl.BlockSpec((1, d), lambda i: (0, 0)),
                  pl.BlockSpec(memory_space=pl.ANY)],
        out_specs=pl.BlockSpec((rows, d), lambda i: (i, 0)),
        out_shape=jax.ShapeDtypeStruct((nrows, d), F32),
        scratch_shapes=[pltpu.VMEM((rows, d), F32), pltpu.VMEM((rows, d), F32), pltpu.SemaphoreType.DMA],
        compiler_params=_cparams(("arbitrary",), 8 * _nbytes((rows, d), F32)),
        name="moe_combine",
    )(pos1.reshape(n // rows, 1, rows), pos2.reshape(n // rows, 1, rows), x, route, gain.reshape(1, d), y)


def _swap_halves(w, width):
    lead = w.shape[:-1]
    return jnp.flip(w.reshape(lead + (-1, 2, width // 2)), axis=-2).reshape(w.shape)


def _pad_cols(w, width):
    return jnp.pad(w, [(0, 0)] * (w.ndim - 1) + [(0, width - w.shape[-1])])


def _even_in_weight(w):
    kr = w[:, 2 * MLA_LORA:2 * MLA_LORA + MLA_ROPE]
    return jnp.concatenate([w[:, :2 * MLA_LORA], _pad_cols(kr, LANES), _pad_cols(_swap_halves(kr, MLA_ROPE), LANES),
                            w[:, 2 * MLA_LORA + MLA_ROPE:]], axis=1).astype(BF16)


def _mla_q_weight(w):
    w = w.reshape(MLA_LORA, MLA_H, MLA_NOPE + MLA_ROPE)
    rope = w[..., MLA_NOPE:]
    w = jnp.concatenate([w[..., :MLA_NOPE], _pad_cols(rope, LANES)], -1)
    return w.reshape(MLA_LORA, MLA_H * 2 * LANES).astype(BF16)


def _pairs_apart(w):
    quarter = AX_HD // 4
    g = w.reshape(w.shape[:-1] + (-1, 4, quarter))
    return g[..., jnp.array([0, 2, 1, 3]), :].reshape(w.shape)


def _odd_in_weight(w):
    d = w.shape[0]
    o = 0
    parts = {}
    for name, width in (("qc", WIN_H * WIN_HD), ("kc", WIN_KV * WIN_HD), ("vc", WIN_KV * WIN_HD),
                        ("qd", AX_H * AX_HD), ("kd", AX_KV * AX_HD), ("vd", AX_KV * AX_HD)):
        parts[name] = w[:, o:o + width]
        o += width

    def dup(x):
        x = x.reshape(d, WIN_KV, WIN_HD)
        return jnp.concatenate([x, x], axis=-1).reshape(d, WIN_KV * LANES)

    out = jnp.concatenate([_pairs_apart(parts["qd"]), parts["qc"], _pairs_apart(parts["kd"]), parts["vd"],
                           dup(parts["kc"]), dup(parts["vc"])], axis=1)
    assert out.shape[1] == O_END * LANES
    return out.astype(BF16)


def _rope_tables(pos, dim):
    inv = ROPE_THETA ** (-jnp.arange(0, dim, 2, dtype=F32) / dim)
    ang = pos[:, None] * inv[None, :]
    cos, sin = jnp.cos(ang), jnp.sin(ang)
    return jnp.concatenate([cos, cos], -1), jnp.concatenate([-sin, sin], -1)


def _alibi_slopes(n):
    return jnp.asarray(2.0 ** (-8.0 * np.arange(1, n + 1) / n), dtype=F32)


def _mixer_even(x_parts, bsz, seq, j, norm_g, w_in, q_norm, w_uq, kv_norm, w_ukv, lb_fwd, lb_bwd, out_norm, w_out):
    n = bsz * seq
    proj = _norm_matmul(x_parts, norm_g, _even_in_weight(w_in), tn=10 * LANES, out_dtype=F32)
    cos, sin = _rope_tables(jnp.arange(seq, dtype=F32), MLA_ROPE)
    q, k, v = _mla_up(proj, q_norm, kv_norm, _mla_q_weight(w_uq), w_ukv.astype(BF16),
                      _pad_cols(cos, LANES), _pad_cols(sin, LANES), seq)
    o_a = _attention(q.reshape(bsz, seq, -1), k.reshape(bsz, seq, -1), v.reshape(bsz, seq, -1), MLA_H, MLA_H)

    def lower_bound(tab):
        return jnp.cumsum(jax.nn.softmax(tab.astype(F32), axis=0), axis=0)[j].reshape(1, HG_H * HG_DK)

    o_b = _hgrn2(proj.reshape(bsz, seq, -1), lower_bound(lb_fwd), lower_bound(lb_bwd), out_norm)
    return _out_proj(o_a.reshape(n, -1), o_b.reshape(n, -1), w_out.astype(BF16), x_parts)


def _mixer_odd(x_parts, bsz, seq, norm_g, w_in, sink, q_norm, k_norm, w_out):
    n = bsz * seq
    proj = _norm_matmul(x_parts, norm_g, _odd_in_weight(w_in), tn=8 * LANES, out_dtype=F32)
    o_c = _window_attention(proj.reshape(bsz, seq, -1), sink.astype(F32), _alibi_slopes(WIN_H))
    pos = jnp.arange(seq)
    half = AX_HD // 2
    c_row, s_row = _rope_tables((pos // GRID_W).astype(F32), half)
    c_col, s_col = _rope_tables((pos % GRID_W).astype(F32), half)
    cos = jnp.concatenate([c_row, c_col], -1)
    sin = jnp.concatenate([s_row, s_col], -1)

    def tables(g):
        g = g.astype(F32)
        return _pairs_apart(cos * g[None, :]), _pairs_apart(sin * _swap_halves(g, half)[None, :])

    cq, sq = tables(q_norm)
    ck, sk = tables(k_norm)
    q, k, v = _axial_prep(proj, cq, sq, ck, sk, seq)
    o_d = _attention(q.reshape(bsz, seq, -1), k.reshape(bsz, seq, -1), v.reshape(bsz, seq, -1), AX_H, AX_KV)
    return _out_proj(o_c.reshape(n, -1), o_d.reshape(n, -1), w_out.astype(BF16), x_parts)


def _moe(x, norm_g, router, w_gate, w_up, w_down, out_gain, out_splits):
    n = x.shape[0]
    tile = MOE_TM
    route, counts = _router(x, norm_g, router.astype(F32).T)
    counts = counts[0, :N_EXPERTS].astype(jnp.int32)
    padded = (counts + tile - 1) // tile * tile
    ends = jnp.cumsum(padded)
    starts = ends - padded
    e1, e2 = route[:, R_E1].astype(jnp.int32), route[:, R_E2].astype(jnp.int32)
    pos1 = starts[e1] + route[:, R_RANK1].astype(jnp.int32)
    pos2 = starts[e2] + route[:, R_RANK2].astype(jnp.int32)
    n_tiles = 2 * n // tile + N_EXPERTS
    n_valid = ends[-1] // tile
    tile_row = jnp.minimum(jnp.arange(n_tiles), n_valid - 1) * tile
    tile_expert = jnp.sum(tile_row[:, None] >= ends[None, :], axis=1).astype(jnp.int32)
    group_tail = jnp.where(padded > 0, ends - tile, -1)
    spare = (n_valid + jnp.arange(N_EXPERTS)) * tile
    fill_rows = jnp.concatenate([group_tail, jnp.where(spare < n_tiles * tile, spare, -1)]).astype(jnp.int32)
    xs = _moe_scatter(x, pos1, pos2, fill_rows, n_tiles * tile, tile)
    y = _moe_experts(xs, norm_g, w_gate.astype(BF16), w_up.astype(BF16), w_down.astype(BF16),
                     tile_expert, n_valid.reshape(1).astype(jnp.int32), tile)
    final = out_gain is not None
    gain = out_gain if final else jnp.ones((x.shape[1],), F32)
    return [_moe_combine(x, route, pos1, pos2, y, gain, row0, nrows, final) for row0, nrows in out_splits]


def _trunk(x_parts, bsz, seq, norm_mix_e, w_in_e, mla_q_norm, mla_w_uq, mla_kv_norm, mla_w_ukv, hg_lb_fwd, hg_lb_bwd,
           hg_out_norm, w_out_e, norm_ffn_e, ffn_w_gate, ffn_w_up, ffn_w_down, norm_mix_o, w_in_o, win_sink,
           ax_q_norm, ax_k_norm, w_out_o, norm_ffn_o, moe_router, moe_w_gate, moe_w_up, moe_w_down, final_norm,
           out_splits):
    assert DEPTH % 2 == 0
    for l in range(DEPTH):
        j = l // 2
        if l % 2 == 0:
            x = _mixer_even(x_parts, bsz, seq, j, norm_mix_e[j], w_in_e[j], mla_q_norm[j], mla_w_uq[j],
                            mla_kv_norm[j], mla_w_ukv[j], hg_lb_fwd, hg_lb_bwd, hg_out_norm[j], w_out_e[j])
            x = _ffn(x, norm_ffn_e[j], ffn_w_gate[j].astype(BF16), ffn_w_up[j].astype(BF16),
                     ffn_w_down[j].astype(BF16))
        else:
            x = _mixer_odd(x_parts, bsz, seq, norm_mix_o[j], w_in_o[j], win_sink[j], ax_q_norm[j], ax_k_norm[j],
                           w_out_o[j])
            last = l == DEPTH - 1
            outs = _moe(x, norm_ffn_o[j], moe_router[j], moe_w_gate[j], moe_w_up[j], moe_w_down[j],
                        final_norm if last else None, out_splits if last else [(0, x.shape[0])])
            if last:
                return outs
            x = outs[0]
        x_parts = (x,)


def kernel(x_prompt, x_sample, norm_mix_e, w_in_e, mla_q_norm, mla_w_uq, mla_kv_norm, mla_w_ukv, hg_lb_fwd, hg_lb_bwd, hg_out_norm, w_out_e, norm_ffn_e, ffn_w_gate, ffn_w_up, ffn_w_down, norm_mix_o, w_in_o, win_sink, ax_q_norm, ax_k_norm, w_out_o, norm_ffn_o, moe_router, moe_w_gate, moe_w_up, moe_w_down, final_norm):
    bp, seq, d = x_prompt.shape
    bs = x_sample.shape[0]
    assert x_sample.shape[1:] == (seq, d)
    y_prompt, y_sample = _trunk(
        (x_prompt.reshape(bp * seq, d), x_sample.reshape(bs * seq, d)), bp + bs, seq, norm_mix_e, w_in_e, mla_q_norm, mla_w_uq, mla_kv_norm, mla_w_ukv, hg_lb_fwd, hg_lb_bwd,
        hg_out_norm, w_out_e, norm_ffn_e, ffn_w_gate, ffn_w_up, ffn_w_down, norm_mix_o, w_in_o, win_sink, ax_q_norm,
        ax_k_norm, w_out_o, norm_ffn_o, moe_router, moe_w_gate, moe_w_up, moe_w_down, final_norm,
        [(0, bp * seq), (bp * seq, bs * seq)])
    return (y_prompt.reshape(bp, seq, d), y_sample.reshape(bs, seq, d))
```

```python
import functools

import jax
import jax.numpy as jnp
import numpy as np
from jax import lax
from jax.experimental import pallas as pl
from jax.experimental.pallas import tpu as pltpu

D_MODEL = 2048
DEPTH = 2
GRID_W = 64
EPS = 1e-6
ROPE_THETA = 10000.0

MLA_H = 8
MLA_NOPE = 128
MLA_ROPE = 64
MLA_V = 128
MLA_LORA = D_MODEL // 4

HG_H = 8
HG_DK = 128
HG_DV = 128

WIN_H = 16
WIN_KV = 2
WIN_HD = 64
WINDOW = 128

AX_H = 8
AX_KV = 2
AX_HD = 128

FF_DENSE = 5632
N_EXPERTS = 8
FF_EXPERT = 1408

LANES = 128
VMEM_CAP = 60000 * 1024
BF16 = jnp.bfloat16
F32 = jnp.float32

E_CQ, E_CKV, E_KRA, E_KRB, E_HQ, E_FF, E_FB, E_HI, E_HG, E_END = 0, 4, 8, 9, 10, 18, 26, 34, 42, 50
O_QD, O_QC, O_KD, O_VD, O_KC, O_VC, O_END = 0, 8, 16, 18, 20, 22, 24


def _cparams(sem, *block_bytes):
    need = int(sum(block_bytes)) + (6 << 20)
    return pltpu.CompilerParams(dimension_semantics=sem, vmem_limit_bytes=min(max(need, 16 << 20), VMEM_CAP))


def _nbytes(shape, dtype):
    return int(np.prod(shape)) * jnp.dtype(dtype).itemsize


def _rms_rows(x, g):
    return x * lax.rsqrt(jnp.mean(x * x, axis=-1, keepdims=True) + EPS) * g


def _norm_into(dst_ref, x_ref, g_ref, chunk=256):
    rows = x_ref.shape[0]
    chunk = min(chunk, rows)

    def body(c, carry):
        r = pl.ds(pl.multiple_of(c * chunk, chunk), chunk)
        dst_ref[r, :] = _rms_rows(x_ref[r, :].astype(F32), g_ref[...]).astype(dst_ref.dtype)
        return carry

    lax.fori_loop(0, rows // chunk, body, 0)


def _row_parts(parts, tm, width, col):
    specs, bounds, start = [], [], 0
    for p in parts:
        nt = p.shape[0] // tm

        def index(i, j, start=start, nt=nt):
            inside = jnp.logical_and(i >= start, i < start + nt)
            return (jnp.clip(i - start, 0, nt - 1), jnp.where(inside, col(j), 0))

        specs.append(pl.BlockSpec((tm, width), index))
        bounds.append((start, start + nt))
        start += nt
    return specs, bounds


def _in_part(i, bound):
    return jnp.logical_and(i >= bound[0], i < bound[1])


def _norm_matmul_kernel(*refs, bounds):
    x_refs = refs[:len(bounds)]
    g_ref, w_ref, o_ref, xn_ref = refs[len(bounds):]
    for x_ref, bound in zip(x_refs, bounds):
        @pl.when(jnp.logical_and(pl.program_id(1) == 0, _in_part(pl.program_id(0), bound)))
        def _():
            _norm_into(xn_ref, x_ref, g_ref)

    o_ref[...] = jnp.dot(xn_ref[...], w_ref[...], preferred_element_type=F32).astype(o_ref.dtype)


def _norm_matmul(parts, g, w, tn, out_dtype):
    k = parts[0].shape[1]
    n = sum(p.shape[0] for p in parts)
    nout = w.shape[1]

    def resident(tm):
        return (2 * len(parts) * _nbytes((tm, k), F32), _nbytes((tm, k), BF16), 2 * _nbytes((k, tn), BF16),
                3 * _nbytes((tm, tn), F32))

    rows = min(p.shape[0] for p in parts)
    tm = next(t for t in (1024, 512, 256) if t <= rows and sum(resident(t)) + (8 << 20) <= VMEM_CAP)
    x_specs, bounds = _row_parts(parts, tm, k, lambda j: 0)
    return pl.pallas_call(
        functools.partial(_norm_matmul_kernel, bounds=bounds),
        grid=(n // tm, nout // tn),
        in_specs=x_specs + [pl.BlockSpec((1, k), lambda i, j: (0, 0)),
                            pl.BlockSpec((k, tn), lambda i, j: (0, j))],
        out_specs=pl.BlockSpec((tm, tn), lambda i, j: (i, j)),
        out_shape=jax.ShapeDtypeStruct((n, nout), out_dtype),
        scratch_shapes=[pltpu.VMEM((tm, k), BF16)],
        compiler_params=_cparams(("parallel", "arbitrary"), *resident(tm)),
        name="norm_matmul",
    )(*parts, g.reshape(1, k), w)


def _partner32(x):
    lane = lax.broadcasted_iota(jnp.int32, x.shape, 1)
    return jnp.where((lane & 32) == 0, pltpu.roll(x, LANES - 32, axis=1), pltpu.roll(x, 32, axis=1))


def _mla_up_kernel(cq_ref, ckv_ref, kra_ref, krb_ref, qn_ref, kvn_ref, wq_ref, wkv_ref, cos_ref, sin_ref,
                   q_ref, k_ref, v_ref, cqn_ref, ckvn_ref, *, scale):
    _norm_into(cqn_ref, cq_ref, qn_ref)
    _norm_into(ckvn_ref, ckv_ref, kvn_ref)
    cos, sin = cos_ref[...], sin_ref[...]
    k_rope = (kra_ref[...] * cos + krb_ref[...] * sin).astype(BF16)
    q = jnp.dot(cqn_ref[...], wq_ref[...], preferred_element_type=F32)
    kv = jnp.dot(ckvn_ref[...], wkv_ref[...], preferred_element_type=F32)
    for h in range(MLA_H):
        lo, mid, hi = 2 * h * LANES, (2 * h + 1) * LANES, (2 * h + 2) * LANES
        q_r = q[:, mid:hi]
        q_ref[:, lo:mid] = (q[:, lo:mid] * scale).astype(BF16)
        q_ref[:, mid:hi] = ((q_r * cos + _partner32(q_r) * sin) * scale).astype(BF16)
        k_ref[:, lo:mid] = kv[:, lo:mid].astype(BF16)
        k_ref[:, mid:hi] = k_rope
        v_ref[:, h * LANES:(h + 1) * LANES] = kv[:, mid:hi].astype(BF16)


def _mla_up(proj, qn, kvn, wq, wkv, cos_t, sin_t, seq, tm=512):
    n = proj.shape[0]
    tm = min(tm, seq)
    nt = seq // tm
    lora = MLA_LORA
    width = MLA_H * 2 * LANES
    scale = float((MLA_NOPE + MLA_ROPE) ** -0.5) * LOG2E
    return pl.pallas_call(
        functools.partial(_mla_up_kernel, scale=scale),
        grid=(n // tm,),
        in_specs=[pl.BlockSpec((tm, lora), lambda i: (i, 0)),
                  pl.BlockSpec((tm, lora), lambda i: (i, 1)),
                  pl.BlockSpec((tm, LANES), lambda i: (i, E_KRA)),
                  pl.BlockSpec((tm, LANES), lambda i: (i, E_KRB)),
                  pl.BlockSpec((1, lora), lambda i: (0, 0)),
                  pl.BlockSpec((1, lora), lambda i: (0, 0)),
                  pl.BlockSpec((lora, width), lambda i: (0, 0)),
                  pl.BlockSpec((lora, width), lambda i: (0, 0)),
                  pl.BlockSpec((tm, LANES), lambda i: (i % nt, 0)),
                  pl.BlockSpec((tm, LANES), lambda i: (i % nt, 0))],
        out_specs=[pl.BlockSpec((tm, width), lambda i: (i, 0)),
                   pl.BlockSpec((tm, width), lambda i: (i, 0)),
                   pl.BlockSpec((tm, MLA_H * LANES), lambda i: (i, 0))],
        out_shape=[jax.ShapeDtypeStruct((n, width), BF16),
                   jax.ShapeDtypeStruct((n, width), BF16),
                   jax.ShapeDtypeStruct((n, MLA_H * LANES), BF16)],
        scratch_shapes=[pltpu.VMEM((tm, lora), BF16), pltpu.VMEM((tm, lora), BF16)],
        compiler_params=_cparams(("parallel",), 4 * _nbytes((tm, lora), F32), 12 * _nbytes((tm, LANES), F32),
                                 4 * _nbytes((lora, width), BF16), 2 * _nbytes((tm, 5 * MLA_H * LANES), BF16),
                                 3 * _nbytes((tm, width), F32)),
        name="mla_up",
    )(proj, proj, proj, proj, qn.reshape(1, lora), kvn.reshape(1, lora), wq, wkv, cos_t, sin_t)


ATT_TK = 1024
ATT_SUB = 1024
LOG2E = 1.4426950408889634


def _attention_kernel(q_ref, k_ref, v_ref, o_ref):
    tq = q_ref.shape[1]
    t = k_ref.shape[1]
    for r0 in range(0, tq, ATT_SUB):
        q = q_ref[0, r0:r0 + ATT_SUB, :]
        m = l = acc = None
        for c0 in range(0, t, ATT_TK):
            s = lax.dot_general(q, k_ref[0, c0:c0 + ATT_TK, :], (((1,), (1,)), ((), ())),
                                preferred_element_type=F32)
            m_c = jnp.max(s, axis=-1, keepdims=True)
            m_new = m_c if m is None else jnp.maximum(m, m_c)
            p = jnp.exp2(s - m_new)
            pv = jnp.dot(p.astype(BF16), v_ref[0, c0:c0 + ATT_TK, :], preferred_element_type=F32)
            l_c = jnp.sum(p, axis=-1, keepdims=True)
            if m is None:
                l, acc = l_c, pv
            else:
                alpha = jnp.exp2(m - m_new)
                l, acc = alpha * l + l_c, alpha * acc + pv
            m = m_new
        o_ref[0, r0:r0 + ATT_SUB, :] = (acc / l).astype(o_ref.dtype)


def _attention(q, k, v, heads, kv_heads, tq=2048):
    b, t, _ = q.shape
    dq = q.shape[2] // heads
    dv = v.shape[2] // kv_heads
    g = heads // kv_heads
    tq = min(tq, t)
    return pl.pallas_call(
        _attention_kernel,
        grid=(b, heads, t // tq),
        in_specs=[pl.BlockSpec((1, tq, dq), lambda bi, h, qi: (bi, qi, h)),
                  pl.BlockSpec((1, t, dq), lambda bi, h, qi: (bi, 0, h // g)),
                  pl.BlockSpec((1, t, dv), lambda bi, h, qi: (bi, 0, h // g))],
        out_specs=pl.BlockSpec((1, tq, dv), lambda bi, h, qi: (bi, qi, h)),
        out_shape=jax.ShapeDtypeStruct((b, t, heads * dv), BF16),
        compiler_params=_cparams(("parallel", "parallel", "arbitrary"), 2 * _nbytes((tq, dq), BF16),
                                 2 * _nbytes((t, dq + dv), BF16), 2 * _nbytes((tq, dv), BF16),
                                 6 * _nbytes((ATT_SUB, ATT_TK), F32)),
        name="attention",
    )(q, k, v)


HG_CHUNK = 128
HG_DIRECT_CHUNK = 32
HG_MAX_LOG_RANGE = 80.0


def _split3(x):
    a = x.astype(BF16)
    r = x - a.astype(F32)
    b = r.astype(BF16)
    c = (r - b.astype(F32)).astype(BF16)
    return a, b, c


def _tri_masks(c_sz, reverse):
    row = lax.broadcasted_iota(jnp.int32, (c_sz, c_sz), 0)
    col = lax.broadcasted_iota(jnp.int32, (c_sz, c_sz), 1)
    keep = (row <= col) if reverse else (row >= col)
    return keep, jnp.where(keep, 1.0, 0.0).astype(BF16)


def _hgrn2_kernel(hq_ref, ff_ref, fb_ref, hi_ref, hg_ref, lbf_ref, lbb_ref, on_ref, o_ref,
                  accf_ref, accb_ref, sf_ref, sb_ref, dev_ref):
    t = hq_ref.shape[1]

    def chunk_inputs(f_ref, lb, rows):
        gate = lb + (1.0 - lb) * jax.nn.sigmoid(f_ref[0, rows, :])
        hq = hq_ref[0, rows, :]
        return hq * jax.nn.sigmoid(hq), 1.0 - gate, hi_ref[0, rows, :], jnp.log(gate)

    def log_decay(lf, tri):
        p0, p1, p2 = _split3(lf)
        return (jnp.dot(tri, p0, preferred_element_type=F32) + jnp.dot(tri, p1, preferred_element_type=F32)
                + jnp.dot(tri, p2, preferred_element_type=F32))

    def state_step(s_ref, q, k, v, b, b_last):
        s_t = s_ref[...]
        inter = lax.dot_general((q * jnp.exp(b)).astype(BF16), s_t.astype(BF16), (((1,), (1,)), ((), ())),
                                preferred_element_type=F32)
        kd = (k * jnp.exp(b_last - b)).astype(BF16)
        upd = lax.dot_general(v.astype(BF16), kd, (((0,), (0,)), ((), ())), preferred_element_type=F32)
        s_ref[...] = jnp.exp(b_last) * s_t + upd
        return inter

    def fast_step(f_ref, lb, s_ref, acc_ref, c, masks, reverse):
        c_sz = HG_CHUNK
        keep, tri = masks
        rows = pl.ds(pl.multiple_of(c * c_sz, c_sz), c_sz)
        q, k, v, lf = chunk_inputs(f_ref, lb, rows)
        b = log_decay(lf, tri)
        r = b[c_sz // 2:c_sz // 2 + 1, :]
        b_last = b[0:1, :] if reverse else b[c_sz - 1:c_sz, :]
        b_first = b[c_sz - 1:c_sz, :] if reverse else b[0:1, :]
        dev_ref[...] = jnp.maximum(dev_ref[...], jnp.maximum(jnp.abs(b_first - r), jnp.abs(b_last - r)))
        qe = (q * jnp.exp(b - r)).astype(BF16)
        ke = (k * jnp.exp(r - b)).astype(BF16)
        a = lax.dot_general(qe, ke, (((1,), (1,)), ((), ())), preferred_element_type=F32)
        a = jnp.where(keep, a, 0.0).astype(BF16)
        intra = jnp.dot(a, v.astype(BF16), preferred_element_type=F32)
        acc_ref[rows, :] = intra + state_step(s_ref, q, k, v, b, b_last)

    def direct_step(f_ref, lb, s_ref, acc_ref, c, masks, reverse):
        c_sz = HG_DIRECT_CHUNK
        _, tri = masks
        row1 = lax.broadcasted_iota(jnp.int32, (c_sz, 1), 0)
        rows = pl.ds(pl.multiple_of(c * c_sz, c_sz), c_sz)
        q, k, v, lf = chunk_inputs(f_ref, lb, rows)
        b = log_decay(lf, tri)
        b_last = b[0:1, :] if reverse else b[c_sz - 1:c_sz, :]
        o = state_step(s_ref, q, k, v, b, b_last)
        for s in range(c_sz):
            e = jnp.exp(jnp.minimum(b - b[s:s + 1, :], 0.0))
            a = jnp.sum(q * k[s:s + 1, :] * e, axis=-1, keepdims=True)
            a = jnp.where((row1 <= s) if reverse else (row1 >= s), a, 0.0)
            o = o + a * v[s:s + 1, :]
        acc_ref[rows, :] = o

    def scan_both(step, c_sz, unroll):
        nc = t // c_sz
        masks_f = _tri_masks(c_sz, False)
        masks_b = _tri_masks(c_sz, True)
        sf_ref[...] = jnp.zeros_like(sf_ref)
        sb_ref[...] = jnp.zeros_like(sb_ref)

        def body(ci, carry):
            step(ff_ref, lbf_ref[...], sf_ref, accf_ref, ci, masks_f, False)
            step(fb_ref, lbb_ref[...], sb_ref, accb_ref, nc - 1 - ci, masks_b, True)
            return carry

        lax.fori_loop(0, nc, body, 0, unroll=unroll)

    dev_ref[...] = jnp.zeros_like(dev_ref)
    scan_both(fast_step, HG_CHUNK, 8)

    @pl.when(jnp.logical_not(jnp.max(dev_ref[...]) <= HG_MAX_LOG_RANGE))
    def _():
        scan_both(direct_step, HG_DIRECT_CHUNK, 1)

    def finish(c, carry):
        rows = pl.ds(pl.multiple_of(c * 256, 256), 256)
        hg = hg_ref[0, rows, :]
        y = _rms_rows(accf_ref[rows, :] + accb_ref[rows, :], on_ref[...]) * (hg * jax.nn.sigmoid(hg))
        o_ref[0, rows, :] = y.astype(o_ref.dtype)
        return carry

    lax.fori_loop(0, t // 256, finish, 0, unroll=2)


def _hgrn2(proj, lb_f, lb_b, out_norm):
    b, t, _ = proj.shape

    def col(base):
        return pl.BlockSpec((1, t, LANES), lambda bi, h: (bi, 0, base + h))

    return pl.pallas_call(
        _hgrn2_kernel,
        grid=(b, HG_H),
        in_specs=[col(E_HQ), col(E_FF), col(E_FB), col(E_HI), col(E_HG),
                  pl.BlockSpec((1, LANES), lambda bi, h: (0, h)),
                  pl.BlockSpec((1, LANES), lambda bi, h: (0, h)),
                  pl.BlockSpec((1, LANES), lambda bi, h: (0, 0))],
        out_specs=pl.BlockSpec((1, t, LANES), lambda bi, h: (bi, 0, h)),
        out_shape=jax.ShapeDtypeStruct((b, t, HG_H * HG_DV), BF16),
        scratch_shapes=[pltpu.VMEM((t, HG_DV), F32), pltpu.VMEM((t, HG_DV), F32),
                        pltpu.VMEM((HG_DV, HG_DK), F32), pltpu.VMEM((HG_DV, HG_DK), F32), pltpu.VMEM((1, HG_DK), F32)],
        compiler_params=_cparams(("parallel", "parallel"), 10 * _nbytes((t, LANES), F32), 4 * _nbytes((t, LANES), F32)),
        name="hgrn2",
    )(proj, proj, proj, proj, proj, lb_f, lb_b, out_norm.reshape(1, HG_DV))


def _out_proj_kernel(a_ref, b_ref, wa_ref, wb_ref, *refs, bounds):
    x_refs, o_ref = refs[:-1], refs[-1]
    acc = jnp.dot(a_ref[...], wa_ref[...], preferred_element_type=F32)
    acc = acc + jnp.dot(b_ref[...], wb_ref[...], preferred_element_type=F32)
    for x_ref, bound in zip(x_refs, bounds):
        @pl.when(_in_part(pl.program_id(0), bound))
        def _():
            o_ref[...] = x_ref[...] + acc


def _out_proj(a, b, w, x_parts, tm=1024, tn=512):
    n, ka = a.shape
    d = w.shape[1]
    tm = min([tm] + [p.shape[0] for p in x_parts])
    x_specs, bounds = _row_parts(x_parts, tm, tn, lambda j: j)
    return pl.pallas_call(
        functools.partial(_out_proj_kernel, bounds=bounds),
        grid=(n // tm, d // tn),
        in_specs=[pl.BlockSpec((tm, ka), lambda i, j: (i, 0)),
                  pl.BlockSpec((tm, ka), lambda i, j: (i, 0)),
                  pl.BlockSpec((ka, tn), lambda i, j: (0, j)),
                  pl.BlockSpec((ka, tn), lambda i, j: (1, j))] + x_specs,
        out_specs=pl.BlockSpec((tm, tn), lambda i, j: (i, j)),
        out_shape=jax.ShapeDtypeStruct((n, d), F32),
        compiler_params=_cparams(("parallel", "arbitrary"), 4 * _nbytes((tm, ka), BF16), 4 * _nbytes((ka, tn), BF16),
                                 (3 + 2 * len(x_parts)) * _nbytes((tm, tn), F32)),
        name="out_proj",
    )(a, b, w, w, *x_parts)


def _ffn_kernel(x_ref, g_ref, wg_ref, wu_ref, wd_ref, o_ref, xn_ref):
    f = pl.program_id(1)

    @pl.when(f == 0)
    def _():
        _norm_into(xn_ref, x_ref, g_ref)
        o_ref[...] = x_ref[...]

    xn = xn_ref[...]
    gt = jnp.dot(xn, wg_ref[...], preferred_element_type=F32)
    up = jnp.dot(xn, wu_ref[...], preferred_element_type=F32)
    act = (gt * jax.nn.sigmoid(gt) * up).astype(BF16)
    o_ref[...] += jnp.dot(act, wd_ref[...], preferred_element_type=F32)


def _ffn(x, g, wg, wu, wd, tm=768, tf=512):
    n, d = x.shape
    ff = wg.shape[1]
    tm = min(tm, n)
    return pl.pallas_call(
        _ffn_kernel,
        grid=(n // tm, ff // tf),
        in_specs=[pl.BlockSpec((tm, d), lambda i, f: (i, 0)),
                  pl.BlockSpec((1, d), lambda i, f: (0, 0)),
                  pl.BlockSpec((d, tf), lambda i, f: (0, f)),
                  pl.BlockSpec((d, tf), lambda i, f: (0, f)),
                  pl.BlockSpec((tf, d), lambda i, f: (f, 0))],
        out_specs=pl.BlockSpec((tm, d), lambda i, f: (i, 0)),
        out_shape=jax.ShapeDtypeStruct((n, d), F32),
        scratch_shapes=[pltpu.VMEM((tm, d), BF16)],
        compiler_params=_cparams(("parallel", "arbitrary"), 4 * _nbytes((tm, d), F32), _nbytes((tm, d), BF16),
                                 6 * _nbytes((d, tf), BF16), 4 * _nbytes((tm, tf), F32)),
        name="ffn",
    )(x, g.reshape(1, d), wg, wu, wd)


def _axial_prep_kernel(qd_ref, kd_ref, vd_ref, cq_ref, sq_ref, ck_ref, sk_ref, q_ref, k_ref, v_ref, *, scale):
    def rope(x_ref, c_ref, s_ref, h, mul):
        x = x_ref[:, h * LANES:(h + 1) * LANES]
        r = lax.rsqrt(jnp.mean(x * x, axis=-1, keepdims=True) + EPS)
        return ((x * c_ref[...] + pltpu.roll(x, LANES // 2, axis=1) * s_ref[...]) * (r * mul)).astype(BF16)

    for h in range(AX_H):
        q_ref[:, h * LANES:(h + 1) * LANES] = rope(qd_ref, cq_ref, sq_ref, h, scale)
    for h in range(AX_KV):
        k_ref[:, h * LANES:(h + 1) * LANES] = rope(kd_ref, ck_ref, sk_ref, h, 1.0)
    v_ref[...] = vd_ref[...].astype(BF16)


def _axial_prep(proj, cq, sq, ck, sk, seq, tm=512):
    n = proj.shape[0]
    tm = min(tm, seq)
    nt = seq // tm
    qw, kw = AX_H * AX_HD, AX_KV * AX_HD
    tab = pl.BlockSpec((tm, LANES), lambda i: (i % nt, 0))
    return pl.pallas_call(
        functools.partial(_axial_prep_kernel, scale=float(AX_HD ** -0.5) * LOG2E),
        grid=(n // tm,),
        in_specs=[pl.BlockSpec((tm, qw), lambda i: (i, O_QD * LANES // qw)),
                  pl.BlockSpec((tm, kw), lambda i: (i, O_KD * LANES // kw)),
                  pl.BlockSpec((tm, kw), lambda i: (i, O_VD * LANES // kw)),
                  tab, tab, tab, tab],
        out_specs=[pl.BlockSpec((tm, qw), lambda i: (i, 0)),
                   pl.BlockSpec((tm, kw), lambda i: (i, 0)),
                   pl.BlockSpec((tm, kw), lambda i: (i, 0))],
        out_shape=[jax.ShapeDtypeStruct((n, qw), BF16), jax.ShapeDtypeStruct((n, kw), BF16),
                   jax.ShapeDtypeStruct((n, kw), BF16)],
        compiler_params=_cparams(("parallel",), 4 * _nbytes((tm, qw), F32), 6 * _nbytes((tm, kw), F32),
                                 8 * _nbytes((tm, LANES), F32), 2 * _nbytes((tm, qw + 2 * kw), BF16)),
        name="axial_prep",
    )(proj, proj, proj, cq, sq, ck, sk)


WIN_BLOCK = 128


def _window_kernel(sink_ref, slope_ref, q_ref, k_ref, v_ref, o_ref, *, scale):
    t = q_ref.shape[1]
    wb = WIN_BLOCK
    span = 3 * wb
    pair = pl.program_id(1)
    low = lax.broadcasted_iota(jnp.int32, (wb, LANES), 1) < WIN_HD
    row = lax.broadcasted_iota(jnp.int32, (2 * wb, span), 0)
    delta = (row & (wb - 1)) - lax.broadcasted_iota(jnp.int32, (2 * wb, span), 1)
    top = lax.broadcasted_iota(jnp.int32, (2 * wb, 1), 0) < wb
    slope = jnp.where(top, slope_ref[2 * pair], slope_ref[2 * pair + 1]) * LOG2E
    sink = jnp.where(top, sink_ref[2 * pair], sink_ref[2 * pair + 1]) * LOG2E

    def body(qb, carry):
        start = pl.multiple_of(jnp.clip((qb - 1) * wb, 0, t - span), wb)
        kwin = k_ref[0, pl.ds(start, span), :].astype(BF16)
        vwin = v_ref[0, pl.ds(start, span), :].astype(BF16)
        qrows = pl.ds(pl.multiple_of(qb * wb, wb), wb)
        q2 = q_ref[0, qrows, :] * (scale * LOG2E)
        qs = jnp.concatenate([jnp.where(low, q2, 0.0), jnp.where(low, 0.0, q2)], axis=0).astype(BF16)
        dist = jnp.abs(delta + (qb * wb - start))
        s = lax.dot_general(qs, kwin, (((1,), (1,)), ((), ())), preferred_element_type=F32)
        s = jnp.where(dist <= WINDOW, s - slope * dist.astype(F32), -jnp.inf)
        m = jnp.maximum(jnp.max(s, axis=-1, keepdims=True), sink)
        e = jnp.exp2(s - m)
        den = jnp.sum(e, axis=-1, keepdims=True) + jnp.exp2(sink - m)
        o = jnp.dot(e.astype(BF16), vwin, preferred_element_type=F32) / den
        o_ref[0, qrows, :] = jnp.where(low, o[:wb], o[wb:]).astype(o_ref.dtype)
        return carry

    lax.fori_loop(0, t // wb, body, 0, unroll=8)


def _window_attention(proj, sink, slopes):
    b, t, _ = proj.shape
    pairs = WIN_H // 2
    per_kv = pairs // WIN_KV
    smem = pl.BlockSpec(memory_space=pltpu.SMEM)
    return pl.pallas_call(
        functools.partial(_window_kernel, scale=float(WIN_HD ** -0.5)),
        grid=(b, pairs),
        in_specs=[smem, smem,
                  pl.BlockSpec((1, t, LANES), lambda bi, p: (bi, 0, O_QC + p)),
                  pl.BlockSpec((1, t, LANES), lambda bi, p: (bi, 0, O_KC + p // per_kv)),
                  pl.BlockSpec((1, t, LANES), lambda bi, p: (bi, 0, O_VC + p // per_kv))],
        out_specs=pl.BlockSpec((1, t, LANES), lambda bi, p: (bi, 0, p)),
        out_shape=jax.ShapeDtypeStruct((b, t, WIN_H * WIN_HD), BF16),
        compiler_params=_cparams(("parallel", "parallel"), 6 * _nbytes((t, LANES), F32), 2 * _nbytes((t, LANES), BF16)),
        name="window_attention",
    )(sink, slopes, proj, proj, proj)


MOE_TM = 512
R_E1, R_E2, R_W1, R_W2, R_RANK1, R_RANK2 = range(6)


def _router_kernel(x_ref, g_ref, r_ref, route_ref, cnt_ref, carry_ref):
    @pl.when(pl.program_id(0) == 0)
    def _():
        carry_ref[...] = jnp.zeros_like(carry_ref)

    tm = x_ref.shape[0]
    xn = _rms_rows(x_ref[...], g_ref[...])
    logits = [jnp.sum(xn * r_ref[e:e + 1, :], axis=-1, keepdims=True) for e in range(N_EXPERTS)]

    def top(ls):
        m = functools.reduce(jnp.maximum, ls)
        idx = jnp.full_like(m, N_EXPERTS).astype(jnp.int32)
        for e in reversed(range(N_EXPERTS)):
            idx = jnp.where(ls[e] == m, e, idx)
        return m, idx

    m1, i1 = top(logits)
    m2, i2 = top([jnp.where(i1 == e, -jnp.inf, logits[e]) for e in range(N_EXPERTS)])
    e2 = jnp.exp(m2 - m1)
    w1 = 1.0 / (1.0 + e2)
    w2 = e2 / (1.0 + e2)
    lane = lax.broadcasted_iota(jnp.int32, route_ref.shape, 1)
    sel1, sel2 = lane == i1, lane == i2
    onehot = jnp.where(jnp.logical_or(sel1, sel2), 1.0, 0.0)
    row = lax.broadcasted_iota(jnp.int32, (tm, tm), 0)
    col = lax.broadcasted_iota(jnp.int32, (tm, tm), 1)
    earlier = jnp.where(row > col, 1.0, 0.0).astype(BF16)
    before = carry_ref[...] + jnp.dot(earlier, onehot.astype(BF16), preferred_element_type=F32)
    rank1 = jnp.sum(jnp.where(sel1, before, 0.0), axis=-1, keepdims=True)
    rank2 = jnp.sum(jnp.where(sel2, before, 0.0), axis=-1, keepdims=True)
    carry_ref[...] = carry_ref[...] + jnp.sum(onehot, axis=0, keepdims=True)
    cnt_ref[...] = carry_ref[...]
    rec = jnp.zeros(route_ref.shape, F32)
    for pos, val in ((R_E1, i1.astype(F32)), (R_E2, i2.astype(F32)), (R_W1, w1), (R_W2, w2),
                     (R_RANK1, rank1), (R_RANK2, rank2)):
        rec = jnp.where(lane == pos, val, rec)
    route_ref[...] = rec


def _router(x, g, router_t, tm=256):
    n, d = x.shape
    tm = min(tm, n)
    return pl.pallas_call(
        _router_kernel,
        grid=(n // tm,),
        in_specs=[pl.BlockSpec((tm, d), lambda i: (i, 0)),
                  pl.BlockSpec((1, d), lambda i: (0, 0)),
                  pl.BlockSpec((N_EXPERTS, d), lambda i: (0, 0))],
        out_specs=[pl.BlockSpec((tm, LANES), lambda i: (i, 0)),
                   pl.BlockSpec((1, LANES), lambda i: (0, 0))],
        out_shape=[jax.ShapeDtypeStruct((n, LANES), F32), jax.ShapeDtypeStruct((1, LANES), F32)],
        scratch_shapes=[pltpu.VMEM((1, LANES), F32)],
        compiler_params=_cparams(("arbitrary",), 6 * _nbytes((tm, d), F32)),
        name="router",
    )(x, g.reshape(1, d), router_t)


def _row_copy(src_hbm, src_row, dst, dst_row, sem):
    return pltpu.make_async_copy(src_hbm.at[pl.ds(src_row, 1)], dst.at[pl.ds(dst_row, 1)], sem)


def _moe_scatter_kernel(fill_ref, p1_ref, p2_ref, x_ref, xs_hbm, zero_ref, sem, zsem, *, tile):
    i = pl.program_id(0)
    rows = p1_ref.shape[-1]

    @pl.when(i == 0)
    def _():
        zero_ref[...] = jnp.zeros_like(zero_ref)
        for k in range(fill_ref.shape[0]):
            @pl.when(fill_ref[k] >= 0)
            def _():
                start = pl.multiple_of(fill_ref[k], tile)
                fill = pltpu.make_async_copy(zero_ref, xs_hbm.at[pl.ds(start, tile)], zsem)
                fill.start()
                fill.wait()

    def issue(r, carry):
        _row_copy(x_ref, r, xs_hbm, p1_ref[0, 0, r], sem).start()
        _row_copy(x_ref, r, xs_hbm, p2_ref[0, 0, r], sem).start()
        return carry

    lax.fori_loop(0, rows, issue, 0, unroll=8)
    for _ in range(2):
        pltpu.make_async_copy(x_ref, xs_hbm.at[pl.ds(0, rows)], sem).wait()


def _moe_scatter(x, pos1, pos2, fill_rows, total_rows, tile, rows=256):
    n, d = x.shape
    rows = min(rows, n)
    smem_blk = pl.BlockSpec((1, 1, rows), lambda i, fill: (i, 0, 0), memory_space=pltpu.SMEM)
    return pl.pallas_call(
        functools.partial(_moe_scatter_kernel, tile=tile),
        grid_spec=pltpu.PrefetchScalarGridSpec(
            num_scalar_prefetch=1,
            grid=(n // rows,),
            in_specs=[smem_blk, smem_blk, pl.BlockSpec((rows, d), lambda i, fill: (i, 0))],
            out_specs=pl.BlockSpec(memory_space=pl.ANY),
            scratch_shapes=[pltpu.VMEM((tile, d), x.dtype), pltpu.SemaphoreType.DMA, pltpu.SemaphoreType.DMA]),
        out_shape=jax.ShapeDtypeStruct((total_rows, d), x.dtype),
        compiler_params=_cparams(("arbitrary",), _nbytes((tile, d), x.dtype), 2 * _nbytes((rows, d), x.dtype)),
        name="moe_scatter",
    )(fill_rows, pos1.reshape(n // rows, 1, rows), pos2.reshape(n // rows, 1, rows), x)


def _moe_experts_kernel(te_ref, nv_ref, xs_ref, g_ref, wg_ref, wu_ref, wd_ref, y_ref, xn_ref):
    valid = pl.program_id(0) < nv_ref[0]

    @pl.when(valid)
    def _():
        _norm_into(xn_ref, xs_ref, g_ref)
        xn = xn_ref[...]
        gt = jnp.dot(xn, wg_ref[0], preferred_element_type=F32)
        up = jnp.dot(xn, wu_ref[0], preferred_element_type=F32)
        act = (gt * jax.nn.sigmoid(gt) * up).astype(BF16)
        y_ref[...] = jnp.dot(act, wd_ref[0], preferred_element_type=F32)

    @pl.when(jnp.logical_not(valid))
    def _():
        y_ref[...] = jnp.zeros_like(y_ref)


def _moe_experts(xs, g, wg, wu, wd, tile_expert, n_valid, tile):
    rows, d = xs.shape
    ff = wg.shape[2]
    once = dict(pipeline_mode=pl.Buffered(1))

    def row_blk(i, te, nv):
        return (jnp.minimum(i, nv[0] - 1), 0)

    def w_blk(i, te, nv):
        return (te[i], 0, 0)

    return pl.pallas_call(
        _moe_experts_kernel,
        grid_spec=pltpu.PrefetchScalarGridSpec(
            num_scalar_prefetch=2,
            grid=(rows // tile,),
            in_specs=[pl.BlockSpec((tile, d), row_blk),
                      pl.BlockSpec((1, d), lambda i, te, nv: (0, 0)),
                      pl.BlockSpec((1, d, ff), w_blk, **once),
                      pl.BlockSpec((1, d, ff), w_blk, **once),
                      pl.BlockSpec((1, ff, d), w_blk, **once)],
            out_specs=pl.BlockSpec((tile, d), lambda i, te, nv: (i, 0)),
            scratch_shapes=[pltpu.VMEM((tile, d), BF16)]),
        out_shape=jax.ShapeDtypeStruct((rows, d), F32),
        compiler_params=_cparams(("arbitrary",), 4 * _nbytes((tile, d), F32), _nbytes((tile, d), BF16),
                                 3 * _nbytes((d, ff), BF16), 4 * _nbytes((tile, ff), F32), _nbytes((tile, d), F32)),
        name="moe_experts",
    )(tile_expert, n_valid, xs, g.reshape(1, d), wg, wu, wd)


def _moe_combine_kernel(p1_ref, p2_ref, x_ref, route_ref, g_ref, y_hbm, o_ref, ya_ref, yb_ref, sem, *, final):
    rows = x_ref.shape[0]

    def issue(r, carry):
        _row_copy(y_hbm, p1_ref[0, 0, r], ya_ref, r, sem).start()
        _row_copy(y_hbm, p2_ref[0, 0, r], yb_ref, r, sem).start()
        return carry

    lax.fori_loop(0, rows, issue, 0, unroll=8)
    for dst in (ya_ref, yb_ref):
        pltpu.make_async_copy(y_hbm.at[pl.ds(0, rows)], dst, sem).wait()
    lane = lax.broadcasted_iota(jnp.int32, route_ref.shape, 1)
    w1 = jnp.sum(jnp.where(lane == R_W1, route_ref[...], 0.0), axis=-1, keepdims=True)
    w2 = jnp.sum(jnp.where(lane == R_W2, route_ref[...], 0.0), axis=-1, keepdims=True)
    out = x_ref[...] + w1 * ya_ref[...] + w2 * yb_ref[...]
    o_ref[...] = _rms_rows(out, g_ref[...]) if final else out


def _moe_combine(x, route, pos1, pos2, y, gain, row0, nrows, final, rows=256):
    n, d = x.shape
    rows = min(rows, nrows)
    off = row0 // rows
    smem_blk = pl.BlockSpec((1, 1, rows), lambda i: (i + off, 0, 0), memory_space=pltpu.SMEM)
    return pl.pallas_call(
        functools.partial(_moe_combine_kernel, final=final),
        grid=(nrows // rows,),
        in_specs=[smem_blk, smem_blk,
                  pl.BlockSpec((rows, d), lambda i: (i + off, 0)),
                  pl.BlockSpec((rows, LANES), lambda i: (i + off, 0)),
                  pl.BlockSpec((1, d), lambda i: (0, 0)),
                  pl.BlockSpec(memory_space=pl.ANY)],
        out_specs=pl.BlockSpec((rows, d), lambda i: (i, 0)),
        out_shape=jax.ShapeDtypeStruct((nrows, d), F32),
        scratch_shapes=[pltpu.VMEM((rows, d), F32), pltpu.VMEM((rows, d), F32), pltpu.SemaphoreType.DMA],
        compiler_params=_cparams(("arbitrary",), 8 * _nbytes((rows, d), F32)),
        name="moe_combine",
    )(pos1.reshape(n // rows, 1, rows), pos2.reshape(n // rows, 1, rows), x, route, gain.reshape(1, d), y)


def _swap_halves(w, width):
    lead = w.shape[:-1]
    return jnp.flip(w.reshape(lead + (-1, 2, width // 2)), axis=-2).reshape(w.shape)


def _pad_cols(w, width):
    return jnp.pad(w, [(0, 0)] * (w.ndim - 1) + [(0, width - w.shape[-1])])


def _even_in_weight(w):
    kr = w[:, 2 * MLA_LORA:2 * MLA_LORA + MLA_ROPE]
    return jnp.concatenate([w[:, :2 * MLA_LORA], _pad_cols(kr, LANES), _pad_cols(_swap_halves(kr, MLA_ROPE), LANES),
                            w[:, 2 * MLA_LORA + MLA_ROPE:]], axis=1).astype(BF16)


def _mla_q_weight(w):
    w = w.reshape(MLA_LORA, MLA_H, MLA_NOPE + MLA_ROPE)
    rope = w[..., MLA_NOPE:]
    w = jnp.concatenate([w[..., :MLA_NOPE], _pad_cols(rope, LANES)], -1)
    return w.reshape(MLA_LORA, MLA_H * 2 * LANES).astype(BF16)


def _pairs_apart(w):
    quarter = AX_HD // 4
    g = w.reshape(w.shape[:-1] + (-1, 4, quarter))
    return g[..., jnp.array([0, 2, 1, 3]), :].reshape(w.shape)


def _odd_in_weight(w):
    d = w.shape[0]
    o = 0
    parts = {}
    for name, width in (("qc", WIN_H * WIN_HD), ("kc", WIN_KV * WIN_HD), ("vc", WIN_KV * WIN_HD),
                        ("qd", AX_H * AX_HD), ("kd", AX_KV * AX_HD), ("vd", AX_KV * AX_HD)):
        parts[name] = w[:, o:o + width]
        o += width

    def dup(x):
        x = x.reshape(d, WIN_KV, WIN_HD)
        return jnp.concatenate([x, x], axis=-1).reshape(d, WIN_KV * LANES)

    out = jnp.concatenate([_pairs_apart(parts["qd"]), parts["qc"], _pairs_apart(parts["kd"]), parts["vd"],
                           dup(parts["kc"]), dup(parts["vc"])], axis=1)
    assert out.shape[1] == O_END * LANES
    return out.astype(BF16)


def _rope_tables(pos, dim):
    inv = ROPE_THETA ** (-jnp.arange(0, dim, 2, dtype=F32) / dim)
    ang = pos[:, None] * inv[None, :]
    cos, sin = jnp.cos(ang), jnp.sin(ang)
    return jnp.concatenate([cos, cos], -1), jnp.concatenate([-sin, sin], -1)


def _alibi_slopes(n):
    return jnp.asarray(2.0 ** (-8.0 * np.arange(1, n + 1) / n), dtype=F32)


def _mixer_even(x_parts, bsz, seq, j, norm_g, w_in, q_norm, w_uq, kv_norm, w_ukv, lb_fwd, lb_bwd, out_norm, w_out):
    n = bsz * seq
    proj = _norm_matmul(x_parts, norm_g, _even_in_weight(w_in), tn=10 * LANES, out_dtype=F32)
    cos, sin = _rope_tables(jnp.arange(seq, dtype=F32), MLA_ROPE)
    q, k, v = _mla_up(proj, q_norm, kv_norm, _mla_q_weight(w_uq), w_ukv.astype(BF16),
                      _pad_cols(cos, LANES), _pad_cols(sin, LANES), seq)
    o_a = _attention(q.reshape(bsz, seq, -1), k.reshape(bsz, seq, -1), v.reshape(bsz, seq, -1), MLA_H, MLA_H)

    def lower_bound(tab):
        return jnp.cumsum(jax.nn.softmax(tab.astype(F32), axis=0), axis=0)[j].reshape(1, HG_H * HG_DK)

    o_b = _hgrn2(proj.reshape(bsz, seq, -1), lower_bound(lb_fwd), lower_bound(lb_bwd), out_norm)
    return _out_proj(o_a.reshape(n, -1), o_b.reshape(n, -1), w_out.astype(BF16), x_parts)


def _mixer_odd(x_parts, bsz, seq, norm_g, w_in, sink, q_norm, k_norm, w_out):
    n = bsz * seq
    proj = _norm_matmul(x_parts, norm_g, _odd_in_weight(w_in), tn=8 * LANES, out_dtype=F32)
    o_c = _window_attention(proj.reshape(bsz, seq, -1), sink.astype(F32), _alibi_slopes(WIN_H))
    pos = jnp.arange(seq)
    half = AX_HD // 2
    c_row, s_row = _rope_tables((pos // GRID_W).astype(F32), half)
    c_col, s_col = _rope_tables((pos % GRID_W).astype(F32), half)
    cos = jnp.concatenate([c_row, c_col], -1)
    sin = jnp.concatenate([s_row, s_col], -1)

    def tables(g):
        g = g.astype(F32)
        return _pairs_apart(cos * g[None, :]), _pairs_apart(sin * _swap_halves(g, half)[None, :])

    cq, sq = tables(q_norm)
    ck, sk = tables(k_norm)
    q, k, v = _axial_prep(proj, cq, sq, ck, sk, seq)
    o_d = _attention(q.reshape(bsz, seq, -1), k.reshape(bsz, seq, -1), v.reshape(bsz, seq, -1), AX_H, AX_KV)
    return _out_proj(o_c.reshape(n, -1), o_d.reshape(n, -1), w_out.astype(BF16), x_parts)


def _moe(x, norm_g, router, w_gate, w_up, w_down, out_gain, out_splits):
    n = x.shape[0]
    tile = MOE_TM
    route, counts = _router(x, norm_g, router.astype(F32).T)
    counts = counts[0, :N_EXPERTS].astype(jnp.int32)
    padded = (counts + tile - 1) // tile * tile
    ends = jnp.cumsum(padded)
    starts = ends - padded
    e1, e2 = route[:, R_E1].astype(jnp.int32), route[:, R_E2].astype(jnp.int32)
    pos1 = starts[e1] + route[:, R_RANK1].astype(jnp.int32)
    pos2 = starts[e2] + route[:, R_RANK2].astype(jnp.int32)
    n_tiles = 2 * n // tile + N_EXPERTS
    n_valid = ends[-1] // tile
    tile_row = jnp.minimum(jnp.arange(n_tiles), n_valid - 1) * tile
    tile_expert = jnp.sum(tile_row[:, None] >= ends[None, :], axis=1).astype(jnp.int32)
    group_tail = jnp.where(padded > 0, ends - tile, -1)
    spare = (n_valid + jnp.arange(N_EXPERTS)) * tile
    fill_rows = jnp.concatenate([group_tail, jnp.where(spare < n_tiles * tile, spare, -1)]).astype(jnp.int32)
    xs = _moe_scatter(x, pos1, pos2, fill_rows, n_tiles * tile, tile)
    y = _moe_experts(xs, norm_g, w_gate.astype(BF16), w_up.astype(BF16), w_down.astype(BF16),
                     tile_expert, n_valid.reshape(1).astype(jnp.int32), tile)
    final = out_gain is not None
    gain = out_gain if final else jnp.ones((x.shape[1],), F32)
    return [_moe_combine(x, route, pos1, pos2, y, gain, row0, nrows, final) for row0, nrows in out_splits]


def _trunk(x_parts, bsz, seq, norm_mix_e, w_in_e, mla_q_norm, mla_w_uq, mla_kv_norm, mla_w_ukv, hg_lb_fwd, hg_lb_bwd,
           hg_out_norm, w_out_e, norm_ffn_e, ffn_w_gate, ffn_w_up, ffn_w_down, norm_mix_o, w_in_o, win_sink,
           ax_q_norm, ax_k_norm, w_out_o, norm_ffn_o, moe_router, moe_w_gate, moe_w_up, moe_w_down, final_norm,
           out_splits):
    assert DEPTH % 2 == 0
    for l in range(DEPTH):
        j = l // 2
        if l % 2 == 0:
            x = _mixer_even(x_parts, bsz, seq, j, norm_mix_e[j], w_in_e[j], mla_q_norm[j], mla_w_uq[j],
                            mla_kv_norm[j], mla_w_ukv[j], hg_lb_fwd, hg_lb_bwd, hg_out_norm[j], w_out_e[j])
            x = _ffn(x, norm_ffn_e[j], ffn_w_gate[j].astype(BF16), ffn_w_up[j].astype(BF16),
                     ffn_w_down[j].astype(BF16))
        else:
            x = _mixer_odd(x_parts, bsz, seq, norm_mix_o[j], w_in_o[j], win_sink[j], ax_q_norm[j], ax_k_norm[j],
                           w_out_o[j])
            last = l == DEPTH - 1
            outs = _moe(x, norm_ffn_o[j], moe_router[j], moe_w_gate[j], moe_w_up[j], moe_w_down[j],
                        final_norm if last else None, out_splits if last else [(0, x.shape[0])])
            if last:
                return outs
            x = outs[0]
        x_parts = (x,)


def kernel(x_prompt, x_sample, norm_mix_e, w_in_e, mla_q_norm, mla_w_uq, mla_kv_norm, mla_w_ukv, hg_lb_fwd, hg_lb_bwd, hg_out_norm, w_out_e, norm_ffn_e, ffn_w_gate, ffn_w_up, ffn_w_down, norm_mix_o, w_in_o, win_sink, ax_q_norm, ax_k_norm, w_out_o, norm_ffn_o, moe_router, moe_w_gate, moe_w_up, moe_w_down, final_norm):
    bp, seq, d = x_prompt.shape
    bs = x_sample.shape[0]
    assert x_sample.shape[1:] == (seq, d)
    y_prompt, y_sample = _trunk(
        (x_prompt.reshape(bp * seq, d), x_sample.reshape(bs * seq, d)), bp + bs, seq, norm_mix_e, w_in_e, mla_q_norm, mla_w_uq, mla_kv_norm, mla_w_ukv, hg_lb_fwd, hg_lb_bwd,
        hg_out_norm, w_out_e, norm_ffn_e, ffn_w_gate, ffn_w_up, ffn_w_down, norm_mix_o, w_in_o, win_sink, ax_q_norm,
        ax_k_norm, w_out_o, norm_ffn_o, moe_router, moe_w_gate, moe_w_up, moe_w_down, final_norm,
        [(0, bp * seq), (bp * seq, bs * seq)])
    return (y_prompt.reshape(bp, seq, d), y_sample.reshape(bs, seq, d))
```

```python
import functools

import jax
import jax.numpy as jnp
import numpy as np
from jax import lax
from jax.experimental import pallas as pl
from jax.experimental.pallas import tpu as pltpu

D_MODEL = 2048
DEPTH = 2
GRID_W = 64
EPS = 1e-6
ROPE_THETA = 10000.0

MLA_H = 8
MLA_NOPE = 128
MLA_ROPE = 64
MLA_V = 128
MLA_LORA = D_MODEL // 4

HG_H = 8
HG_DK = 128
HG_DV = 128

WIN_H = 16
WIN_KV = 2
WIN_HD = 64
WINDOW = 128

AX_H = 8
AX_KV = 2
AX_HD = 128

FF_DENSE = 5632
N_EXPERTS = 8
FF_EXPERT = 1408

LANES = 128
VMEM_CAP = 60000 * 1024
BF16 = jnp.bfloat16
F32 = jnp.float32

E_CQ, E_CKV, E_KRA, E_KRB, E_HQ, E_FF, E_FB, E_HI, E_HG, E_END = 0, 4, 8, 9, 10, 18, 26, 34, 42, 50
O_QD, O_QC, O_KD, O_VD, O_KC, O_VC, O_END = 0, 8, 16, 18, 20, 22, 24


def _cparams(sem, *block_bytes):
    need = int(sum(block_bytes)) + (6 << 20)
    return pltpu.CompilerParams(dimension_semantics=sem, vmem_limit_bytes=min(max(need, 16 << 20), VMEM_CAP))


def _nbytes(shape, dtype):
    return int(np.prod(shape)) * jnp.dtype(dtype).itemsize


def _rms_rows(x, g):
    return x * lax.rsqrt(jnp.mean(x * x, axis=-1, keepdims=True) + EPS) * g


def _norm_into(dst_ref, x_ref, g_ref, chunk=256):
    rows = x_ref.shape[0]
    chunk = min(chunk, rows)

    def body(c, carry):
        r = pl.ds(pl.multiple_of(c * chunk, chunk), chunk)
        dst_ref[r, :] = _rms_rows(x_ref[r, :].astype(F32), g_ref[...]).astype(dst_ref.dtype)
        return carry

    lax.fori_loop(0, rows // chunk, body, 0)


def _row_parts(parts, tm, width, col):
    specs, bounds, start = [], [], 0
    for p in parts:
        nt = p.shape[0] // tm

        def index(i, j, start=start, nt=nt):
            inside = jnp.logical_and(i >= start, i < start + nt)
            return (jnp.clip(i - start, 0, nt - 1), jnp.where(inside, col(j), 0))

        specs.append(pl.BlockSpec((tm, width), index))
        bounds.append((start, start + nt))
        start += nt
    return specs, bounds


def _in_part(i, bound):
    return jnp.logical_and(i >= bound[0], i < bound[1])


def _norm_matmul_kernel(*refs, bounds):
    x_refs = refs[:len(bounds)]
    g_ref, w_ref, o_ref, xn_ref = refs[len(bounds):]
    for x_ref, bound in zip(x_refs, bounds):
        @pl.when(jnp.logical_and(pl.program_id(1) == 0, _in_part(pl.program_id(0), bound)))
        def _():
            _norm_into(xn_ref, x_ref, g_ref)

    o_ref[...] = jnp.dot(xn_ref[...], w_ref[...], preferred_element_type=F32).astype(o_ref.dtype)


def _norm_matmul(parts, g, w, tn, out_dtype):
    k = parts[0].shape[1]
    n = sum(p.shape[0] for p in parts)
    nout = w.shape[1]

    def resident(tm):
        return (2 * len(parts) * _nbytes((tm, k), F32), _nbytes((tm, k), BF16), 2 * _nbytes((k, tn), BF16),
                3 * _nbytes((tm, tn), F32))

    rows = min(p.shape[0] for p in parts)
    tm = next(t for t in (1024, 512, 256) if t <= rows and sum(resident(t)) + (8 << 20) <= VMEM_CAP)
    x_specs, bounds = _row_parts(parts, tm, k, lambda j: 0)
    return pl.pallas_call(
        functools.partial(_norm_matmul_kernel, bounds=bounds),
        grid=(n // tm, nout // tn),
        in_specs=x_specs + [pl.BlockSpec((1, k), lambda i, j: (0, 0)),
                            pl.BlockSpec((k, tn), lambda i, j: (0, j))],
        out_specs=pl.BlockSpec((tm, tn), lambda i, j: (i, j)),
        out_shape=jax.ShapeDtypeStruct((n, nout), out_dtype),
        scratch_shapes=[pltpu.VMEM((tm, k), BF16)],
        compiler_params=_cparams(("parallel", "arbitrary"), *resident(tm)),
        name="norm_matmul",
    )(*parts, g.reshape(1, k), w)


def _partner32(x):
    lane = lax.broadcasted_iota(jnp.int32, x.shape, 1)
    return jnp.where((lane & 32) == 0, pltpu.roll(x, LANES - 32, axis=1), pltpu.roll(x, 32, axis=1))


def _mla_up_kernel(cq_ref, ckv_ref, kra_ref, krb_ref, qn_ref, kvn_ref, wq_ref, wkv_ref, cos_ref, sin_ref,
                   q_ref, k_ref, v_ref, cqn_ref, ckvn_ref, *, scale):
    _norm_into(cqn_ref, cq_ref, qn_ref)
    _norm_into(ckvn_ref, ckv_ref, kvn_ref)
    cos, sin = cos_ref[...], sin_ref[...]
    k_rope = (kra_ref[...] * cos + krb_ref[...] * sin).astype(BF16)
    q = jnp.dot(cqn_ref[...], wq_ref[...], preferred_element_type=F32)
    kv = jnp.dot(ckvn_ref[...], wkv_ref[...], preferred_element_type=F32)
    for h in range(MLA_H):
        lo, mid, hi = 2 * h * LANES, (2 * h + 1) * LANES, (2 * h + 2) * LANES
        q_r = q[:, mid:hi]
        q_ref[:, lo:mid] = (q[:, lo:mid] * scale).astype(BF16)
        q_ref[:, mid:hi] = ((q_r * cos + _partner32(q_r) * sin) * scale).astype(BF16)
        k_ref[:, lo:mid] = kv[:, lo:mid].astype(BF16)
        k_ref[:, mid:hi] = k_rope
        v_ref[:, h * LANES:(h + 1) * LANES] = kv[:, mid:hi].astype(BF16)


def _mla_up(proj, qn, kvn, wq, wkv, cos_t, sin_t, seq, tm=512):
    n = proj.shape[0]
    tm = min(tm, seq)
    nt = seq // tm
    lora = MLA_LORA
    width = MLA_H * 2 * LANES
    scale = float((MLA_NOPE + MLA_ROPE) ** -0.5) * LOG2E
    return pl.pallas_call(
        functools.partial(_mla_up_kernel, scale=scale),
        grid=(n // tm,),
        in_specs=[pl.BlockSpec((tm, lora), lambda i: (i, 0)),
                  pl.BlockSpec((tm, lora), lambda i: (i, 1)),
                  pl.BlockSpec((tm, LANES), lambda i: (i, E_KRA)),
                  pl.BlockSpec((tm, LANES), lambda i: (i, E_KRB)),
                  pl.BlockSpec((1, lora), lambda i: (0, 0)),
                  pl.BlockSpec((1, lora), lambda i: (0, 0)),
                  pl.BlockSpec((lora, width), lambda i: (0, 0)),
                  pl.BlockSpec((lora, width), lambda i: (0, 0)),
                  pl.BlockSpec((tm, LANES), lambda i: (i % nt, 0)),
                  pl.BlockSpec((tm, LANES), lambda i: (i % nt, 0))],
        out_specs=[pl.BlockSpec((tm, width), lambda i: (i, 0)),
                   pl.BlockSpec((tm, width), lambda i: (i, 0)),
                   pl.BlockSpec((tm, MLA_H * LANES), lambda i: (i, 0))],
        out_shape=[jax.ShapeDtypeStruct((n, width), BF16),
                   jax.ShapeDtypeStruct((n, width), BF16),
                   jax.ShapeDtypeStruct((n, MLA_H * LANES), BF16)],
        scratch_shapes=[pltpu.VMEM((tm, lora), BF16), pltpu.VMEM((tm, lora), BF16)],
        compiler_params=_cparams(("parallel",), 4 * _nbytes((tm, lora), F32), 12 * _nbytes((tm, LANES), F32),
                                 4 * _nbytes((lora, width), BF16), 2 * _nbytes((tm, 5 * MLA_H * LANES), BF16),
                                 3 * _nbytes((tm, width), F32)),
        name="mla_up",
    )(proj, proj, proj, proj, qn.reshape(1, lora), kvn.reshape(1, lora), wq, wkv, cos_t, sin_t)


ATT_TK = 1024
ATT_SUB = 1024
LOG2E = 1.4426950408889634


def _attention_kernel(q_ref, k_ref, v_ref, o_ref):
    tq = q_ref.shape[1]
    t = k_ref.shape[1]
    for r0 in range(0, tq, ATT_SUB):
        q = q_ref[0, r0:r0 + ATT_SUB, :]
        m = l = acc = None
        for c0 in range(0, t, ATT_TK):
            s = lax.dot_general(q, k_ref[0, c0:c0 + ATT_TK, :], (((1,), (1,)), ((), ())),
                                preferred_element_type=F32)
            m_c = jnp.max(s, axis=-1, keepdims=True)
            m_new = m_c if m is None else jnp.maximum(m, m_c)
            p = jnp.exp2(s - m_new)
            pv = jnp.dot(p.astype(BF16), v_ref[0, c0:c0 + ATT_TK, :], preferred_element_type=F32)
            l_c = jnp.sum(p, axis=-1, keepdims=True)
            if m is None:
                l, acc = l_c, pv
            else:
                alpha = jnp.exp2(m - m_new)
                l, acc = alpha * l + l_c, alpha * acc + pv
            m = m_new
        o_ref[0, r0:r0 + ATT_SUB, :] = (acc / l).astype(o_ref.dtype)


def _attention(q, k, v, heads, kv_heads, tq=2048):
    b, t, _ = q.shape
    dq = q.shape[2] // heads
    dv = v.shape[2] // kv_heads
    g = heads // kv_heads
    tq = min(tq, t)
    return pl.pallas_call(
        _attention_kernel,
        grid=(b, heads, t // tq),
        in_specs=[pl.BlockSpec((1, tq, dq), lambda bi, h, qi: (bi, qi, h)),
                  pl.BlockSpec((1, t, dq), lambda bi, h, qi: (bi, 0, h // g)),
                  pl.BlockSpec((1, t, dv), lambda bi, h, qi: (bi, 0, h // g))],
        out_specs=pl.BlockSpec((1, tq, dv), lambda bi, h, qi: (bi, qi, h)),
        out_shape=jax.ShapeDtypeStruct((b, t, heads * dv), BF16),
        compiler_params=_cparams(("parallel", "parallel", "arbitrary"), 2 * _nbytes((tq, dq), BF16),
                                 2 * _nbytes((t, dq + dv), BF16), 2 * _nbytes((tq, dv), BF16),
                                 6 * _nbytes((ATT_SUB, ATT_TK), F32)),
        name="attention",
    )(q, k, v)


HG_CHUNK = 128
HG_DIRECT_CHUNK = 32
HG_MAX_LOG_RANGE = 80.0


def _split3(x):
    a = x.astype(BF16)
    r = x - a.astype(F32)
    b = r.astype(BF16)
    c = (r - b.astype(F32)).astype(BF16)
    return a, b, c


def _tri_masks(c_sz, reverse):
    row = lax.broadcasted_iota(jnp.int32, (c_sz, c_sz), 0)
    col = lax.broadcasted_iota(jnp.int32, (c_sz, c_sz), 1)
    keep = (row <= col) if reverse else (row >= col)
    return keep, jnp.where(keep, 1.0, 0.0).astype(BF16)


def _hgrn2_kernel(hq_ref, ff_ref, fb_ref, hi_ref, hg_ref, lbf_ref, lbb_ref, on_ref, o_ref,
                  accf_ref, accb_ref, sf_ref, sb_ref, dev_ref):
    t = hq_ref.shape[1]

    def chunk_inputs(f_ref, lb, rows):
        gate = lb + (1.0 - lb) * jax.nn.sigmoid(f_ref[0, rows, :])
        hq = hq_ref[0, rows, :]
        return hq * jax.nn.sigmoid(hq), 1.0 - gate, hi_ref[0, rows, :], jnp.log(gate)

    def log_decay(lf, tri):
        p0, p1, p2 = _split3(lf)
        return (jnp.dot(tri, p0, preferred_element_type=F32) + jnp.dot(tri, p1, preferred_element_type=F32)
                + jnp.dot(tri, p2, preferred_element_type=F32))

    def state_step(s_ref, q, k, v, b, b_last):
        s_t = s_ref[...]
        inter = lax.dot_general((q * jnp.exp(b)).astype(BF16), s_t.astype(BF16), (((1,), (1,)), ((), ())),
                                preferred_element_type=F32)
        kd = (k * jnp.exp(b_last - b)).astype(BF16)
        upd = lax.dot_general(v.astype(BF16), kd, (((0,), (0,)), ((), ())), preferred_element_type=F32)
        s_ref[...] = jnp.exp(b_last) * s_t + upd
        return inter

    def fast_step(f_ref, lb, s_ref, acc_ref, c, masks, reverse):
        c_sz = HG_CHUNK
        keep, tri = masks
        rows = pl.ds(pl.multiple_of(c * c_sz, c_sz), c_sz)
        q, k, v, lf = chunk_inputs(f_ref, lb, rows)
        b = log_decay(lf, tri)
        r = b[c_sz // 2:c_sz // 2 + 1, :]
        b_last = b[0:1, :] if reverse else b[c_sz - 1:c_sz, :]
        b_first = b[c_sz - 1:c_sz, :] if reverse else b[0:1, :]
        dev_ref[...] = jnp.maximum(dev_ref[...], jnp.maximum(jnp.abs(b_first - r), jnp.abs(b_last - r)))
        qe = (q * jnp.exp(b - r)).astype(BF16)
        ke = (k * jnp.exp(r - b)).astype(BF16)
        a = lax.dot_general(qe, ke, (((1,), (1,)), ((), ())), preferred_element_type=F32)
        a = jnp.where(keep, a, 0.0).astype(BF16)
        intra = jnp.dot(a, v.astype(BF16), preferred_element_type=F32)
        acc_ref[rows, :] = intra + state_step(s_ref, q, k, v, b, b_last)

    def direct_step(f_ref, lb, s_ref, acc_ref, c, masks, reverse):
        c_sz = HG_DIRECT_CHUNK
        _, tri = masks
        row1 = lax.broadcasted_iota(jnp.int32, (c_sz, 1), 0)
        rows = pl.ds(pl.multiple_of(c * c_sz, c_sz), c_sz)
        q, k, v, lf = chunk_inputs(f_ref, lb, rows)
        b = log_decay(lf, tri)
        b_last = b[0:1, :] if reverse else b[c_sz - 1:c_sz, :]
        o = state_step(s_ref, q, k, v, b, b_last)
        for s in range(c_sz):
            e = jnp.exp(jnp.minimum(b - b[s:s + 1, :], 0.0))
            a = jnp.sum(q * k[s:s + 1, :] * e, axis=-1, keepdims=True)
            a = jnp.where((row1 <= s) if reverse else (row1 >= s), a, 0.0)
            o = o + a * v[s:s + 1, :]
        acc_ref[rows, :] = o

    def scan_both(step, c_sz, unroll):
        nc = t // c_sz
        masks_f = _tri_masks(c_sz, False)
        masks_b = _tri_masks(c_sz, True)
        sf_ref[...] = jnp.zeros_like(sf_ref)
        sb_ref[...] = jnp.zeros_like(sb_ref)

        def body(ci, carry):
            step(ff_ref, lbf_ref[...], sf_ref, accf_ref, ci, masks_f, False)
            step(fb_ref, lbb_ref[...], sb_ref, accb_ref, nc - 1 - ci, masks_b, True)
            return carry

        lax.fori_loop(0, nc, body, 0, unroll=unroll)

    dev_ref[...] = jnp.zeros_like(dev_ref)
    scan_both(fast_step, HG_CHUNK, 16)

    @pl.when(jnp.logical_not(jnp.max(dev_ref[...]) <= HG_MAX_LOG_RANGE))
    def _():
        scan_both(direct_step, HG_DIRECT_CHUNK, 1)

    def finish(c, carry):
        rows = pl.ds(pl.multiple_of(c * 256, 256), 256)
        hg = hg_ref[0, rows, :]
        y = _rms_rows(accf_ref[rows, :] + accb_ref[rows, :], on_ref[...]) * (hg * jax.nn.sigmoid(hg))
        o_ref[0, rows, :] = y.astype(o_ref.dtype)
        return carry

    lax.fori_loop(0, t // 256, finish, 0, unroll=2)


def _hgrn2(proj, lb_f, lb_b, out_norm):
    b, t, _ = proj.shape

    def col(base):
        return pl.BlockSpec((1, t, LANES), lambda bi, h: (bi, 0, base + h))

    return pl.pallas_call(
        _hgrn2_kernel,
        grid=(b, HG_H),
        in_specs=[col(E_HQ), col(E_FF), col(E_FB), col(E_HI), col(E_HG),
                  pl.BlockSpec((1, LANES), lambda bi, h: (0, h)),
                  pl.BlockSpec((1, LANES), lambda bi, h: (0, h)),
                  pl.BlockSpec((1, LANES), lambda bi, h: (0, 0))],
        out_specs=pl.BlockSpec((1, t, LANES), lambda bi, h: (bi, 0, h)),
        out_shape=jax.ShapeDtypeStruct((b, t, HG_H * HG_DV), BF16),
        scratch_shapes=[pltpu.VMEM((t, HG_DV), F32), pltpu.VMEM((t, HG_DV), F32),
                        pltpu.VMEM((HG_DV, HG_DK), F32), pltpu.VMEM((HG_DV, HG_DK), F32), pltpu.VMEM((1, HG_DK), F32)],
        compiler_params=_cparams(("parallel", "parallel"), 10 * _nbytes((t, LANES), F32), 4 * _nbytes((t, LANES), F32)),
        name="hgrn2",
    )(proj, proj, proj, proj, proj, lb_f, lb_b, out_norm.reshape(1, HG_DV))


def _out_proj_kernel(a_ref, b_ref, wa_ref, wb_ref, *refs, bounds):
    x_refs, o_ref = refs[:-1], refs[-1]
    acc = jnp.dot(a_ref[...], wa_ref[...], preferred_element_type=F32)
    acc = acc + jnp.dot(b_ref[...], wb_ref[...], preferred_element_type=F32)
    for x_ref, bound in zip(x_refs, bounds):
        @pl.when(_in_part(pl.program_id(0), bound))
        def _():
            o_ref[...] = x_ref[...] + acc


def _out_proj(a, b, w, x_parts, tm=1024, tn=512):
    n, ka = a.shape
    d = w.shape[1]
    tm = min([tm] + [p.shape[0] for p in x_parts])
    x_specs, bounds = _row_parts(x_parts, tm, tn, lambda j: j)
    return pl.pallas_call(
        functools.partial(_out_proj_kernel, bounds=bounds),
        grid=(n // tm, d // tn),
        in_specs=[pl.BlockSpec((tm, ka), lambda i, j: (i, 0)),
                  pl.BlockSpec((tm, ka), lambda i, j: (i, 0)),
                  pl.BlockSpec((ka, tn), lambda i, j: (0, j)),
                  pl.BlockSpec((ka, tn), lambda i, j: (1, j))] + x_specs,
        out_specs=pl.BlockSpec((tm, tn), lambda i, j: (i, j)),
        out_shape=jax.ShapeDtypeStruct((n, d), F32),
        compiler_params=_cparams(("parallel", "arbitrary"), 4 * _nbytes((tm, ka), BF16), 4 * _nbytes((ka, tn), BF16),
                                 (3 + 2 * len(x_parts)) * _nbytes((tm, tn), F32)),
        name="out_proj",
    )(a, b, w, w, *x_parts)


def _ffn_kernel(x_ref, g_ref, wg_ref, wu_ref, wd_ref, o_ref, xn_ref):
    f = pl.program_id(1)

    @pl.when(f == 0)
    def _():
        _norm_into(xn_ref, x_ref, g_ref)
        o_ref[...] = x_ref[...]

    xn = xn_ref[...]
    gt = jnp.dot(xn, wg_ref[...], preferred_element_type=F32)
    up = jnp.dot(xn, wu_ref[...], preferred_element_type=F32)
    act = (gt * jax.nn.sigmoid(gt) * up).astype(BF16)
    o_ref[...] += jnp.dot(act, wd_ref[...], preferred_element_type=F32)


def _ffn(x, g, wg, wu, wd, tm=768, tf=512):
    n, d = x.shape
    ff = wg.shape[1]
    tm = min(tm, n)
    return pl.pallas_call(
        _ffn_kernel,
        grid=(n // tm, ff // tf),
        in_specs=[pl.BlockSpec((tm, d), lambda i, f: (i, 0)),
                  pl.BlockSpec((1, d), lambda i, f: (0, 0)),
                  pl.BlockSpec((d, tf), lambda i, f: (0, f)),
                  pl.BlockSpec((d, tf), lambda i, f: (0, f)),
                  pl.BlockSpec((tf, d), lambda i, f: (f, 0))],
        out_specs=pl.BlockSpec((tm, d), lambda i, f: (i, 0)),
        out_shape=jax.ShapeDtypeStruct((n, d), F32),
        scratch_shapes=[pltpu.VMEM((tm, d), BF16)],
        compiler_params=_cparams(("parallel", "arbitrary"), 4 * _nbytes((tm, d), F32), _nbytes((tm, d), BF16),
                                 6 * _nbytes((d, tf), BF16), 4 * _nbytes((tm, tf), F32)),
        name="ffn",
    )(x, g.reshape(1, d), wg, wu, wd)


def _axial_prep_kernel(qd_ref, kd_ref, vd_ref, cq_ref, sq_ref, ck_ref, sk_ref, q_ref, k_ref, v_ref, *, scale):
    def rope(x_ref, c_ref, s_ref, h, mul):
        x = x_ref[:, h * LANES:(h + 1) * LANES]
        r = lax.rsqrt(jnp.mean(x * x, axis=-1, keepdims=True) + EPS)
        return ((x * c_ref[...] + pltpu.roll(x, LANES // 2, axis=1) * s_ref[...]) * (r * mul)).astype(BF16)

    for h in range(AX_H):
        q_ref[:, h * LANES:(h + 1) * LANES] = rope(qd_ref, cq_ref, sq_ref, h, scale)
    for h in range(AX_KV):
        k_ref[:, h * LANES:(h + 1) * LANES] = rope(kd_ref, ck_ref, sk_ref, h, 1.0)
    v_ref[...] = vd_ref[...].astype(BF16)


def _axial_prep(proj, cq, sq, ck, sk, seq, tm=512):
    n = proj.shape[0]
    tm = min(tm, seq)
    nt = seq // tm
    qw, kw = AX_H * AX_HD, AX_KV * AX_HD
    tab = pl.BlockSpec((tm, LANES), lambda i: (i % nt, 0))
    return pl.pallas_call(
        functools.partial(_axial_prep_kernel, scale=float(AX_HD ** -0.5) * LOG2E),
        grid=(n // tm,),
        in_specs=[pl.BlockSpec((tm, qw), lambda i: (i, O_QD * LANES // qw)),
                  pl.BlockSpec((tm, kw), lambda i: (i, O_KD * LANES // kw)),
                  pl.BlockSpec((tm, kw), lambda i: (i, O_VD * LANES // kw)),
                  tab, tab, tab, tab],
        out_specs=[pl.BlockSpec((tm, qw), lambda i: (i, 0)),
                   pl.BlockSpec((tm, kw), lambda i: (i, 0)),
                   pl.BlockSpec((tm, kw), lambda i: (i, 0))],
        out_shape=[jax.ShapeDtypeStruct((n, qw), BF16), jax.ShapeDtypeStruct((n, kw), BF16),
                   jax.ShapeDtypeStruct((n, kw), BF16)],
        compiler_params=_cparams(("parallel",), 4 * _nbytes((tm, qw), F32), 6 * _nbytes((tm, kw), F32),
                                 8 * _nbytes((tm, LANES), F32), 2 * _nbytes((tm, qw + 2 * kw), BF16)),
        name="axial_prep",
    )(proj, proj, proj, cq, sq, ck, sk)


WIN_BLOCK = 128


def _window_kernel(sink_ref, slope_ref, q_ref, k_ref, v_ref, o_ref, *, scale):
    t = q_ref.shape[1]
    wb = WIN_BLOCK
    span = 3 * wb
    pair = pl.program_id(1)
    low = lax.broadcasted_iota(jnp.int32, (wb, LANES), 1) < WIN_HD
    row = lax.broadcasted_iota(jnp.int32, (2 * wb, span), 0)
    delta = (row & (wb - 1)) - lax.broadcasted_iota(jnp.int32, (2 * wb, span), 1)
    top = lax.broadcasted_iota(jnp.int32, (2 * wb, 1), 0) < wb
    slope = jnp.where(top, slope_ref[2 * pair], slope_ref[2 * pair + 1]) * LOG2E
    sink = jnp.where(top, sink_ref[2 * pair], sink_ref[2 * pair + 1]) * LOG2E

    def body(qb, carry):
        start = pl.multiple_of(jnp.clip((qb - 1) * wb, 0, t - span), wb)
        kwin = k_ref[0, pl.ds(start, span), :].astype(BF16)
        vwin = v_ref[0, pl.ds(start, span), :].astype(BF16)
        qrows = pl.ds(pl.multiple_of(qb * wb, wb), wb)
        q2 = q_ref[0, qrows, :] * (scale * LOG2E)
        qs = jnp.concatenate([jnp.where(low, q2, 0.0), jnp.where(low, 0.0, q2)], axis=0).astype(BF16)
        dist = jnp.abs(delta + (qb * wb - start))
        s = lax.dot_general(qs, kwin, (((1,), (1,)), ((), ())), preferred_element_type=F32)
        s = jnp.where(dist <= WINDOW, s - slope * dist.astype(F32), -jnp.inf)
        m = jnp.maximum(jnp.max(s, axis=-1, keepdims=True), sink)
        e = jnp.exp2(s - m)
        den = jnp.sum(e, axis=-1, keepdims=True) + jnp.exp2(sink - m)
        o = jnp.dot(e.astype(BF16), vwin, preferred_element_type=F32) / den
        o_ref[0, qrows, :] = jnp.where(low, o[:wb], o[wb:]).astype(o_ref.dtype)
        return carry

    lax.fori_loop(0, t // wb, body, 0, unroll=8)


def _window_attention(proj, sink, slopes):
    b, t, _ = proj.shape
    pairs = WIN_H // 2
    per_kv = pairs // WIN_KV
    smem = pl.BlockSpec(memory_space=pltpu.SMEM)
    return pl.pallas_call(
        functools.partial(_window_kernel, scale=float(WIN_HD ** -0.5)),
        grid=(b, pairs),
        in_specs=[smem, smem,
                  pl.BlockSpec((1, t, LANES), lambda bi, p: (bi, 0, O_QC + p)),
                  pl.BlockSpec((1, t, LANES), lambda bi, p: (bi, 0, O_KC + p // per_kv)),
                  pl.BlockSpec((1, t, LANES), lambda bi, p: (bi, 0, O_VC + p // per_kv))],
        out_specs=pl.BlockSpec((1, t, LANES), lambda bi, p: (bi, 0, p)),
        out_shape=jax.ShapeDtypeStruct((b, t, WIN_H * WIN_HD), BF16),
        compiler_params=_cparams(("parallel", "parallel"), 6 * _nbytes((t, LANES), F32), 2 * _nbytes((t, LANES), BF16)),
        name="window_attention",
    )(sink, slopes, proj, proj, proj)


MOE_TM = 512
R_E1, R_E2, R_W1, R_W2, R_RANK1, R_RANK2 = range(6)


def _router_kernel(x_ref, g_ref, r_ref, route_ref, cnt_ref, carry_ref):
    @pl.when(pl.program_id(0) == 0)
    def _():
        carry_ref[...] = jnp.zeros_like(carry_ref)

    tm = x_ref.shape[0]
    xn = _rms_rows(x_ref[...], g_ref[...])
    logits = [jnp.sum(xn * r_ref[e:e + 1, :], axis=-1, keepdims=True) for e in range(N_EXPERTS)]

    def top(ls):
        m = functools.reduce(jnp.maximum, ls)
        idx = jnp.full_like(m, N_EXPERTS).astype(jnp.int32)
        for e in reversed(range(N_EXPERTS)):
            idx = jnp.where(ls[e] == m, e, idx)
        return m, idx

    m1, i1 = top(logits)
    m2, i2 = top([jnp.where(i1 == e, -jnp.inf, logits[e]) for e in range(N_EXPERTS)])
    e2 = jnp.exp(m2 - m1)
    w1 = 1.0 / (1.0 + e2)
    w2 = e2 / (1.0 + e2)
    lane = lax.broadcasted_iota(jnp.int32, route_ref.shape, 1)
    sel1, sel2 = lane == i1, lane == i2
    onehot = jnp.where(jnp.logical_or(sel1, sel2), 1.0, 0.0)
    row = lax.broadcasted_iota(jnp.int32, (tm, tm), 0)
    col = lax.broadcasted_iota(jnp.int32, (tm, tm), 1)
    earlier = jnp.where(row > col, 1.0, 0.0).astype(BF16)
    before = carry_ref[...] + jnp.dot(earlier, onehot.astype(BF16), preferred_element_type=F32)
    rank1 = jnp.sum(jnp.where(sel1, before, 0.0), axis=-1, keepdims=True)
    rank2 = jnp.sum(jnp.where(sel2, before, 0.0), axis=-1, keepdims=True)
    carry_ref[...] = carry_ref[...] + jnp.sum(onehot, axis=0, keepdims=True)
    cnt_ref[...] = carry_ref[...]
    rec = jnp.zeros(route_ref.shape, F32)
    for pos, val in ((R_E1, i1.astype(F32)), (R_E2, i2.astype(F32)), (R_W1, w1), (R_W2, w2),
                     (R_RANK1, rank1), (R_RANK2, rank2)):
        rec = jnp.where(lane == pos, val, rec)
    route_ref[...] = rec


def _router(x, g, router_t, tm=256):
    n, d = x.shape
    tm = min(tm, n)
    return pl.pallas_call(
        _router_kernel,
        grid=(n // tm,),
        in_specs=[pl.BlockSpec((tm, d), lambda i: (i, 0)),
                  pl.BlockSpec((1, d), lambda i: (0, 0)),
                  pl.BlockSpec((N_EXPERTS, d), lambda i: (0, 0))],
        out_specs=[pl.BlockSpec((tm, LANES), lambda i: (i, 0)),
                   pl.BlockSpec((1, LANES), lambda i: (0, 0))],
        out_shape=[jax.ShapeDtypeStruct((n, LANES), F32), jax.ShapeDtypeStruct((1, LANES), F32)],
        scratch_shapes=[pltpu.VMEM((1, LANES), F32)],
        compiler_params=_cparams(("arbitrary",), 6 * _nbytes((tm, d), F32)),
        name="router",
    )(x, g.reshape(1, d), router_t)


def _row_copy(src_hbm, src_row, dst, dst_row, sem):
    return pltpu.make_async_copy(src_hbm.at[pl.ds(src_row, 1)], dst.at[pl.ds(dst_row, 1)], sem)


def _moe_scatter_kernel(fill_ref, p1_ref, p2_ref, x_ref, xs_hbm, zero_ref, sem, zsem, *, tile):
    i = pl.program_id(0)
    rows = p1_ref.shape[-1]

    @pl.when(i == 0)
    def _():
        zero_ref[...] = jnp.zeros_like(zero_ref)
        for k in range(fill_ref.shape[0]):
            @pl.when(fill_ref[k] >= 0)
            def _():
                start = pl.multiple_of(fill_ref[k], tile)
                fill = pltpu.make_async_copy(zero_ref, xs_hbm.at[pl.ds(start, tile)], zsem)
                fill.start()
                fill.wait()

    def issue(r, carry):
        _row_copy(x_ref, r, xs_hbm, p1_ref[0, 0, r], sem).start()
        _row_copy(x_ref, r, xs_hbm, p2_ref[0, 0, r], sem).start()
        return carry

    lax.fori_loop(0, rows, issue, 0, unroll=8)
    for _ in range(2):
        pltpu.make_async_copy(x_ref, xs_hbm.at[pl.ds(0, rows)], sem).wait()


def _moe_scatter(x, pos1, pos2, fill_rows, total_rows, tile, rows=256):
    n, d = x.shape
    rows = min(rows, n)
    smem_blk = pl.BlockSpec((1, 1, rows), lambda i, fill: (i, 0, 0), memory_space=pltpu.SMEM)
    return pl.pallas_call(
        functools.partial(_moe_scatter_kernel, tile=tile),
        grid_spec=pltpu.PrefetchScalarGridSpec(
            num_scalar_prefetch=1,
            grid=(n // rows,),
            in_specs=[smem_blk, smem_blk, pl.BlockSpec((rows, d), lambda i, fill: (i, 0))],
            out_specs=pl.BlockSpec(memory_space=pl.ANY),
            scratch_shapes=[pltpu.VMEM((tile, d), x.dtype), pltpu.SemaphoreType.DMA, pltpu.SemaphoreType.DMA]),
        out_shape=jax.ShapeDtypeStruct((total_rows, d), x.dtype),
        compiler_params=_cparams(("arbitrary",), _nbytes((tile, d), x.dtype), 2 * _nbytes((rows, d), x.dtype)),
        name="moe_scatter",
    )(fill_rows, pos1.reshape(n // rows, 1, rows), pos2.reshape(n // rows, 1, rows), x)


def _moe_experts_kernel(te_ref, nv_ref, xs_ref, g_ref, wg_ref, wu_ref, wd_ref, y_ref, xn_ref):
    valid = pl.program_id(0) < nv_ref[0]

    @pl.when(valid)
    def _():
        xn_ref[...] = _rms_rows(xs_ref[...], g_ref[...]).astype(BF16)
        xn = xn_ref[...]
        gt = jnp.dot(xn, wg_ref[0], preferred_element_type=F32)
        up = jnp.dot(xn, wu_ref[0], preferred_element_type=F32)
        act = (gt * jax.nn.sigmoid(gt) * up).astype(BF16)
        y_ref[...] = jnp.dot(act, wd_ref[0], preferred_element_type=F32)

    @pl.when(jnp.logical_not(valid))
    def _():
        y_ref[...] = jnp.zeros_like(y_ref)


def _moe_experts(xs, g, wg, wu, wd, tile_expert, n_valid, tile):
    rows, d = xs.shape
    ff = wg.shape[2]
    once = dict(pipeline_mode=pl.Buffered(1))

    def row_blk(i, te, nv):
        return (jnp.minimum(i, nv[0] - 1), 0)

    def w_blk(i, te, nv):
        return (te[i], 0, 0)

    return pl.pallas_call(
        _moe_experts_kernel,
        grid_spec=pltpu.PrefetchScalarGridSpec(
            num_scalar_prefetch=2,
            grid=(rows // tile,),
            in_specs=[pl.BlockSpec((tile, d), row_blk),
                      pl.BlockSpec((1, d), lambda i, te, nv: (0, 0)),
                      pl.BlockSpec((1, d, ff), w_blk, **once),
                      pl.BlockSpec((1, d, ff), w_blk, **once),
                      pl.BlockSpec((1, ff, d), w_blk, **once)],
            out_specs=pl.BlockSpec((tile, d), lambda i, te, nv: (i, 0)),
            scratch_shapes=[pltpu.VMEM((tile, d), BF16)]),
        out_shape=jax.ShapeDtypeStruct((rows, d), F32),
        compiler_params=_cparams(("arbitrary",), 4 * _nbytes((tile, d), F32), _nbytes((tile, d), BF16),
                                 3 * _nbytes((d, ff), BF16), 4 * _nbytes((tile, ff), F32), _nbytes((tile, d), F32)),
        name="moe_experts",
    )(tile_expert, n_valid, xs, g.reshape(1, d), wg, wu, wd)


def _moe_combine_kernel(p1_ref, p2_ref, x_ref, route_ref, g_ref, y_hbm, o_ref, ya_ref, yb_ref, sem, *, final):
    rows = x_ref.shape[0]

    def issue(r, carry):
        _row_copy(y_hbm, p1_ref[0, 0, r], ya_ref, r, sem).start()
        _row_copy(y_hbm, p2_ref[0, 0, r], yb_ref, r, sem).start()
        return carry

    lax.fori_loop(0, rows, issue, 0, unroll=8)
    for dst in (ya_ref, yb_ref):
        pltpu.make_async_copy(y_hbm.at[pl.ds(0, rows)], dst, sem).wait()
    lane = lax.broadcasted_iota(jnp.int32, route_ref.shape, 1)
    w1 = jnp.sum(jnp.where(lane == R_W1, route_ref[...], 0.0), axis=-1, keepdims=True)
    w2 = jnp.sum(jnp.where(lane == R_W2, route_ref[...], 0.0), axis=-1, keepdims=True)
    out = x_ref[...] + w1 * ya_ref[...] + w2 * yb_ref[...]
    o_ref[...] = _rms_rows(out, g_ref[...]) if final else out


def _moe_combine(x, route, pos1, pos2, y, gain, row0, nrows, final, rows=256):
    n, d = x.shape
    rows = min(rows, nrows)
    off = row0 // rows
    smem_blk = pl.BlockSpec((1, 1, rows), lambda i: (i + off, 0, 0), memory_space=pltpu.SMEM)
    return pl.pallas_call(
        functools.partial(_moe_combine_kernel, final=final),
        grid=(nrows // rows,),
        in_specs=[smem_blk, smem_blk,
                  pl.BlockSpec((rows, d), lambda i: (i + off, 0)),
                  pl.BlockSpec((rows, LANES), lambda i: (i + off, 0)),
                  pl.BlockSpec((1, d), lambda i: (0, 0)),
                  pl.BlockSpec(memory_space=pl.ANY)],
        out_specs=pl.BlockSpec((rows, d), lambda i: (i, 0)),
        out_shape=jax.ShapeDtypeStruct((nrows, d), F32),
        scratch_shapes=[pltpu.VMEM((rows, d), F32), pltpu.VMEM((rows, d), F32), pltpu.SemaphoreType.DMA],
        compiler_params=_cparams(("arbitrary",), 8 * _nbytes((rows, d), F32)),
        name="moe_combine",
    )(pos1.reshape(n // rows, 1, rows), pos2.reshape(n // rows, 1, rows), x, route, gain.reshape(1, d), y)


def _swap_halves(w, width):
    lead = w.shape[:-1]
    return jnp.flip(w.reshape(lead + (-1, 2, width // 2)), axis=-2).reshape(w.shape)


def _pad_cols(w, width):
    return jnp.pad(w, [(0, 0)] * (w.ndim - 1) + [(0, width - w.shape[-1])])


def _even_in_weight(w):
    kr = w[:, 2 * MLA_LORA:2 * MLA_LORA + MLA_ROPE]
    return jnp.concatenate([w[:, :2 * MLA_LORA], _pad_cols(kr, LANES), _pad_cols(_swap_halves(kr, MLA_ROPE), LANES),
                            w[:, 2 * MLA_LORA + MLA_ROPE:]], axis=1).astype(BF16)


def _mla_q_weight(w):
    w = w.reshape(MLA_LORA, MLA_H, MLA_NOPE + MLA_ROPE)
    rope = w[..., MLA_NOPE:]
    w = jnp.concatenate([w[..., :MLA_NOPE], _pad_cols(rope, LANES)], -1)
    return w.reshape(MLA_LORA, MLA_H * 2 * LANES).astype(BF16)


def _pairs_apart(w):
    quarter = AX_HD // 4
    g = w.reshape(w.shape[:-1] + (-1, 4, quarter))
    return g[..., jnp.array([0, 2, 1, 3]), :].reshape(w.shape)


def _odd_in_weight(w):
    d = w.shape[0]
    o = 0
    parts = {}
    for name, width in (("qc", WIN_H * WIN_HD), ("kc", WIN_KV * WIN_HD), ("vc", WIN_KV * WIN_HD),
                        ("qd", AX_H * AX_HD), ("kd", AX_KV * AX_HD), ("vd", AX_KV * AX_HD)):
        parts[name] = w[:, o:o + width]
        o += width

    def dup(x):
        x = x.reshape(d, WIN_KV, WIN_HD)
        return jnp.concatenate([x, x], axis=-1).reshape(d, WIN_KV * LANES)

    out = jnp.concatenate([_pairs_apart(parts["qd"]), parts["qc"], _pairs_apart(parts["kd"]), parts["vd"],
                           dup(parts["kc"]), dup(parts["vc"])], axis=1)
    assert out.shape[1] == O_END * LANES
    return out.astype(BF16)


def _rope_tables(pos, dim):
    inv = ROPE_THETA ** (-jnp.arange(0, dim, 2, dtype=F32) / dim)
    ang = pos[:, None] * inv[None, :]
    cos, sin = jnp.cos(ang), jnp.sin(ang)
    return jnp.concatenate([cos, cos], -1), jnp.concatenate([-sin, sin], -1)


def _alibi_slopes(n):
    return jnp.asarray(2.0 ** (-8.0 * np.arange(1, n + 1) / n), dtype=F32)


def _mixer_even(x_parts, bsz, seq, j, norm_g, w_in, q_norm, w_uq, kv_norm, w_ukv, lb_fwd, lb_bwd, out_norm, w_out):
    n = bsz * seq
    proj = _norm_matmul(x_parts, norm_g, _even_in_weight(w_in), tn=10 * LANES, out_dtype=F32)
    cos, sin = _rope_tables(jnp.arange(seq, dtype=F32), MLA_ROPE)
    q, k, v = _mla_up(proj, q_norm, kv_norm, _mla_q_weight(w_uq), w_ukv.astype(BF16),
                      _pad_cols(cos, LANES), _pad_cols(sin, LANES), seq)
    o_a = _attention(q.reshape(bsz, seq, -1), k.reshape(bsz, seq, -1), v.reshape(bsz, seq, -1), MLA_H, MLA_H)

    def lower_bound(tab):
        return jnp.cumsum(jax.nn.softmax(tab.astype(F32), axis=0), axis=0)[j].reshape(1, HG_H * HG_DK)

    o_b = _hgrn2(proj.reshape(bsz, seq, -1), lower_bound(lb_fwd), lower_bound(lb_bwd), out_norm)
    return _out_proj(o_a.reshape(n, -1), o_b.reshape(n, -1), w_out.astype(BF16), x_parts)


def _mixer_odd(x_parts, bsz, seq, norm_g, w_in, sink, q_norm, k_norm, w_out):
    n = bsz * seq
    proj = _norm_matmul(x_parts, norm_g, _odd_in_weight(w_in), tn=8 * LANES, out_dtype=F32)
    o_c = _window_attention(proj.reshape(bsz, seq, -1), sink.astype(F32), _alibi_slopes(WIN_H))
    pos = jnp.arange(seq)
    half = AX_HD // 2
    c_row, s_row = _rope_tables((pos // GRID_W).astype(F32), half)
    c_col, s_col = _rope_tables((pos % GRID_W).astype(F32), half)
    cos = jnp.concatenate([c_row, c_col], -1)
    sin = jnp.concatenate([s_row, s_col], -1)

    def tables(g):
        g = g.astype(F32)
        return _pairs_apart(cos * g[None, :]), _pairs_apart(sin * _swap_halves(g, half)[None, :])

    cq, sq = tables(q_norm)
    ck, sk = tables(k_norm)
    q, k, v = _axial_prep(proj, cq, sq, ck, sk, seq)
    o_d = _attention(q.reshape(bsz, seq, -1), k.reshape(bsz, seq, -1), v.reshape(bsz, seq, -1), AX_H, AX_KV)
    return _out_proj(o_c.reshape(n, -1), o_d.reshape(n, -1), w_out.astype(BF16), x_parts)


def _moe(x, norm_g, router, w_gate, w_up, w_down, out_gain, out_splits):
    n = x.shape[0]
    tile = MOE_TM
    route, counts = _router(x, norm_g, router.astype(F32).T)
    counts = counts[0, :N_EXPERTS].astype(jnp.int32)
    padded = (counts + tile - 1) // tile * tile
    ends = jnp.cumsum(padded)
    starts = ends - padded
    e1, e2 = route[:, R_E1].astype(jnp.int32), route[:, R_E2].astype(jnp.int32)
    pos1 = starts[e1] + route[:, R_RANK1].astype(jnp.int32)
    pos2 = starts[e2] + route[:, R_RANK2].astype(jnp.int32)
    n_tiles = 2 * n // tile + N_EXPERTS
    n_valid = ends[-1] // tile
    tile_row = jnp.minimum(jnp.arange(n_tiles), n_valid - 1) * tile
    tile_expert = jnp.sum(tile_row[:, None] >= ends[None, :], axis=1).astype(jnp.int32)
    group_tail = jnp.where(padded > 0, ends - tile, -1)
    spare = (n_valid + jnp.arange(N_EXPERTS)) * tile
    fill_rows = jnp.concatenate([group_tail, jnp.where(spare < n_tiles * tile, spare, -1)]).astype(jnp.int32)
    xs = _moe_scatter(x, pos1, pos2, fill_rows, n_tiles * tile, tile)
    y = _moe_experts(xs, norm_g, w_gate.astype(BF16), w_up.astype(BF16), w_down.astype(BF16),
                     tile_expert, n_valid.reshape(1).astype(jnp.int32), tile)
    final = out_gain is not None
    gain = out_gain if final else jnp.ones((x.shape[1],), F32)
    return [_moe_combine(x, route, pos1, pos2, y, gain, row0, nrows, final) for row0, nrows in out_splits]


def _trunk(x_parts, bsz, seq, norm_mix_e, w_in_e, mla_q_norm, mla_w_uq, mla_kv_norm, mla_w_ukv, hg_lb_fwd, hg_lb_bwd,
           hg_out_norm, w_out_e, norm_ffn_e, ffn_w_gate, ffn_w_up, ffn_w_down, norm_mix_o, w_in_o, win_sink,
           ax_q_norm, ax_k_norm, w_out_o, norm_ffn_o, moe_router, moe_w_gate, moe_w_up, moe_w_down, final_norm,
           out_splits):
    assert DEPTH % 2 == 0
    for l in range(DEPTH):
        j = l // 2
        if l % 2 == 0:
            x = _mixer_even(x_parts, bsz, seq, j, norm_mix_e[j], w_in_e[j], mla_q_norm[j], mla_w_uq[j],
                            mla_kv_norm[j], mla_w_ukv[j], hg_lb_fwd, hg_lb_bwd, hg_out_norm[j], w_out_e[j])
            x = _ffn(x, norm_ffn_e[j], ffn_w_gate[j].astype(BF16), ffn_w_up[j].astype(BF16),
                     ffn_w_down[j].astype(BF16))
        else:
            x = _mixer_odd(x_parts, bsz, seq, norm_mix_o[j], w_in_o[j], win_sink[j], ax_q_norm[j], ax_k_norm[j],
                           w_out_o[j])
            last = l == DEPTH - 1
            outs = _moe(x, norm_ffn_o[j], moe_router[j], moe_w_gate[j], moe_w_up[j], moe_w_down[j],
                        final_norm if last else None, out_splits if last else [(0, x.shape[0])])
            if last:
                return outs
            x = outs[0]
        x_parts = (x,)


def kernel(x_prompt, x_sample, norm_mix_e, w_in_e, mla_q_norm, mla_w_uq, mla_kv_norm, mla_w_ukv, hg_lb_fwd, hg_lb_bwd, hg_out_norm, w_out_e, norm_ffn_e, ffn_w_gate, ffn_w_up, ffn_w_down, norm_mix_o, w_in_o, win_sink, ax_q_norm, ax_k_norm, w_out_o, norm_ffn_o, moe_router, moe_w_gate, moe_w_up, moe_w_down, final_norm):
    bp, seq, d = x_prompt.shape
    bs = x_sample.shape[0]
    assert x_sample.shape[1:] == (seq, d)
    y_prompt, y_sample = _trunk(
        (x_prompt.reshape(bp * seq, d), x_sample.reshape(bs * seq, d)), bp + bs, seq, norm_mix_e, w_in_e, mla_q_norm, mla_w_uq, mla_kv_norm, mla_w_ukv, hg_lb_fwd, hg_lb_bwd,
        hg_out_norm, w_out_e, norm_ffn_e, ffn_w_gate, ffn_w_up, ffn_w_down, norm_mix_o, w_in_o, win_sink, ax_q_norm,
        ax_k_norm, w_out_o, norm_ffn_o, moe_router, moe_w_gate, moe_w_up, moe_w_down, final_norm,
        [(0, bp * seq), (bp * seq, bs * seq)])
    return (y_prompt.reshape(bp, seq, d), y_sample.reshape(bs, seq, d))
```

```python
import functools

import jax
import jax.numpy as jnp
import numpy as np
from jax import lax
from jax.experimental import pallas as pl
from jax.experimental.pallas import tpu as pltpu

D_MODEL = 2048
DEPTH = 2
GRID_W = 64
EPS = 1e-6
ROPE_THETA = 10000.0

MLA_H = 8
MLA_NOPE = 128
MLA_ROPE = 64
MLA_V = 128
MLA_LORA = D_MODEL // 4

HG_H = 8
HG_DK = 128
HG_DV = 128

WIN_H = 16
WIN_KV = 2
WIN_HD = 64
WINDOW = 128

AX_H = 8
AX_KV = 2
AX_HD = 128

FF_DENSE = 5632
N_EXPERTS = 8
FF_EXPERT = 1408

LANES = 128
VMEM_CAP = 60000 * 1024
BF16 = jnp.bfloat16
F32 = jnp.float32

E_CQ, E_CKV, E_KRA, E_KRB, E_HQ, E_FF, E_FB, E_HI, E_HG, E_END = 0, 4, 8, 9, 10, 18, 26, 34, 42, 50
O_QD, O_QC, O_KD, O_VD, O_KC, O_VC, O_END = 0, 8, 16, 18, 20, 22, 24


def _cparams(sem, *block_bytes):
    need = int(sum(block_bytes)) + (6 << 20)
    return pltpu.CompilerParams(dimension_semantics=sem, vmem_limit_bytes=min(max(need, 16 << 20), VMEM_CAP))


def _nbytes(shape, dtype):
    return int(np.prod(shape)) * jnp.dtype(dtype).itemsize


def _rms_rows(x, g):
    return x * lax.rsqrt(jnp.mean(x * x, axis=-1, keepdims=True) + EPS) * g


def _norm_into(dst_ref, x_ref, g_ref, chunk=256):
    rows = x_ref.shape[0]
    chunk = min(chunk, rows)

    def body(c, carry):
        r = pl.ds(pl.multiple_of(c * chunk, chunk), chunk)
        dst_ref[r, :] = _rms_rows(x_ref[r, :].astype(F32), g_ref[...]).astype(dst_ref.dtype)
        return carry

    lax.fori_loop(0, rows // chunk, body, 0)


def _row_parts(parts, tm, width, col):
    specs, bounds, start = [], [], 0
    for p in parts:
        nt = p.shape[0] // tm

        def index(i, j, start=start, nt=nt):
            inside = jnp.logical_and(i >= start, i < start + nt)
            return (jnp.clip(i - start, 0, nt - 1), jnp.where(inside, col(j), 0))

        specs.append(pl.BlockSpec((tm, width), index))
        bounds.append((start, start + nt))
        start += nt
    return specs, bounds


def _in_part(i, bound):
    return jnp.logical_and(i >= bound[0], i < bound[1])


def _norm_matmul_kernel(*refs, bounds):
    x_refs = refs[:len(bounds)]
    g_ref, w_ref, o_ref, xn_ref = refs[len(bounds):]
    for x_ref, bound in zip(x_refs, bounds):
        @pl.when(jnp.logical_and(pl.program_id(1) == 0, _in_part(pl.program_id(0), bound)))
        def _():
            _norm_into(xn_ref, x_ref, g_ref)

    o_ref[...] = jnp.dot(xn_ref[...], w_ref[...], preferred_element_type=F32).astype(o_ref.dtype)


def _norm_matmul(parts, g, w, tn, out_dtype):
    k = parts[0].shape[1]
    n = sum(p.shape[0] for p in parts)
    nout = w.shape[1]

    def resident(tm):
        return (2 * len(parts) * _nbytes((tm, k), F32), _nbytes((tm, k), BF16), 2 * _nbytes((k, tn), BF16),
                3 * _nbytes((tm, tn), F32))

    rows = min(p.shape[0] for p in parts)
    tm = next(t for t in (1024, 512, 256) if t <= rows and sum(resident(t)) + (8 << 20) <= VMEM_CAP)
    x_specs, bounds = _row_parts(parts, tm, k, lambda j: 0)
    return pl.pallas_call(
        functools.partial(_norm_matmul_kernel, bounds=bounds),
        grid=(n // tm, nout // tn),
        in_specs=x_specs + [pl.BlockSpec((1, k), lambda i, j: (0, 0)),
                            pl.BlockSpec((k, tn), lambda i, j: (0, j))],
        out_specs=pl.BlockSpec((tm, tn), lambda i, j: (i, j)),
        out_shape=jax.ShapeDtypeStruct((n, nout), out_dtype),
        scratch_shapes=[pltpu.VMEM((tm, k), BF16)],
        compiler_params=_cparams(("parallel", "arbitrary"), *resident(tm)),
        name="norm_matmul",
    )(*parts, g.reshape(1, k), w)


def _partner32(x):
    lane = lax.broadcasted_iota(jnp.int32, x.shape, 1)
    return jnp.where((lane & 32) == 0, pltpu.roll(x, LANES - 32, axis=1), pltpu.roll(x, 32, axis=1))


def _mla_up_kernel(cq_ref, ckv_ref, kra_ref, krb_ref, qn_ref, kvn_ref, wq_ref, wkv_ref, cos_ref, sin_ref,
                   q_ref, k_ref, v_ref, cqn_ref, ckvn_ref, *, scale):
    _norm_into(cqn_ref, cq_ref, qn_ref)
    _norm_into(ckvn_ref, ckv_ref, kvn_ref)
    cos, sin = cos_ref[...], sin_ref[...]
    k_rope = (kra_ref[...] * cos + krb_ref[...] * sin).astype(BF16)
    q = jnp.dot(cqn_ref[...], wq_ref[...], preferred_element_type=F32)
    kv = jnp.dot(ckvn_ref[...], wkv_ref[...], preferred_element_type=F32)
    for h in range(MLA_H):
        lo, mid, hi = 2 * h * LANES, (2 * h + 1) * LANES, (2 * h + 2) * LANES
        q_r = q[:, mid:hi]
        q_ref[:, lo:mid] = (q[:, lo:mid] * scale).astype(BF16)
        q_ref[:, mid:hi] = ((q_r * cos + _partner32(q_r) * sin) * scale).astype(BF16)
        k_ref[:, lo:mid] = kv[:, lo:mid].astype(BF16)
        k_ref[:, mid:hi] = k_rope
        v_ref[:, h * LANES:(h + 1) * LANES] = kv[:, mid:hi].astype(BF16)


def _mla_up(proj, qn, kvn, wq, wkv, cos_t, sin_t, seq, tm=512):
    n = proj.shape[0]
    tm = min(tm, seq)
    nt = seq // tm
    lora = MLA_LORA
    width = MLA_H * 2 * LANES
    scale = float((MLA_NOPE + MLA_ROPE) ** -0.5) * LOG2E
    return pl.pallas_call(
        functools.partial(_mla_up_kernel, scale=scale),
        grid=(n // tm,),
        in_specs=[pl.BlockSpec((tm, lora), lambda i: (i, 0)),
                  pl.BlockSpec((tm, lora), lambda i: (i, 1)),
                  pl.BlockSpec((tm, LANES), lambda i: (i, E_KRA)),
                  pl.BlockSpec((tm, LANES), lambda i: (i, E_KRB)),
                  pl.BlockSpec((1, lora), lambda i: (0, 0)),
                  pl.BlockSpec((1, lora), lambda i: (0, 0)),
                  pl.BlockSpec((lora, width), lambda i: (0, 0)),
                  pl.BlockSpec((lora, width), lambda i: (0, 0)),
                  pl.BlockSpec((tm, LANES), lambda i: (i % nt, 0)),
                  pl.BlockSpec((tm, LANES), lambda i: (i % nt, 0))],
        out_specs=[pl.BlockSpec((tm, width), lambda i: (i, 0)),
                   pl.BlockSpec((tm, width), lambda i: (i, 0)),
                   pl.BlockSpec((tm, MLA_H * LANES), lambda i: (i, 0))],
        out_shape=[jax.ShapeDtypeStruct((n, width), BF16),
                   jax.ShapeDtypeStruct((n, width), BF16),
                   jax.ShapeDtypeStruct((n, MLA_H * LANES), BF16)],
        scratch_shapes=[pltpu.VMEM((tm, lora), BF16), pltpu.VMEM((tm, lora), BF16)],
        compiler_params=_cparams(("parallel",), 4 * _nbytes((tm, lora), F32), 12 * _nbytes((tm, LANES), F32),
                                 4 * _nbytes((lora, width), BF16), 2 * _nbytes((tm, 5 * MLA_H * LANES), BF16),
                                 3 * _nbytes((tm, width), F32)),
        name="mla_up",
    )(proj, proj, proj, proj, qn.reshape(1, lora), kvn.reshape(1, lora), wq, wkv, cos_t, sin_t)


ATT_TK = 1024
ATT_SUB = 1024
LOG2E = 1.4426950408889634


def _attention_kernel(q_ref, k_ref, v_ref, o_ref):
    tq = q_ref.shape[1]
    t = k_ref.shape[1]
    for r0 in range(0, tq, ATT_SUB):
        q = q_ref[0, r0:r0 + ATT_SUB, :]
        m = l = acc = None
        for c0 in range(0, t, ATT_TK):
            s = lax.dot_general(q, k_ref[0, c0:c0 + ATT_TK, :], (((1,), (1,)), ((), ())),
                                preferred_element_type=F32)
            m_c = jnp.max(s, axis=-1, keepdims=True)
            m_new = m_c if m is None else jnp.maximum(m, m_c)
            p = jnp.exp2(s - m_new)
            pv = jnp.dot(p.astype(BF16), v_ref[0, c0:c0 + ATT_TK, :], preferred_element_type=F32)
            l_c = jnp.sum(p, axis=-1, keepdims=True)
            if m is None:
                l, acc = l_c, pv
            else:
                alpha = jnp.exp2(m - m_new)
                l, acc = alpha * l + l_c, alpha * acc + pv
            m = m_new
        o_ref[0, r0:r0 + ATT_SUB, :] = (acc / l).astype(o_ref.dtype)


def _attention(q, k, v, heads, kv_heads, tq=2048):
    b, t, _ = q.shape
    dq = q.shape[2] // heads
    dv = v.shape[2] // kv_heads
    g = heads // kv_heads
    tq = min(tq, t)
    return pl.pallas_call(
        _attention_kernel,
        grid=(b, heads, t // tq),
        in_specs=[pl.BlockSpec((1, tq, dq), lambda bi, h, qi: (bi, qi, h)),
                  pl.BlockSpec((1, t, dq), lambda bi, h, qi: (bi, 0, h // g)),
                  pl.BlockSpec((1, t, dv), lambda bi, h, qi: (bi, 0, h // g))],
        out_specs=pl.BlockSpec((1, tq, dv), lambda bi, h, qi: (bi, qi, h)),
        out_shape=jax.ShapeDtypeStruct((b, t, heads * dv), BF16),
        compiler_params=_cparams(("parallel", "parallel", "arbitrary"), 2 * _nbytes((tq, dq), BF16),
                                 2 * _nbytes((t, dq + dv), BF16), 2 * _nbytes((tq, dv), BF16),
                                 6 * _nbytes((ATT_SUB, ATT_TK), F32)),
        name="attention",
    )(q, k, v)


HG_CHUNK = 128
HG_DIRECT_CHUNK = 32
HG_MAX_LOG_RANGE = 80.0


def _split3(x):
    a = x.astype(BF16)
    r = x - a.astype(F32)
    b = r.astype(BF16)
    c = (r - b.astype(F32)).astype(BF16)
    return a, b, c


def _tri_masks(c_sz, reverse):
    row = lax.broadcasted_iota(jnp.int32, (c_sz, c_sz), 0)
    col = lax.broadcasted_iota(jnp.int32, (c_sz, c_sz), 1)
    keep = (row <= col) if reverse else (row >= col)
    return keep, jnp.where(keep, 1.0, 0.0).astype(BF16)


def _hgrn2_kernel(hq_ref, ff_ref, fb_ref, hi_ref, hg_ref, lbf_ref, lbb_ref, on_ref, o_ref,
                  accf_ref, accb_ref, sf_ref, sb_ref, dev_ref):
    t = hq_ref.shape[1]

    def chunk_inputs(f_ref, lb, rows):
        gate = lb + (1.0 - lb) * jax.nn.sigmoid(f_ref[0, rows, :])
        hq = hq_ref[0, rows, :]
        return hq * jax.nn.sigmoid(hq), 1.0 - gate, hi_ref[0, rows, :], jnp.log(gate)

    def log_decay(lf, tri):
        p0, p1, p2 = _split3(lf)
        return (jnp.dot(tri, p0, preferred_element_type=F32) + jnp.dot(tri, p1, preferred_element_type=F32)
                + jnp.dot(tri, p2, preferred_element_type=F32))

    def state_step(s_ref, q, k, v, b, b_last):
        s_t = s_ref[...]
        inter = lax.dot_general((q * jnp.exp(b)).astype(BF16), s_t.astype(BF16), (((1,), (1,)), ((), ())),
                                preferred_element_type=F32)
        kd = (k * jnp.exp(b_last - b)).astype(BF16)
        upd = lax.dot_general(v.astype(BF16), kd, (((0,), (0,)), ((), ())), preferred_element_type=F32)
        s_ref[...] = jnp.exp(b_last) * s_t + upd
        return inter

    def fast_step(f_ref, lb, s_ref, acc_ref, c, masks, reverse):
        c_sz = HG_CHUNK
        keep, tri = masks
        rows = pl.ds(pl.multiple_of(c * c_sz, c_sz), c_sz)
        q, k, v, lf = chunk_inputs(f_ref, lb, rows)
        b = log_decay(lf, tri)
        r = b[c_sz // 2:c_sz // 2 + 1, :]
        b_last = b[0:1, :] if reverse else b[c_sz - 1:c_sz, :]
        b_first = b[c_sz - 1:c_sz, :] if reverse else b[0:1, :]
        dev_ref[...] = jnp.maximum(dev_ref[...], jnp.maximum(jnp.abs(b_first - r), jnp.abs(b_last - r)))
        qe = (q * jnp.exp(b - r)).astype(BF16)
        ke = (k * jnp.exp(r - b)).astype(BF16)
        a = lax.dot_general(qe, ke, (((1,), (1,)), ((), ())), preferred_element_type=F32)
        a = jnp.where(keep, a, 0.0).astype(BF16)
        intra = jnp.dot(a, v.astype(BF16), preferred_element_type=F32)
        acc_ref[rows, :] = intra + state_step(s_ref, q, k, v, b, b_last)

    def direct_step(f_ref, lb, s_ref, acc_ref, c, masks, reverse):
        c_sz = HG_DIRECT_CHUNK
        _, tri = masks
        row1 = lax.broadcasted_iota(jnp.int32, (c_sz, 1), 0)
        rows = pl.ds(pl.multiple_of(c * c_sz, c_sz), c_sz)
        q, k, v, lf = chunk_inputs(f_ref, lb, rows)
        b = log_decay(lf, tri)
        b_last = b[0:1, :] if reverse else b[c_sz - 1:c_sz, :]
        o = state_step(s_ref, q, k, v, b, b_last)
        for s in range(c_sz):
            e = jnp.exp(jnp.minimum(b - b[s:s + 1, :], 0.0))
            a = jnp.sum(q * k[s:s + 1, :] * e, axis=-1, keepdims=True)
            a = jnp.where((row1 <= s) if reverse else (row1 >= s), a, 0.0)
            o = o + a * v[s:s + 1, :]
        acc_ref[rows, :] = o

    def scan_both(step, c_sz, unroll):
        nc = t // c_sz
        masks_f = _tri_masks(c_sz, False)
        masks_b = _tri_masks(c_sz, True)
        sf_ref[...] = jnp.zeros_like(sf_ref)
        sb_ref[...] = jnp.zeros_like(sb_ref)

        def body(ci, carry):
            step(ff_ref, lbf_ref[...], sf_ref, accf_ref, ci, masks_f, False)
            step(fb_ref, lbb_ref[...], sb_ref, accb_ref, nc - 1 - ci, masks_b, True)
            return carry

        lax.fori_loop(0, nc, body, 0, unroll=unroll)

    dev_ref[...] = jnp.zeros_like(dev_ref)
    scan_both(fast_step, HG_CHUNK, 16)

    @pl.when(jnp.logical_not(jnp.max(dev_ref[...]) <= HG_MAX_LOG_RANGE))
    def _():
        scan_both(direct_step, HG_DIRECT_CHUNK, 1)

    def finish(c, carry):
        rows = pl.ds(pl.multiple_of(c * 256, 256), 256)
        hg = hg_ref[0, rows, :]
        y = _rms_rows(accf_ref[rows, :] + accb_ref[rows, :], on_ref[...]) * (hg * jax.nn.sigmoid(hg))
        o_ref[0, rows, :] = y.astype(o_ref.dtype)
        return carry

    lax.fori_loop(0, t // 256, finish, 0, unroll=2)


def _hgrn2(proj, lb_f, lb_b, out_norm):
    b, t, _ = proj.shape

    def col(base):
        return pl.BlockSpec((1, t, LANES), lambda bi, h: (bi, 0, base + h))

    return pl.pallas_call(
        _hgrn2_kernel,
        grid=(b, HG_H),
        in_specs=[col(E_HQ), col(E_FF), col(E_FB), col(E_HI), col(E_HG),
                  pl.BlockSpec((1, LANES), lambda bi, h: (0, h)),
                  pl.BlockSpec((1, LANES), lambda bi, h: (0, h)),
                  pl.BlockSpec((1, LANES), lambda bi, h: (0, 0))],
        out_specs=pl.BlockSpec((1, t, LANES), lambda bi, h: (bi, 0, h)),
        out_shape=jax.ShapeDtypeStruct((b, t, HG_H * HG_DV), BF16),
        scratch_shapes=[pltpu.VMEM((t, HG_DV), F32), pltpu.VMEM((t, HG_DV), F32),
                        pltpu.VMEM((HG_DV, HG_DK), F32), pltpu.VMEM((HG_DV, HG_DK), F32), pltpu.VMEM((1, HG_DK), F32)],
        compiler_params=_cparams(("parallel", "parallel"), 10 * _nbytes((t, LANES), F32), 4 * _nbytes((t, LANES), F32)),
        name="hgrn2",
    )(proj, proj, proj, proj, proj, lb_f, lb_b, out_norm.reshape(1, HG_DV))


def _out_proj_kernel(a_ref, b_ref, wa_ref, wb_ref, *refs, bounds):
    x_refs, o_ref = refs[:-1], refs[-1]
    acc = jnp.dot(a_ref[...], wa_ref[...], preferred_element_type=F32)
    acc = acc + jnp.dot(b_ref[...], wb_ref[...], preferred_element_type=F32)
    for x_ref, bound in zip(x_refs, bounds):
        @pl.when(_in_part(pl.program_id(0), bound))
        def _():
            o_ref[...] = x_ref[...] + acc


def _out_proj(a, b, w, x_parts, tm=1024, tn=512):
    n, ka = a.shape
    d = w.shape[1]
    tm = min([tm] + [p.shape[0] for p in x_parts])
    x_specs, bounds = _row_parts(x_parts, tm, tn, lambda j: j)
    return pl.pallas_call(
        functools.partial(_out_proj_kernel, bounds=bounds),
        grid=(n // tm, d // tn),
        in_specs=[pl.BlockSpec((tm, ka), lambda i, j: (i, 0)),
                  pl.BlockSpec((tm, ka), lambda i, j: (i, 0)),
                  pl.BlockSpec((ka, tn), lambda i, j: (0, j)),
                  pl.BlockSpec((ka, tn), lambda i, j: (1, j))] + x_specs,
        out_specs=pl.BlockSpec((tm, tn), lambda i, j: (i, j)),
        out_shape=jax.ShapeDtypeStruct((n, d), F32),
        compiler_params=_cparams(("parallel", "arbitrary"), 4 * _nbytes((tm, ka), BF16), 4 * _nbytes((ka, tn), BF16),
                                 (3 + 2 * len(x_parts)) * _nbytes((tm, tn), F32)),
        name="out_proj",
    )(a, b, w, w, *x_parts)


def _ffn_kernel(x_ref, g_ref, wg_ref, wu_ref, wd_ref, o_ref, xn_ref):
    f = pl.program_id(1)

    @pl.when(f == 0)
    def _():
        _norm_into(xn_ref, x_ref, g_ref)
        o_ref[...] = x_ref[...]

    xn = xn_ref[...]
    gt = jnp.dot(xn, wg_ref[...], preferred_element_type=F32)
    up = jnp.dot(xn, wu_ref[...], preferred_element_type=F32)
    act = (gt * jax.nn.sigmoid(gt) * up).astype(BF16)
    o_ref[...] += jnp.dot(act, wd_ref[...], preferred_element_type=F32)


def _ffn(x, g, wg, wu, wd, tm=768, tf=512):
    n, d = x.shape
    ff = wg.shape[1]
    tm = min(tm, n)
    return pl.pallas_call(
        _ffn_kernel,
        grid=(n // tm, ff // tf),
        in_specs=[pl.BlockSpec((tm, d), lambda i, f: (i, 0)),
                  pl.BlockSpec((1, d), lambda i, f: (0, 0)),
                  pl.BlockSpec((d, tf), lambda i, f: (0, f)),
                  pl.BlockSpec((d, tf), lambda i, f: (0, f)),
                  pl.BlockSpec((tf, d), lambda i, f: (f, 0))],
        out_specs=pl.BlockSpec((tm, d), lambda i, f: (i, 0)),
        out_shape=jax.ShapeDtypeStruct((n, d), F32),
        scratch_shapes=[pltpu.VMEM((tm, d), BF16)],
        compiler_params=_cparams(("parallel", "arbitrary"), 4 * _nbytes((tm, d), F32), _nbytes((tm, d), BF16),
                                 6 * _nbytes((d, tf), BF16), 4 * _nbytes((tm, tf), F32)),
        name="ffn",
    )(x, g.reshape(1, d), wg, wu, wd)


def _axial_prep_kernel(qd_ref, kd_ref, vd_ref, cq_ref, sq_ref, ck_ref, sk_ref, q_ref, k_ref, v_ref, *, scale):
    def rope(x_ref, c_ref, s_ref, h, mul):
        x = x_ref[:, h * LANES:(h + 1) * LANES]
        r = lax.rsqrt(jnp.mean(x * x, axis=-1, keepdims=True) + EPS)
        return ((x * c_ref[...] + pltpu.roll(x, LANES // 2, axis=1) * s_ref[...]) * (r * mul)).astype(BF16)

    for h in range(AX_H):
        q_ref[:, h * LANES:(h + 1) * LANES] = rope(qd_ref, cq_ref, sq_ref, h, scale)
    for h in range(AX_KV):
        k_ref[:, h * LANES:(h + 1) * LANES] = rope(kd_ref, ck_ref, sk_ref, h, 1.0)
    v_ref[...] = vd_ref[...].astype(BF16)


def _axial_prep(proj, cq, sq, ck, sk, seq, tm=512):
    n = proj.shape[0]
    tm = min(tm, seq)
    nt = seq // tm
    qw, kw = AX_H * AX_HD, AX_KV * AX_HD
    tab = pl.BlockSpec((tm, LANES), lambda i: (i % nt, 0))
    return pl.pallas_call(
        functools.partial(_axial_prep_kernel, scale=float(AX_HD ** -0.5) * LOG2E),
        grid=(n // tm,),
        in_specs=[pl.BlockSpec((tm, qw), lambda i: (i, O_QD * LANES // qw)),
                  pl.BlockSpec((tm, kw), lambda i: (i, O_KD * LANES // kw)),
                  pl.BlockSpec((tm, kw), lambda i: (i, O_VD * LANES // kw)),
                  tab, tab, tab, tab],
        out_specs=[pl.BlockSpec((tm, qw), lambda i: (i, 0)),
                   pl.BlockSpec((tm, kw), lambda i: (i, 0)),
                   pl.BlockSpec((tm, kw), lambda i: (i, 0))],
        out_shape=[jax.ShapeDtypeStruct((n, qw), BF16), jax.ShapeDtypeStruct((n, kw), BF16),
                   jax.ShapeDtypeStruct((n, kw), BF16)],
        compiler_params=_cparams(("parallel",), 4 * _nbytes((tm, qw), F32), 6 * _nbytes((tm, kw), F32),
                                 8 * _nbytes((tm, LANES), F32), 2 * _nbytes((tm, qw + 2 * kw), BF16)),
        name="axial_prep",
    )(proj, proj, proj, cq, sq, ck, sk)


WIN_BLOCK = 128


def _window_kernel(sink_ref, slope_ref, q_ref, k_ref, v_ref, o_ref, *, scale):
    t = q_ref.shape[1]
    wb = WIN_BLOCK
    span = 3 * wb
    pair = pl.program_id(1)
    low = lax.broadcasted_iota(jnp.int32, (wb, LANES), 1) < WIN_HD
    row = lax.broadcasted_iota(jnp.int32, (2 * wb, span), 0)
    delta = (row & (wb - 1)) - lax.broadcasted_iota(jnp.int32, (2 * wb, span), 1)
    top = lax.broadcasted_iota(jnp.int32, (2 * wb, 1), 0) < wb
    slope = jnp.where(top, slope_ref[2 * pair], slope_ref[2 * pair + 1]) * LOG2E
    sink = jnp.where(top, sink_ref[2 * pair], sink_ref[2 * pair + 1]) * LOG2E

    def body(qb, carry):
        start = pl.multiple_of(jnp.clip((qb - 1) * wb, 0, t - span), wb)
        kwin = k_ref[0, pl.ds(start, span), :].astype(BF16)
        vwin = v_ref[0, pl.ds(start, span), :].astype(BF16)
        qrows = pl.ds(pl.multiple_of(qb * wb, wb), wb)
        q2 = q_ref[0, qrows, :] * (scale * LOG2E)
        qs = jnp.concatenate([jnp.where(low, q2, 0.0), jnp.where(low, 0.0, q2)], axis=0).astype(BF16)
        dist = jnp.abs(delta + (qb * wb - start))
        s = lax.dot_general(qs, kwin, (((1,), (1,)), ((), ())), preferred_element_type=F32)
        s = jnp.where(dist <= WINDOW, s - slope * dist.astype(F32), -jnp.inf)
        m = jnp.maximum(jnp.max(s, axis=-1, keepdims=True), sink)
        e = jnp.exp2(s - m)
        den = jnp.sum(e, axis=-1, keepdims=True) + jnp.exp2(sink - m)
        o = jnp.dot(e.astype(BF16), vwin, preferred_element_type=F32) / den
        o_ref[0, qrows, :] = jnp.where(low, o[:wb], o[wb:]).astype(o_ref.dtype)
        return carry

    lax.fori_loop(0, t // wb, body, 0, unroll=8)


def _window_attention(proj, sink, slopes):
    b, t, _ = proj.shape
    pairs = WIN_H // 2
    per_kv = pairs // WIN_KV
    smem = pl.BlockSpec(memory_space=pltpu.SMEM)
    return pl.pallas_call(
        functools.partial(_window_kernel, scale=float(WIN_HD ** -0.5)),
        grid=(b, pairs),
        in_specs=[smem, smem,
                  pl.BlockSpec((1, t, LANES), lambda bi, p: (bi, 0, O_QC + p)),
                  pl.BlockSpec((1, t, LANES), lambda bi, p: (bi, 0, O_KC + p // per_kv)),
                  pl.BlockSpec((1, t, LANES), lambda bi, p: (bi, 0, O_VC + p // per_kv))],
        out_specs=pl.BlockSpec((1, t, LANES), lambda bi, p: (bi, 0, p)),
        out_shape=jax.ShapeDtypeStruct((b, t, WIN_H * WIN_HD), BF16),
        compiler_params=_cparams(("parallel", "parallel"), 6 * _nbytes((t, LANES), F32), 2 * _nbytes((t, LANES), BF16)),
        name="window_attention",
    )(sink, slopes, proj, proj, proj)


MOE_TM = 512
R_E1, R_E2, R_W1, R_W2, R_RANK1, R_RANK2 = range(6)


def _router_kernel(x_ref, g_ref, r_ref, route_ref, cnt_ref, carry_ref):
    @pl.when(pl.program_id(0) == 0)
    def _():
        carry_ref[...] = jnp.zeros_like(carry_ref)

    tm = x_ref.shape[0]
    xn = _rms_rows(x_ref[...], g_ref[...])
    logits = [jnp.sum(xn * r_ref[e:e + 1, :], axis=-1, keepdims=True) for e in range(N_EXPERTS)]

    def top(ls):
        m = functools.reduce(jnp.maximum, ls)
        idx = jnp.full_like(m, N_EXPERTS).astype(jnp.int32)
        for e in reversed(range(N_EXPERTS)):
            idx = jnp.where(ls[e] == m, e, idx)
        return m, idx

    m1, i1 = top(logits)
    m2, i2 = top([jnp.where(i1 == e, -jnp.inf, logits[e]) for e in range(N_EXPERTS)])
    e2 = jnp.exp(m2 - m1)
    w1 = 1.0 / (1.0 + e2)
    w2 = e2 / (1.0 + e2)
    lane = lax.broadcasted_iota(jnp.int32, route_ref.shape, 1)
    sel1, sel2 = lane == i1, lane == i2
    onehot = jnp.where(jnp.logical_or(sel1, sel2), 1.0, 0.0)
    row = lax.broadcasted_iota(jnp.int32, (tm, tm), 0)
    col = lax.broadcasted_iota(jnp.int32, (tm, tm), 1)
    earlier = jnp.where(row > col, 1.0, 0.0).astype(BF16)
    before = carry_ref[...] + jnp.dot(earlier, onehot.astype(BF16), preferred_element_type=F32)
    rank1 = jnp.sum(jnp.where(sel1, before, 0.0), axis=-1, keepdims=True)
    rank2 = jnp.sum(jnp.where(sel2, before, 0.0), axis=-1, keepdims=True)
    carry_ref[...] = carry_ref[...] + jnp.sum(onehot, axis=0, keepdims=True)
    cnt_ref[...] = carry_ref[...]
    rec = jnp.zeros(route_ref.shape, F32)
    for pos, val in ((R_E1, i1.astype(F32)), (R_E2, i2.astype(F32)), (R_W1, w1), (R_W2, w2),
                     (R_RANK1, rank1), (R_RANK2, rank2)):
        rec = jnp.where(lane == pos, val, rec)
    route_ref[...] = rec


def _router(x, g, router_t, tm=256):
    n, d = x.shape
    tm = min(tm, n)
    return pl.pallas_call(
        _router_kernel,
        grid=(n // tm,),
        in_specs=[pl.BlockSpec((tm, d), lambda i: (i, 0)),
                  pl.BlockSpec((1, d), lambda i: (0, 0)),
                  pl.BlockSpec((N_EXPERTS, d), lambda i: (0, 0))],
        out_specs=[pl.BlockSpec((tm, LANES), lambda i: (i, 0)),
                   pl.BlockSpec((1, LANES), lambda i: (0, 0))],
        out_shape=[jax.ShapeDtypeStruct((n, LANES), F32), jax.ShapeDtypeStruct((1, LANES), F32)],
        scratch_shapes=[pltpu.VMEM((1, LANES), F32)],
        compiler_params=_cparams(("arbitrary",), 6 * _nbytes((tm, d), F32)),
        name="router",
    )(x, g.reshape(1, d), router_t)


def _row_copy(src_hbm, src_row, dst, dst_row, sem):
    return pltpu.make_async_copy(src_hbm.at[pl.ds(src_row, 1)], dst.at[pl.ds(dst_row, 1)], sem)


def _moe_scatter_kernel(fill_ref, p1_ref, p2_ref, x_ref, xs_hbm, zero_ref, sem, zsem, *, tile):
    i = pl.program_id(0)
    rows = p1_ref.shape[-1]

    @pl.when(i == 0)
    def _():
        zero_ref[...] = jnp.zeros_like(zero_ref)
        for k in range(fill_ref.shape[0]):
            @pl.when(fill_ref[k] >= 0)
            def _():
                start = pl.multiple_of(fill_ref[k], tile)
                fill = pltpu.make_async_copy(zero_ref, xs_hbm.at[pl.ds(start, tile)], zsem)
                fill.start()
                fill.wait()

    def issue(r, carry):
        _row_copy(x_ref, r, xs_hbm, p1_ref[0, 0, r], sem).start()
        _row_copy(x_ref, r, xs_hbm, p2_ref[0, 0, r], sem).start()
        return carry

    lax.fori_loop(0, rows, issue, 0, unroll=8)
    for _ in range(2):
        pltpu.make_async_copy(x_ref, xs_hbm.at[pl.ds(0, rows)], sem).wait()


def _moe_scatter(x, pos1, pos2, fill_rows, total_rows, tile, rows=256):
    n, d = x.shape
    rows = min(rows, n)
    smem_blk = pl.BlockSpec((1, 1, rows), lambda i, fill: (i, 0, 0), memory_space=pltpu.SMEM)
    return pl.pallas_call(
        functools.partial(_moe_scatter_kernel, tile=tile),
        grid_spec=pltpu.PrefetchScalarGridSpec(
            num_scalar_prefetch=1,
            grid=(n // rows,),
            in_specs=[smem_blk, smem_blk, pl.BlockSpec((rows, d), lambda i, fill: (i, 0))],
            out_specs=pl.BlockSpec(memory_space=pl.ANY),
            scratch_shapes=[pltpu.VMEM((tile, d), x.dtype), pltpu.SemaphoreType.DMA, pltpu.SemaphoreType.DMA]),
        out_shape=jax.ShapeDtypeStruct((total_rows, d), x.dtype),
        compiler_params=_cparams(("arbitrary",), _nbytes((tile, d), x.dtype), 2 * _nbytes((rows, d), x.dtype)),
        name="moe_scatter",
    )(fill_rows, pos1.reshape(n // rows, 1, rows), pos2.reshape(n // rows, 1, rows), x)


def _moe_experts_kernel(te_ref, nv_ref, xs_ref, g_ref, wg_ref, wu_ref, wd_ref, y_ref, xn_ref):
    valid = pl.program_id(0) < nv_ref[0]

    @pl.when(valid)
    def _():
        xn_ref[...] = _rms_rows(xs_ref[...], g_ref[...]).astype(BF16)
        xn = xn_ref[...]
        gt = jnp.dot(xn, wg_ref[0], preferred_element_type=F32)
        up = jnp.dot(xn, wu_ref[0], preferred_element_type=F32)
        act = (gt * jax.nn.sigmoid(gt) * up).astype(BF16)
        y_ref[...] = jnp.dot(act, wd_ref[0], preferred_element_type=F32)

    @pl.when(jnp.logical_not(valid))
    def _():
        y_ref[...] = jnp.zeros_like(y_ref)


def _moe_experts(xs, g, wg, wu, wd, tile_expert, n_valid, tile):
    rows, d = xs.shape
    ff = wg.shape[2]
    once = dict(pipeline_mode=pl.Buffered(1))

    def row_blk(i, te, nv):
        return (jnp.minimum(i, nv[0] - 1), 0)

    def w_blk(i, te, nv):
        return (te[i], 0, 0)

    return pl.pallas_call(
        _moe_experts_kernel,
        grid_spec=pltpu.PrefetchScalarGridSpec(
            num_scalar_prefetch=2,
            grid=(rows // tile,),
            in_specs=[pl.BlockSpec((tile, d), row_blk),
                      pl.BlockSpec((1, d), lambda i, te, nv: (0, 0)),
                      pl.BlockSpec((1, d, ff), w_blk, **once),
                      pl.BlockSpec((1, d, ff), w_blk, **once),
                      pl.BlockSpec((1, ff, d), w_blk, **once)],
            out_specs=pl.BlockSpec((tile, d), lambda i, te, nv: (i, 0)),
            scratch_shapes=[pltpu.VMEM((tile, d), BF16)]),
        out_shape=jax.ShapeDtypeStruct((rows, d), F32),
        compiler_params=_cparams(("arbitrary",), 4 * _nbytes((tile, d), F32), _nbytes((tile, d), BF16),
                                 3 * _nbytes((d, ff), BF16), 4 * _nbytes((tile, ff), F32), _nbytes((tile, d), F32)),
        name="moe_experts",
    )(tile_expert, n_valid, xs, g.reshape(1, d), wg, wu, wd)


def _moe_combine_kernel(p1_ref, p2_ref, n1_ref, n2_ref, x_ref, route_ref, g_ref, y_hbm, o_ref, ya_ref, yb_ref, sem,
                        *, final):
    i = pl.program_id(0)
    rows = x_ref.shape[0]
    slot = i % 2

    def gather(i1_ref, i2_ref, s):
        def issue(r, carry):
            _row_copy(y_hbm, i1_ref[0, 0, r], ya_ref.at[s], r, sem.at[s]).start()
            _row_copy(y_hbm, i2_ref[0, 0, r], yb_ref.at[s], r, sem.at[s]).start()
            return carry

        lax.fori_loop(0, rows, issue, 0, unroll=8)

    @pl.when(i == 0)
    def _():
        gather(p1_ref, p2_ref, 0)

    @pl.when(i + 1 < pl.num_programs(0))
    def _():
        gather(n1_ref, n2_ref, 1 - slot)

    for dst in (ya_ref, yb_ref):
        pltpu.make_async_copy(y_hbm.at[pl.ds(0, rows)], dst.at[slot], sem.at[slot]).wait()
    lane = lax.broadcasted_iota(jnp.int32, route_ref.shape, 1)
    w1 = jnp.sum(jnp.where(lane == R_W1, route_ref[...], 0.0), axis=-1, keepdims=True)
    w2 = jnp.sum(jnp.where(lane == R_W2, route_ref[...], 0.0), axis=-1, keepdims=True)
    out = x_ref[...] + w1 * ya_ref[slot] + w2 * yb_ref[slot]
    o_ref[...] = _rms_rows(out, g_ref[...]) if final else out


def _moe_combine(x, route, pos1, pos2, y, gain, row0, nrows, final, rows=256):
    n, d = x.shape
    rows = min(rows, nrows)
    off = row0 // rows
    steps = nrows // rows
    smem_blk = pl.BlockSpec((1, 1, rows), lambda i: (i + off, 0, 0), memory_space=pltpu.SMEM)
    next_blk = pl.BlockSpec((1, 1, rows), lambda i: (jnp.minimum(i + 1, steps - 1) + off, 0, 0),
                            memory_space=pltpu.SMEM)
    pos1, pos2 = pos1.reshape(n // rows, 1, rows), pos2.reshape(n // rows, 1, rows)
    return pl.pallas_call(
        functools.partial(_moe_combine_kernel, final=final),
        grid=(steps,),
        in_specs=[smem_blk, smem_blk, next_blk, next_blk,
                  pl.BlockSpec((rows, d), lambda i: (i + off, 0)),
                  pl.BlockSpec((rows, LANES), lambda i: (i + off, 0)),
                  pl.BlockSpec((1, d), lambda i: (0, 0)),
                  pl.BlockSpec(memory_space=pl.ANY)],
        out_specs=pl.BlockSpec((rows, d), lambda i: (i, 0)),
        out_shape=jax.ShapeDtypeStruct((nrows, d), F32),
        scratch_shapes=[pltpu.VMEM((2, rows, d), F32), pltpu.VMEM((2, rows, d), F32), pltpu.SemaphoreType.DMA((2,))],
        compiler_params=_cparams(("arbitrary",), 10 * _nbytes((rows, d), F32)),
        name="moe_combine",
    )(pos1, pos2, pos1, pos2, x, route, gain.reshape(1, d), y)


def _swap_halves(w, width):
    lead = w.shape[:-1]
    return jnp.flip(w.reshape(lead + (-1, 2, width // 2)), axis=-2).reshape(w.shape)


def _pad_cols(w, width):
    return jnp.pad(w, [(0, 0)] * (w.ndim - 1) + [(0, width - w.shape[-1])])


def _even_in_weight(w):
    kr = w[:, 2 * MLA_LORA:2 * MLA_LORA + MLA_ROPE]
    return jnp.concatenate([w[:, :2 * MLA_LORA], _pad_cols(kr, LANES), _pad_cols(_swap_halves(kr, MLA_ROPE), LANES),
                            w[:, 2 * MLA_LORA + MLA_ROPE:]], axis=1).astype(BF16)


def _mla_q_weight(w):
    w = w.reshape(MLA_LORA, MLA_H, MLA_NOPE + MLA_ROPE)
    rope = w[..., MLA_NOPE:]
    w = jnp.concatenate([w[..., :MLA_NOPE], _pad_cols(rope, LANES)], -1)
    return w.reshape(MLA_LORA, MLA_H * 2 * LANES).astype(BF16)


def _pairs_apart(w):
    quarter = AX_HD // 4
    g = w.reshape(w.shape[:-1] + (-1, 4, quarter))
    return g[..., jnp.array([0, 2, 1, 3]), :].reshape(w.shape)


def _odd_in_weight(w):
    d = w.shape[0]
    o = 0
    parts = {}
    for name, width in (("qc", WIN_H * WIN_HD), ("kc", WIN_KV * WIN_HD), ("vc", WIN_KV * WIN_HD),
                        ("qd", AX_H * AX_HD), ("kd", AX_KV * AX_HD), ("vd", AX_KV * AX_HD)):
        parts[name] = w[:, o:o + width]
        o += width

    def dup(x):
        x = x.reshape(d, WIN_KV, WIN_HD)
        return jnp.concatenate([x, x], axis=-1).reshape(d, WIN_KV * LANES)

    out = jnp.concatenate([_pairs_apart(parts["qd"]), parts["qc"], _pairs_apart(parts["kd"]), parts["vd"],
                           dup(parts["kc"]), dup(parts["vc"])], axis=1)
    assert out.shape[1] == O_END * LANES
    return out.astype(BF16)


def _rope_tables(pos, dim):
    inv = ROPE_THETA ** (-jnp.arange(0, dim, 2, dtype=F32) / dim)
    ang = pos[:, None] * inv[None, :]
    cos, sin = jnp.cos(ang), jnp.sin(ang)
    return jnp.concatenate([cos, cos], -1), jnp.concatenate([-sin, sin], -1)


def _alibi_slopes(n):
    return jnp.asarray(2.0 ** (-8.0 * np.arange(1, n + 1) / n), dtype=F32)


def _mixer_even(x_parts, bsz, seq, j, norm_g, w_in, q_norm, w_uq, kv_norm, w_ukv, lb_fwd, lb_bwd, out_norm, w_out):
    n = bsz * seq
    proj = _norm_matmul(x_parts, norm_g, _even_in_weight(w_in), tn=10 * LANES, out_dtype=F32)
    cos, sin = _rope_tables(jnp.arange(seq, dtype=F32), MLA_ROPE)
    q, k, v = _mla_up(proj, q_norm, kv_norm, _mla_q_weight(w_uq), w_ukv.astype(BF16),
                      _pad_cols(cos, LANES), _pad_cols(sin, LANES), seq)
    o_a = _attention(q.reshape(bsz, seq, -1), k.reshape(bsz, seq, -1), v.reshape(bsz, seq, -1), MLA_H, MLA_H)

    def lower_bound(tab):
        return jnp.cumsum(jax.nn.softmax(tab.astype(F32), axis=0), axis=0)[j].reshape(1, HG_H * HG_DK)

    o_b = _hgrn2(proj.reshape(bsz, seq, -1), lower_bound(lb_fwd), lower_bound(lb_bwd), out_norm)
    return _out_proj(o_a.reshape(n, -1), o_b.reshape(n, -1), w_out.astype(BF16), x_parts)


def _mixer_odd(x_parts, bsz, seq, norm_g, w_in, sink, q_norm, k_norm, w_out):
    n = bsz * seq
    proj = _norm_matmul(x_parts, norm_g, _odd_in_weight(w_in), tn=8 * LANES, out_dtype=F32)
    o_c = _window_attention(proj.reshape(bsz, seq, -1), sink.astype(F32), _alibi_slopes(WIN_H))
    pos = jnp.arange(seq)
    half = AX_HD // 2
    c_row, s_row = _rope_tables((pos // GRID_W).astype(F32), half)
    c_col, s_col = _rope_tables((pos % GRID_W).astype(F32), half)
    cos = jnp.concatenate([c_row, c_col], -1)
    sin = jnp.concatenate([s_row, s_col], -1)

    def tables(g):
        g = g.astype(F32)
        return _pairs_apart(cos * g[None, :]), _pairs_apart(sin * _swap_halves(g, half)[None, :])

    cq, sq = tables(q_norm)
    ck, sk = tables(k_norm)
    q, k, v = _axial_prep(proj, cq, sq, ck, sk, seq)
    o_d = _attention(q.reshape(bsz, seq, -1), k.reshape(bsz, seq, -1), v.reshape(bsz, seq, -1), AX_H, AX_KV)
    return _out_proj(o_c.reshape(n, -1), o_d.reshape(n, -1), w_out.astype(BF16), x_parts)


def _moe(x, norm_g, router, w_gate, w_up, w_down, out_gain, out_splits):
    n = x.shape[0]
    tile = MOE_TM
    route, counts = _router(x, norm_g, router.astype(F32).T)
    counts = counts[0, :N_EXPERTS].astype(jnp.int32)
    padded = (counts + tile - 1) // tile * tile
    ends = jnp.cumsum(padded)
    starts = ends - padded
    e1, e2 = route[:, R_E1].astype(jnp.int32), route[:, R_E2].astype(jnp.int32)
    pos1 = starts[e1] + route[:, R_RANK1].astype(jnp.int32)
    pos2 = starts[e2] + route[:, R_RANK2].astype(jnp.int32)
    n_tiles = 2 * n // tile + N_EXPERTS
    n_valid = ends[-1] // tile
    tile_row = jnp.minimum(jnp.arange(n_tiles), n_valid - 1) * tile
    tile_expert = jnp.sum(tile_row[:, None] >= ends[None, :], axis=1).astype(jnp.int32)
    group_tail = jnp.where(padded > 0, ends - tile, -1)
    spare = (n_valid + jnp.arange(N_EXPERTS)) * tile
    fill_rows = jnp.concatenate([group_tail, jnp.where(spare < n_tiles * tile, spare, -1)]).astype(jnp.int32)
    xs = _moe_scatter(x, pos1, pos2, fill_rows, n_tiles * tile, tile)
    y = _moe_experts(xs, norm_g, w_gate.astype(BF16), w_up.astype(BF16), w_down.astype(BF16),
                     tile_expert, n_valid.reshape(1).astype(jnp.int32), tile)
    final = out_gain is not None
    gain = out_gain if final else jnp.ones((x.shape[1],), F32)
    return [_moe_combine(x, route, pos1, pos2, y, gain, row0, nrows, final) for row0, nrows in out_splits]


def _trunk(x_parts, bsz, seq, norm_mix_e, w_in_e, mla_q_norm, mla_w_uq, mla_kv_norm, mla_w_ukv, hg_lb_fwd, hg_lb_bwd,
           hg_out_norm, w_out_e, norm_ffn_e, ffn_w_gate, ffn_w_up, ffn_w_down, norm_mix_o, w_in_o, win_sink,
           ax_q_norm, ax_k_norm, w_out_o, norm_ffn_o, moe_router, moe_w_gate, moe_w_up, moe_w_down, final_norm,
           out_splits):
    assert DEPTH % 2 == 0
    for l in range(DEPTH):
        j = l // 2
        if l % 2 == 0:
            x = _mixer_even(x_parts, bsz, seq, j, norm_mix_e[j], w_in_e[j], mla_q_norm[j], mla_w_uq[j],
                            mla_kv_norm[j], mla_w_ukv[j], hg_lb_fwd, hg_lb_bwd, hg_out_norm[j], w_out_e[j])
            x = _ffn(x, norm_ffn_e[j], ffn_w_gate[j].astype(BF16), ffn_w_up[j].astype(BF16),
                     ffn_w_down[j].astype(BF16))
        else:
            x = _mixer_odd(x_parts, bsz, seq, norm_mix_o[j], w_in_o[j], win_sink[j], ax_q_norm[j], ax_k_norm[j],
                           w_out_o[j])
            last = l == DEPTH - 1
            outs = _moe(x, norm_ffn_o[j], moe_router[j], moe_w_gate[j], moe_w_up[j], moe_w_down[j],
                        final_norm if last else None, out_splits if last else [(0, x.shape[0])])
            if last:
                return outs
            x = outs[0]
        x_parts = (x,)


def kernel(x_prompt, x_sample, norm_mix_e, w_in_e, mla_q_norm, mla_w_uq, mla_kv_norm, mla_w_ukv, hg_lb_fwd, hg_lb_bwd, hg_out_norm, w_out_e, norm_ffn_e, ffn_w_gate, ffn_w_up, ffn_w_down, norm_mix_o, w_in_o, win_sink, ax_q_norm, ax_k_norm, w_out_o, norm_ffn_o, moe_router, moe_w_gate, moe_w_up, moe_w_down, final_norm):
    bp, seq, d = x_prompt.shape
    bs = x_sample.shape[0]
    assert x_sample.shape[1:] == (seq, d)
    y_prompt, y_sample = _trunk(
        (x_prompt.reshape(bp * seq, d), x_sample.reshape(bs * seq, d)), bp + bs, seq, norm_mix_e, w_in_e, mla_q_norm, mla_w_uq, mla_kv_norm, mla_w_ukv, hg_lb_fwd, hg_lb_bwd,
        hg_out_norm, w_out_e, norm_ffn_e, ffn_w_gate, ffn_w_up, ffn_w_down, norm_mix_o, w_in_o, win_sink, ax_q_norm,
        ax_k_norm, w_out_o, norm_ffn_o, moe_router, moe_w_gate, moe_w_up, moe_w_down, final_norm,
        [(0, bp * seq), (bp * seq, bs * seq)])
    return (y_prompt.reshape(bp, seq, d), y_sample.reshape(bs, seq, d))
```
